```python
import jax
import jax.numpy as jnp
from jax import lax
import numpy as np

D_MODEL = 2048
BATCH = 4
SEQ = 4096
DEPTH = 4

N_MIXERS = 4
MEM_LEN = 256
FFN_DIM = 5632
NORM_EPS = 1e-6
NEG_INF = -1e30
ROPE_THETA = 500000.0
CONV_WIDTH = 3
DIL_PATTERNS = ((128, 1), (512, 4), (2048, 16))
DIL_GROUPS = len(DIL_PATTERNS)
DIL_HEADS = 8
DIL_HEAD_DIM = 128
DIL_BLOCK = 128
HGRN_EXPAND = 128
HGRN_HEADS = D_MODEL // HGRN_EXPAND
HGRN_V_DIM = D_MODEL // HGRN_HEADS
HGRN_CHUNK = 16
RWKV_HEAD_DIM = 64
RWKV_HEADS = D_MODEL // RWKV_HEAD_DIM
RWKV_DECAY_LORA = 96
RWKV_A_LORA = 96
RWKV_GATE_LORA = 256
RWKV_GN_EPS = 64e-5
XATTN_HEADS = 4
XATTN_HEAD_DIM = D_MODEL // XATTN_HEADS
N_CONV_LAYERS = (DEPTH + N_MIXERS - 1) // N_MIXERS
N_DIL_LAYERS = (DEPTH + N_MIXERS - 2) // N_MIXERS
N_HGRN_LAYERS = (DEPTH + N_MIXERS - 3) // N_MIXERS
N_RWKV_LAYERS = DEPTH // N_MIXERS

kernel_name = 'hybrid_interleaved_trunk'


def rms_norm(x, gain):
    xf = x.astype(jnp.float32)
    y = xf * lax.rsqrt(jnp.mean(xf * xf, axis=-1, keepdims=True) + NORM_EPS)
    return (y * gain.astype(jnp.float32)).astype(x.dtype)


def swiglu(h, w_gate, w_up, w_down):
    return (jax.nn.silu(h @ w_gate) * (h @ w_up)) @ w_down


def partial_rotary(x, positions):
    rot = x.shape[-1] // 4
    half = rot // 2
    inv_freq = ROPE_THETA ** (-jnp.arange(0, rot, 2, dtype=jnp.float32) / rot)
    ang = positions.astype(jnp.float32)[..., None] * inv_freq
    ang = ang.reshape(ang.shape[:2] + (1,) * (x.ndim - 3) + (half,))
    cos, sin = jnp.cos(ang), jnp.sin(ang)
    x1 = x[..., :half].astype(jnp.float32)
    x2 = x[..., half:rot].astype(jnp.float32)
    rotated = jnp.concatenate([x1 * cos - x2 * sin, x2 * cos + x1 * sin], axis=-1).astype(x.dtype)
    return jnp.concatenate([rotated, x[..., rot:]], axis=-1)


def short_conv_mixer(h, w_in, conv_w, w_out):
    b_gate, c_gate, u = jnp.split(h @ w_in, 3, axis=-1)
    y = lax.conv_general_dilated(c_gate * u, conv_w[:, None, :], window_strides=(1,),
                                 padding=[(CONV_WIDTH - 1, 0)],
                                 dimension_numbers=('NWC', 'WIO', 'NWC'),
                                 feature_group_count=h.shape[-1])
    return (b_gate * y) @ w_out


def dilated_group_attention(q, k, v, dilation, n_back):
    B, S, H, Dh = q.shape
    L = S // dilation
    nb = -(-L // DIL_BLOCK)
    Lp = nb * DIL_BLOCK

    def to_blocks(a):
        a = a.reshape(B, L, dilation, H, Dh)
        a = jnp.pad(a, ((0, 0), (0, Lp - L), (0, 0), (0, 0), (0, 0)))
        return a.reshape(B, nb, DIL_BLOCK, dilation, H, Dh)

    def with_prev(a):
        prev = jnp.pad(a, ((0, 0), (1, 0), (0, 0), (0, 0), (0, 0), (0, 0)))[:, :-1]
        return jnp.concatenate([prev, a], axis=2)

    qb = to_blocks(q)
    kb = with_prev(to_blocks(k))
    vb = with_prev(to_blocks(v))
    s = jnp.einsum('bnqrhd,bnkrhd->bnrhqk', qb, kb, preferred_element_type=jnp.float32) * (Dh ** -0.5)
    qi = jnp.arange(DIL_BLOCK)[:, None]
    kj = jnp.arange(2 * DIL_BLOCK)[None, :]
    dist = DIL_BLOCK + qi - kj
    band = (dist >= 0) & (dist <= n_back)
    has_prev = (jnp.arange(nb) > 0)[:, None, None] | (kj >= DIL_BLOCK)[None]
    mask = band[None] & has_prev
    s = jnp.where(mask[None, :, None, None], s, NEG_INF)
    lse = jax.nn.logsumexp(s, axis=-1)
    p = jnp.exp(s - lse[..., None]).astype(v.dtype)
    o = jnp.einsum('bnrhqk,bnkrhd->bnqrhd', p, vb, preferred_element_type=jnp.float32)
    o = o.reshape(B, Lp, dilation, H, Dh)[:, :L].reshape(B, S, H, Dh)
    lse = lse.transpose(0, 1, 4, 2, 3).reshape(B, Lp, dilation, H)[:, :L].reshape(B, S, H)
    return o, lse


def dilated_attention_mixer(h, positions, w_qkv, q_gain, k_gain, w_out):
    B, S, _ = h.shape
    qkv = (h @ w_qkv).reshape(B, S, 3, DIL_GROUPS, DIL_HEADS, DIL_HEAD_DIM)
    q = partial_rotary(rms_norm(qkv[:, :, 0], q_gain[:, None, :]), positions)
    k = partial_rotary(rms_norm(qkv[:, :, 1], k_gain[:, None, :]), positions)
    v = qkv[:, :, 2]
    outs, lses = [], []
    for g, (window, dilation) in enumerate(DIL_PATTERNS):
        o, lse = dilated_group_attention(q[:, :, g], k[:, :, g], v[:, :, g], dilation, window // dilation)
        outs.append(o)
        lses.append(lse)
    alpha = jax.nn.softmax(jnp.stack(lses, axis=0), axis=0)
    o = jnp.sum(alpha[..., None] * jnp.stack(outs, axis=0), axis=0)
    return o.reshape(B, S, DIL_HEADS * DIL_HEAD_DIM).astype(h.dtype) @ w_out


def hgrn2_chunked(q, k, v, log_f):
    B, S, H, Dk = q.shape
    Dv = v.shape[-1]
    C = HGRN_CHUNK
    nc = S // C

    def chunks(a):
        return a.reshape(B, nc, C, H, a.shape[-1]).transpose(0, 3, 1, 2, 4)

    q, k, v, g = chunks(q), chunks(k), chunks(v), chunks(log_f)
    A = jnp.cumsum(g, axis=3)
    A_last = A[:, :, :, -1:, :]
    q_dec = q * jnp.exp(A)
    k_in = k * jnp.exp(-A)
    k_end = k * jnp.exp(A_last - A)
    causal = jnp.tril(jnp.ones((C, C), dtype=bool))
    att = jnp.where(causal, jnp.einsum('bhncd,bhnsd->bhncs', q_dec, k_in), 0.0)
    o_intra = jnp.einsum('bhncs,bhnse->bhnce', att, v)
    chunk_decay = jnp.exp(A_last[:, :, :, 0, :])

    def step(state, inp):
        qd, ke, vc, dec = inp
        o = jnp.einsum('bhcd,bhde->bhce', qd, state)
        state = state * dec[..., None] + jnp.einsum('bhcd,bhce->bhde', ke, vc)
        return state, o

    xs = (jnp.moveaxis(q_dec, 2, 0), jnp.moveaxis(k_end, 2, 0), jnp.moveaxis(v, 2, 0),
          jnp.moveaxis(chunk_decay, 2, 0))
    _, o_inter = lax.scan(step, jnp.zeros((B, H, Dk, Dv), jnp.float32), xs)
    o = o_intra + jnp.moveaxis(o_inter, 0, 2)
    return o.transpose(0, 2, 3, 1, 4).reshape(B, S, H, Dv)


def hgrn2_mixer(h, w_in, lower_bound, norm_gain, w_out):
    B, S, D = h.shape
    f32 = jnp.float32
    q, f, i, gate = jnp.split(h @ w_in, 4, axis=-1)
    forget = lower_bound + (1.0 - lower_bound) * jax.nn.sigmoid(f.astype(f32))

    def heads(t):
        return t.astype(f32).reshape(B, S, HGRN_HEADS, HGRN_EXPAND)

    o = hgrn2_chunked(heads(q), heads(1.0 - forget),
                      i.astype(f32).reshape(B, S, HGRN_HEADS, HGRN_V_DIM), heads(jnp.log(forget)))
    o = rms_norm(o, norm_gain).reshape(B, S, D) * jax.nn.silu(gate.astype(f32))
    return o.astype(h.dtype) @ w_out


def rwkv7_mixer(h, mu, w_rkv, w0, w1, w2, a0, a1, a2, g1, g2, k_k, k_a, r_k, ln_w, ln_b, w_out):
    B, S, D = h.shape
    H, N = RWKV_HEADS, RWKV_HEAD_DIM
    f32 = jnp.float32
    h_prev = jnp.pad(h, ((0, 0), (1, 0), (0, 0)))[:, :-1]
    mixed = h[None] + (h_prev - h)[None] * mu[:, None, None, :]
    r, k, v = jnp.einsum('nbsd,nde->nbse', mixed[:3], w_rkv)
    xw, xa, xg = mixed[3], mixed[4], mixed[5]
    w_log = -jax.nn.softplus(-(w0 + jnp.tanh(xw @ w1) @ w2).astype(f32)) - 0.5
    decay = jnp.exp(-jnp.exp(w_log))
    a = jax.nn.sigmoid((a0 + (xa @ a1) @ a2).astype(f32))
    g = jax.nn.sigmoid(xg @ g1) @ g2

    def heads(t):
        return t.astype(f32).reshape(B, S, H, N)

    r, k, v, decay, a = heads(r), heads(k), heads(v), heads(decay), heads(a)
    kk = k * k_k.astype(f32).reshape(H, N)
    kk = kk * lax.rsqrt(jnp.maximum(jnp.sum(kk * kk, axis=-1, keepdims=True), 1e-24))
    k = k * (1.0 + (a - 1.0) * k_a.astype(f32).reshape(H, N))

    def step(state, inp):
        r_t, w_t, k_t, v_t, kk_t, b_t = inp
        sa = jnp.einsum('bhvk,bhk->bhv', state, -kk_t)
        state = (state * w_t[:, :, None, :] + sa[..., None] * b_t[:, :, None, :]
                 + v_t[..., None] * k_t[:, :, None, :])
        return state, jnp.einsum('bhvk,bhk->bhv', state, r_t)

    xs = (jnp.moveaxis(r, 1, 0), jnp.moveaxis(decay, 1, 0), jnp.moveaxis(k, 1, 0),
          jnp.moveaxis(v, 1, 0), jnp.moveaxis(kk, 1, 0), jnp.moveaxis(kk * a, 1, 0))
    _, y = lax.scan(step, jnp.zeros((B, H, N, N), f32), xs)
    y = jnp.moveaxis(y, 0, 1)
    mean = jnp.mean(y, axis=-1, keepdims=True)
    var = jnp.mean(jnp.square(y - mean), axis=-1, keepdims=True)
    y = ((y - mean) * lax.rsqrt(var + RWKV_GN_EPS)).reshape(B, S, D) * ln_w.astype(f32) + ln_b.astype(f32)
    bonus = jnp.sum(r * k * r_k.astype(f32), axis=-1, keepdims=True) * v
    y = (y + bonus.reshape(B, S, D)) * g.astype(f32)
    return y.astype(h.dtype) @ w_out


def memory_cross_attention(h, memn, wq, wkv, wo, q_gain, k_gain):
    B, S, D = h.shape
    M = memn.shape[1]
    q = rms_norm((h @ wq).reshape(B, S, XATTN_HEADS, XATTN_HEAD_DIM), q_gain)
    kv = (memn @ wkv).reshape(B, M, 2, XATTN_HEADS, XATTN_HEAD_DIM)
    k = rms_norm(kv[:, :, 0], k_gain)
    v = kv[:, :, 1]
    s = jnp.einsum('bshd,bmhd->bhsm', q, k, preferred_element_type=jnp.float32) * (XATTN_HEAD_DIM ** -0.5)
    p = jax.nn.softmax(s, axis=-1).astype(v.dtype)
    o = jnp.einsum('bhsm,bmhd->bshd', p, v).reshape(B, S, D)
    return o @ wo


def setup_inputs(seed: int = 0) -> dict:
    key = jax.random.key(seed)
    keys = jax.random.split(key, 64)
    counter = iter(range(64))
    f32 = jnp.float32
    D, F = D_MODEL, FFN_DIM

    def nk():
        return keys[next(counter)]

    def dense(shape, fan_in, scale=1.0):
        return jax.random.normal(nk(), shape, f32) * (scale * fan_in ** -0.5)

    def gain(shape):
        return 1.0 + 0.02 * jax.random.normal(nk(), shape, f32)

    def noise(shape, scale):
        return scale * jax.random.normal(nk(), shape, f32)

    x = jax.random.normal(nk(), (BATCH, SEQ, D), f32)
    mem = jax.random.normal(nk(), (BATCH, MEM_LEN, D), f32)
    positions = (jax.random.randint(nk(), (BATCH, 1), 0, 1024) + jnp.arange(SEQ)[None, :]).astype(jnp.int32)
    qkv_cols = 3 * DIL_GROUPS * DIL_HEADS * DIL_HEAD_DIM
    return {
        'x': x,
        'mem': mem,
        'positions': positions,
        'ffn_norm': gain((DEPTH, 2, D)),
        'ffn_w_gate': dense((DEPTH, 2, D, F), D),
        'ffn_w_up': dense((DEPTH, 2, D, F), D),
        'ffn_w_down': dense((DEPTH, 2, F, D), F, 0.5),
        'mix_norm': gain((DEPTH, D)),
        'xattn_norm': gain((DEPTH, D)),
        'mem_norm': gain((DEPTH, D)),
        'xattn_wq': dense((DEPTH, D, D), D),
        'xattn_wkv': dense((DEPTH, D, 2 * D), D),
        'xattn_wo': dense((DEPTH, D, D), D, 0.5),
        'xattn_q_gain': gain((DEPTH, XATTN_HEAD_DIM)),
        'xattn_k_gain': gain((DEPTH, XATTN_HEAD_DIM)),
        'conv_w_in': dense((N_CONV_LAYERS, D, 3 * D), D),
        'conv_w': dense((N_CONV_LAYERS, CONV_WIDTH, D), CONV_WIDTH),
        'conv_w_out': dense((N_CONV_LAYERS, D, D), D, 0.5),
        'dil_w_qkv': dense((N_DIL_LAYERS, D, qkv_cols), D),
        'dil_q_gain': gain((N_DIL_LAYERS, DIL_GROUPS, DIL_HEAD_DIM)),
        'dil_k_gain': gain((N_DIL_LAYERS, DIL_GROUPS, DIL_HEAD_DIM)),
        'dil_w_out': dense((N_DIL_LAYERS, DIL_HEADS * DIL_HEAD_DIM, D), DIL_HEADS * DIL_HEAD_DIM, 0.5),
        'hgrn_w_in': dense((N_HGRN_LAYERS, D, 4 * D), D),
        'hgrn_lb_logits': noise((DEPTH, D), 0.3),
        'hgrn_norm': gain((N_HGRN_LAYERS, HGRN_V_DIM)),
        'hgrn_w_out': dense((N_HGRN_LAYERS, D, D), D, 0.5),
        'rwkv_mu': jax.random.uniform(nk(), (N_RWKV_LAYERS, 6, D), f32),
        'rwkv_w_rkv': dense((N_RWKV_LAYERS, 3, D, D), D),
        'rwkv_w0': jax.random.uniform(nk(), (N_RWKV_LAYERS, D), f32, -3.0, 1.0),
        'rwkv_w1': dense((N_RWKV_LAYERS, D, RWKV_DECAY_LORA), D, 0.5),
        'rwkv_w2': dense((N_RWKV_LAYERS, RWKV_DECAY_LORA, D), RWKV_DECAY_LORA, 0.5),
        'rwkv_a0': noise((N_RWKV_LAYERS, D), 0.1),
        'rwkv_a1': dense((N_RWKV_LAYERS, D, RWKV_A_LORA), D, 0.5),
        'rwkv_a2': dense((N_RWKV_LAYERS, RWKV_A_LORA, D), RWKV_A_LORA, 0.5),
        'rwkv_g1': dense((N_RWKV_LAYERS, D, RWKV_GATE_LORA), D),
        'rwkv_g2': dense((N_RWKV_LAYERS, RWKV_GATE_LORA, D), RWKV_GATE_LORA),
        'rwkv_k_k': 0.85 + noise((N_RWKV_LAYERS, D), 0.02),
        'rwkv_k_a': gain((N_RWKV_LAYERS, D)),
        'rwkv_r_k': noise((N_RWKV_LAYERS, RWKV_HEADS, RWKV_HEAD_DIM), 0.1),
        'rwkv_ln_w': gain((N_RWKV_LAYERS, D)),
        'rwkv_ln_b': noise((N_RWKV_LAYERS, D), 0.02),
        'rwkv_w_out': dense((N_RWKV_LAYERS, D, D), D, 0.5),
    }


def reference(x, mem, positions, ffn_norm, ffn_w_gate, ffn_w_up, ffn_w_down, mix_norm,
              xattn_norm, mem_norm, xattn_wq, xattn_wkv, xattn_wo, xattn_q_gain, xattn_k_gain,
              conv_w_in, conv_w, conv_w_out,
              dil_w_qkv, dil_q_gain, dil_k_gain, dil_w_out,
              hgrn_w_in, hgrn_lb_logits, hgrn_norm, hgrn_w_out,
              rwkv_mu, rwkv_w_rkv, rwkv_w0, rwkv_w1, rwkv_w2, rwkv_a0, rwkv_a1, rwkv_a2,
              rwkv_g1, rwkv_g2, rwkv_k_k, rwkv_k_a, rwkv_r_k, rwkv_ln_w, rwkv_ln_b, rwkv_w_out):
    lb_p = jax.nn.softmax(hgrn_lb_logits.astype(jnp.float32), axis=0)
    lower_bounds = jnp.cumsum(lb_p, axis=0) - lb_p[0]
    for i in range(DEPTH):
        kind, j = i % N_MIXERS, i // N_MIXERS
        x = x + 0.5 * swiglu(rms_norm(x, ffn_norm[i, 0]), ffn_w_gate[i, 0], ffn_w_up[i, 0], ffn_w_down[i, 0])
        h = rms_norm(x, mix_norm[i])
        if kind == 0:
            y = short_conv_mixer(h, conv_w_in[j], conv_w[j], conv_w_out[j])
        elif kind == 1:
            y = dilated_attention_mixer(h, positions, dil_w_qkv[j], dil_q_gain[j], dil_k_gain[j], dil_w_out[j])
        elif kind == 2:
            y = hgrn2_mixer(h, hgrn_w_in[j], lower_bounds[i], hgrn_norm[j], hgrn_w_out[j])
        else:
            y = rwkv7_mixer(h, rwkv_mu[j], rwkv_w_rkv[j], rwkv_w0[j], rwkv_w1[j], rwkv_w2[j],
                            rwkv_a0[j], rwkv_a1[j], rwkv_a2[j], rwkv_g1[j], rwkv_g2[j],
                            rwkv_k_k[j], rwkv_k_a[j], rwkv_r_k[j], rwkv_ln_w[j], rwkv_ln_b[j], rwkv_w_out[j])
        x = x + y
        x = x + memory_cross_attention(rms_norm(x, xattn_norm[i]), rms_norm(mem, mem_norm[i]),
                                       xattn_wq[i], xattn_wkv[i], xattn_wo[i], xattn_q_gain[i], xattn_k_gain[i])
        x = x + 0.5 * swiglu(rms_norm(x, ffn_norm[i, 1]), ffn_w_gate[i, 1], ffn_w_up[i, 1], ffn_w_down[i, 1])
    return x
```

```python
import functools

import jax
import jax.numpy as jnp
from jax import lax
from jax.experimental import pallas as pl
from jax.experimental.pallas import tpu as pltpu

F32 = jnp.float32
BF16 = jnp.bfloat16

D_MODEL = 2048
DEPTH = 4
N_MIXERS = 4
MEM_LEN = 256
FFN_DIM = 5632
NORM_EPS = 1e-6
NEG_INF = -1e30
ROPE_THETA = 500000.0
DIL_PATTERNS = ((128, 1), (512, 4), (2048, 16))
DIL_GROUPS = 3
DIL_HEADS = 8
DIL_HEAD_DIM = 128
DIL_BLOCK = 128
HGRN_CHUNK = 16
HGRN_HEADS = 16
RWKV_HEAD_DIM = 64
RWKV_HEADS = 32
RWKV_CHUNK = 64
RWKV_GN_EPS = 64e-5
XATTN_HEADS = 4
XATTN_HEAD_DIM = 512

V7X_LANES = 128
V7X_SUBLANES = 8
V7X_VMEM_BYTES = 64 * 2**20
V7X_VMEM_CAP = 56 * 2**20


def _cparams(sem, vmem_bytes):
    limit = min(int(vmem_bytes * 1.25) + (4 << 20), V7X_VMEM_CAP)
    return pltpu.CompilerParams(dimension_semantics=sem, vmem_limit_bytes=limit)


def _rms(x, gain):
    return x * lax.rsqrt(jnp.mean(x * x, axis=-1, keepdims=True) + NORM_EPS) * gain


def _dot(a, b):
    return jnp.dot(a, b, preferred_element_type=F32)


def _dot_nt(a, b):
    return lax.dot_general(a, b, (((1,), (1,)), ((), ())), preferred_element_type=F32)


def _dot_tn(a, b):
    return lax.dot_general(a, b, (((0,), (0,)), ((), ())), preferred_element_type=F32)


def _dot_exact(a, b):
    return jnp.dot(a, b, precision=lax.Precision.HIGHEST, preferred_element_type=F32)


def _chunk_of(idx, chunk):
    return jnp.right_shift(idx, chunk.bit_length() - 1)


def _silu(x):
    return x * jax.nn.sigmoid(x)


def _norm_matmul_body(x_ref, g_ref, w_ref, o_ref, h_scr):
    @pl.when(pl.program_id(1) == 0)
    def _():
        h_scr[...] = _rms(x_ref[...], g_ref[...]).astype(BF16)

    o_ref[...] = _dot(h_scr[...], w_ref[...]).astype(o_ref.dtype)


def norm_matmul(x, gain, w, *, tm=512, tn=512, out_dtype=F32):
    M, K = x.shape
    N = w.shape[1]
    tm, tn = min(tm, M), min(tn, N)
    ob = jnp.dtype(out_dtype).itemsize
    vmem = 2 * tm * K * 4 + tm * K * 2 + 2 * K * tn * 2 + 2 * tm * tn * ob
    return pl.pallas_call(
        _norm_matmul_body,
        grid=(M // tm, N // tn),
        in_specs=[
            pl.BlockSpec((tm, K), lambda i, j: (i, 0)),
            pl.BlockSpec((1, K), lambda i, j: (0, 0)),
            pl.BlockSpec((K, tn), lambda i, j: (0, j)),
        ],
        out_specs=pl.BlockSpec((tm, tn), lambda i, j: (i, j)),
        out_shape=jax.ShapeDtypeStruct((M, N), out_dtype),
        scratch_shapes=[pltpu.VMEM((tm, K), BF16)],
        compiler_params=_cparams(("parallel", "arbitrary"), vmem),
        name="norm_matmul",
    )(x, gain.reshape(1, K), w)


def _ffn_body(x_ref, g_ref, wg_ref, wu_ref, wd_ref, o_ref, h_scr):
    @pl.when(pl.program_id(1) == 0)
    def _():
        x = x_ref[...]
        h_scr[...] = _rms(x, g_ref[...]).astype(BF16)
        o_ref[...] = x

    h = h_scr[...]
    act = _silu(_dot(h, wg_ref[...])) * _dot(h, wu_ref[...])
    o_ref[...] += 0.5 * _dot(act.astype(BF16), wd_ref[...])


def ffn_half(x, gain, wg, wu, wd, *, tm=512, tf=512):
    M, D = x.shape
    F = wg.shape[1]
    tm = min(tm, M)
    vmem = 4 * tm * D * 4 + tm * D * 2 + 2 * 3 * D * tf * 2
    return pl.pallas_call(
        _ffn_body,
        grid=(M // tm, F // tf),
        in_specs=[
            pl.BlockSpec((tm, D), lambda i, f: (i, 0)),
            pl.BlockSpec((1, D), lambda i, f: (0, 0)),
            pl.BlockSpec((D, tf), lambda i, f: (0, f)),
            pl.BlockSpec((D, tf), lambda i, f: (0, f)),
            pl.BlockSpec((tf, D), lambda i, f: (f, 0)),
        ],
        out_specs=pl.BlockSpec((tm, D), lambda i, f: (i, 0)),
        out_shape=jax.ShapeDtypeStruct((M, D), F32),
        scratch_shapes=[pltpu.VMEM((tm, D), BF16)],
        compiler_params=_cparams(("parallel", "arbitrary"), vmem),
        name="ffn_half",
    )(x, gain.reshape(1, D), wg, wu, wd)


def _matmul_res_body(a_ref, w_ref, r_ref, o_ref):
    o_ref[...] = r_ref[...] + _dot(a_ref[...], w_ref[...])


def matmul_residual(a, w, res, *, tm=512, tn=512):
    M, K = a.shape
    N = w.shape[1]
    tm = min(tm, M)
    vmem = 2 * tm * K * 2 + 2 * K * tn * 2 + 4 * tm * tn * 4
    return pl.pallas_call(
        _matmul_res_body,
        grid=(M // tm, N // tn),
        in_specs=[
            pl.BlockSpec((tm, K), lambda i, j: (i, 0)),
            pl.BlockSpec((K, tn), lambda i, j: (0, j)),
            pl.BlockSpec((tm, tn), lambda i, j: (i, j)),
        ],
        out_specs=pl.BlockSpec((tm, tn), lambda i, j: (i, j)),
        out_shape=jax.ShapeDtypeStruct((M, N), F32),
        compiler_params=_cparams(("parallel", "arbitrary"), vmem),
        name="matmul_residual",
    )(a, w, res)


def _prologue_matmul_res_body(prologue, n_in, *refs):
    in_refs = refs[:n_in]
    w_ref, r_ref, o_ref, lhs_scr = refs[n_in:]
    row_tile = pl.program_id(0)

    @pl.when(pl.program_id(1) == 0)
    def _():
        prologue(row_tile, *in_refs, lhs_scr)

    o_ref[...] = r_ref[...] + _dot(lhs_scr[...], w_ref[...])


def prologue_matmul_residual(prologue, inputs, in_specs, w, res, *, tm, tn, in_vmem, name):
    M, N = res.shape
    K = w.shape[0]
    vmem = in_vmem + tm * K * 2 + 2 * K * tn * 2 + 4 * tm * tn * 4
    return pl.pallas_call(
        functools.partial(_prologue_matmul_res_body, prologue, len(inputs)),
        grid=(M // tm, N // tn),
        in_specs=list(in_specs) + [
            pl.BlockSpec((K, tn), lambda i, j: (0, j)),
            pl.BlockSpec((tm, tn), lambda i, j: (i, j)),
        ],
        out_specs=pl.BlockSpec((tm, tn), lambda i, j: (i, j)),
        out_shape=jax.ShapeDtypeStruct((M, N), F32),
        scratch_shapes=[pltpu.VMEM((tm, K), BF16)],
        compiler_params=_cparams(("parallel", "arbitrary"), vmem),
        name=name,
    )(*inputs, w, res)


def _conv_prologue(tiles_per_seq, row_tile, b_ref, c_ref, u_ref, cp_ref, up_ref, cw_ref, lhs_scr):
    cu = c_ref[...] * u_ref[...]
    prev = cp_ref[...] * up_ref[...]
    first = (row_tile % tiles_per_seq) == 0
    prev = jnp.where(first, 0.0, prev)
    p1, p2 = prev[7:8, :], prev[6:7, :]
    row = lax.broadcasted_iota(jnp.int32, cu.shape, 0)
    s1 = jnp.where(row == 0, p1, pltpu.roll(cu, 1, 0))
    s2 = jnp.where(row == 0, p2, jnp.where(row == 1, p1, pltpu.roll(cu, 2, 0)))
    w = cw_ref[...]
    y = w[0:1, :] * s2 + w[1:2, :] * s1 + w[2:3, :] * cu
    lhs_scr[...] = (b_ref[...] * y).astype(BF16)


def conv_mixer(x, S, gain, w_in, conv_w, w_out, *, tm=512):
    T, D = x.shape
    tm = min(tm, S)
    bcu = norm_matmul(x, gain, w_in)
    r8 = tm // V7X_SUBLANES

    def prev_map(col):
        return lambda i, j: (jnp.maximum(i * r8 - 1, 0), col)

    in_specs = [
        pl.BlockSpec((tm, D), lambda i, j: (i, 0)),
        pl.BlockSpec((tm, D), lambda i, j: (i, 1)),
        pl.BlockSpec((tm, D), lambda i, j: (i, 2)),
        pl.BlockSpec((V7X_SUBLANES, D), prev_map(1)),
        pl.BlockSpec((V7X_SUBLANES, D), prev_map(2)),
        pl.BlockSpec((3, D), lambda i, j: (0, 0)),
    ]
    return prologue_matmul_residual(
        functools.partial(_conv_prologue, S // tm), (bcu, bcu, bcu, bcu, bcu, conv_w), in_specs, w_out, x,
        tm=tm, tn=512, in_vmem=2 * 3 * tm * D * 4 + 4 * V7X_SUBLANES * D * 4, name="conv_mixer_out")


def _xattn_prologue(row_tile, q_ref, kv_ref, qg_ref, kg_ref, lhs_scr):
    del row_tile
    scale = XATTN_HEAD_DIM ** -0.5
    D = XATTN_HEADS * XATTN_HEAD_DIM
    for h in range(XATTN_HEADS):
        sl = slice(h * XATTN_HEAD_DIM, (h + 1) * XATTN_HEAD_DIM)
        qn = _rms(q_ref[:, sl], qg_ref[...]).astype(BF16)
        kn = _rms(kv_ref[:, sl], kg_ref[...]).astype(BF16)
        v = kv_ref[:, D + h * XATTN_HEAD_DIM:D + (h + 1) * XATTN_HEAD_DIM].astype(BF16)
        s = _dot_nt(qn, kn) * scale
        p = jnp.exp(s - jnp.max(s, axis=-1, keepdims=True))
        l = jnp.sum(p, axis=-1, keepdims=True)
        lhs_scr[:, sl] = (_dot(p.astype(BF16), v) / l).astype(BF16)


def cross_attention(x, S, mem, xgain, mgain, wq, wkv, wo, q_gain, k_gain, *, tm=512):
    T, D = x.shape
    tm = min(tm, S)
    q = norm_matmul(x, xgain, wq)
    kv = norm_matmul(mem, mgain, wkv)
    tps = S // tm
    in_specs = [
        pl.BlockSpec((tm, D), lambda i, j: (i, 0)),
        pl.BlockSpec((MEM_LEN, 2 * D), lambda i, j: (i // tps, 0)),
        pl.BlockSpec((1, XATTN_HEAD_DIM), lambda i, j: (0, 0)),
        pl.BlockSpec((1, XATTN_HEAD_DIM), lambda i, j: (0, 0)),
    ]
    return prologue_matmul_residual(
        _xattn_prologue, (q, kv, q_gain.reshape(1, -1), k_gain.reshape(1, -1)), in_specs, wo, x,
        tm=tm, tn=512, in_vmem=2 * tm * D * 4 + 2 * MEM_LEN * 2 * D * 4, name="xattn_out")


def _dil_prep_body(x_ref, pos_ref, invf_ref, qg_ref, kg_ref, o_ref):
    ang = pos_ref[...] * invf_ref[...]
    lane = lax.broadcasted_iota(jnp.int32, ang.shape, 1)
    half = DIL_HEAD_DIM // 8
    cos, sin = jnp.cos(ang), jnp.sin(ang)
    sin_lo = jnp.where(lane < half, -sin, 0.0)
    sin_hi = jnp.where((lane >= half) & (lane < 2 * half), sin, 0.0)
    for part, g_ref in ((0, qg_ref), (1, kg_ref)):
        for g in range(DIL_GROUPS):
            gain = g_ref[g:g + 1, :]
            for h in range(DIL_HEADS):
                col = ((part * DIL_GROUPS + g) * DIL_HEADS + h) * DIL_HEAD_DIM
                xn = _rms(x_ref[:, col:col + DIL_HEAD_DIM], gain)
                o_ref[:, col:col + DIL_HEAD_DIM] = (
                    xn * cos + pltpu.roll(xn, DIL_HEAD_DIM - half, 1) * sin_lo + pltpu.roll(xn, half, 1) * sin_hi)


def _dil_attn_body(*refs, n_chunks):
    ins, o_ref, scr = refs[:15], refs[15], refs[16:]
    c = pl.program_id(1)
    scale = DIL_HEAD_DIM ** -0.5
    ii = lax.broadcasted_iota(jnp.int32, (DIL_BLOCK, DIL_BLOCK), 0)
    jj = lax.broadcasted_iota(jnp.int32, (DIL_BLOCK, DIL_BLOCK), 1)
    cur_mask = jj <= ii
    prev_mask = jj >= ii
    ch = o_ref.shape[0]
    for g, (_, dil) in enumerate(DIL_PATTERNS):
        q_ref, k_ref, v_ref, kh_ref, vh_ref = ins[5 * g:5 * g + 5]
        kf, vf, og, lg = scr[4 * g:4 * g + 4]
        hist = DIL_BLOCK * dil
        kf[0:hist, :] = kh_ref[...]
        vf[0:hist, :] = vh_ref[...]
        kf[hist:hist + ch, :] = k_ref[...]
        vf[hist:hist + ch, :] = v_ref[...]
        for blk in range(ch // hist):
            for r in range(dil):
                q0 = blk * hist + r
                rows_q = pl.ds(q0, DIL_BLOCK, stride=dil) if dil > 1 else pl.ds(q0, DIL_BLOCK)
                rows_c = pl.ds(hist + q0, DIL_BLOCK, stride=dil) if dil > 1 else pl.ds(hist + q0, DIL_BLOCK)
                rows_p = pl.ds(q0, DIL_BLOCK, stride=dil) if dil > 1 else pl.ds(q0, DIL_BLOCK)
                qv = q_ref[rows_q, :].astype(BF16)
                sc = jnp.where(cur_mask, _dot_nt(qv, kf[rows_c, :].astype(BF16)) * scale, NEG_INF)
                sp = _dot_nt(qv, kf[rows_p, :].astype(BF16)) * scale
                pm = prev_mask if blk > 0 else prev_mask & (c > 0)
                sp = jnp.where(pm, sp, NEG_INF)
                m = jnp.maximum(jnp.max(sc, -1, keepdims=True), jnp.max(sp, -1, keepdims=True))
                pc, pp = jnp.exp(sc - m), jnp.exp(sp - m)
                l = jnp.sum(pc, -1, keepdims=True) + jnp.sum(pp, -1, keepdims=True)
                o = (_dot(pc.astype(BF16), vf[rows_c, :].astype(BF16))
                     + _dot(pp.astype(BF16), vf[rows_p, :].astype(BF16))) / l
                og[rows_q, :] = o
                lg[rows_q, :] = jnp.broadcast_to(m + jnp.log(l), (DIL_BLOCK, DIL_HEAD_DIM))
    l0, l1, l2 = scr[3][...], scr[7][...], scr[11][...]
    mx = jnp.maximum(jnp.maximum(l0, l1), l2)
    e0, e1, e2 = jnp.exp(l0 - mx), jnp.exp(l1 - mx), jnp.exp(l2 - mx)
    o_ref[...] = ((e0 * scr[2][...] + e1 * scr[6][...] + e2 * scr[10][...]) / (e0 + e1 + e2)).astype(o_ref.dtype)


def dilated_mixer(x, B, S, positions, gain, w_qkv, q_gain, k_gain, w_out):
    T, D = x.shape
    nh = DIL_GROUPS * DIL_HEADS
    qkv = norm_matmul(x, gain, w_qkv)
    rot = DIL_HEAD_DIM // 4
    inv_freq = ROPE_THETA ** (-jnp.arange(0, rot, 2, dtype=F32) / rot)
    invf = jnp.concatenate([inv_freq, inv_freq, jnp.zeros((DIL_HEAD_DIM - rot,), F32)]).reshape(1, DIL_HEAD_DIM)
    pos = positions.astype(F32).reshape(T, 1)
    tp = 256
    qk_cols = 2 * nh * DIL_HEAD_DIM
    qk = pl.pallas_call(
        _dil_prep_body,
        grid=(T // tp,),
        in_specs=[
            pl.BlockSpec((tp, qk_cols), lambda i: (i, 0)),
            pl.BlockSpec((tp, 1), lambda i: (i, 0)),
            pl.BlockSpec((1, DIL_HEAD_DIM), lambda i: (0, 0)),
            pl.BlockSpec((DIL_GROUPS, DIL_HEAD_DIM), lambda i: (0, 0)),
            pl.BlockSpec((DIL_GROUPS, DIL_HEAD_DIM), lambda i: (0, 0)),
        ],
        out_specs=pl.BlockSpec((tp, qk_cols), lambda i: (i, 0)),
        out_shape=jax.ShapeDtypeStruct((T, qk_cols), F32),
        compiler_params=_cparams(("parallel",), 4 * tp * qk_cols * 4 + 2 * tp * V7X_LANES * 4),
        name="dil_qk_prep",
    )(qkv, pos, invf, q_gain, k_gain)

    ch = DIL_BLOCK * DIL_PATTERNS[-1][1]
    n_chunks = S // ch
    inputs, in_specs, scratch = [], [], []
    vmem = 2 * ch * DIL_HEAD_DIM * 2
    for g, (_, dil) in enumerate(DIL_PATTERNS):
        hist = DIL_BLOCK * dil
        per = ch // hist

        def cur_map(col):
            return lambda b, c, h: (b * n_chunks + c, col + h)

        def hist_map(col, per=per):
            return lambda b, c, h: (jnp.maximum((b * n_chunks + c) * per - 1, 0), col + h)

        inputs += [qk, qk, qkv, qk, qkv]
        in_specs += [
            pl.BlockSpec((ch, DIL_HEAD_DIM), cur_map(g * DIL_HEADS)),
            pl.BlockSpec((ch, DIL_HEAD_DIM), cur_map(nh + g * DIL_HEADS)),
            pl.BlockSpec((ch, DIL_HEAD_DIM), cur_map(2 * nh + g * DIL_HEADS)),
            pl.BlockSpec((hist, DIL_HEAD_DIM), hist_map(nh + g * DIL_HEADS)),
            pl.BlockSpec((hist, DIL_HEAD_DIM), hist_map(2 * nh + g * DIL_HEADS)),
        ]
        scratch += [pltpu.VMEM((hist + ch, DIL_HEAD_DIM), F32), pltpu.VMEM((hist + ch, DIL_HEAD_DIM), F32),
                    pltpu.VMEM((ch, DIL_HEAD_DIM), F32), pltpu.VMEM((ch, DIL_HEAD_DIM), F32)]
        vmem += (2 * (3 * ch + 2 * hist) + 2 * (hist + ch) + 2 * ch) * DIL_HEAD_DIM * 4
    o = pl.pallas_call(
        functools.partial(_dil_attn_body, n_chunks=n_chunks),
        grid=(B, n_chunks, DIL_HEADS),
        in_specs=in_specs,
        out_specs=pl.BlockSpec((ch, DIL_HEAD_DIM), lambda b, c, h: (b * n_chunks + c, h)),
        out_shape=jax.ShapeDtypeStruct((T, DIL_HEADS * DIL_HEAD_DIM), BF16),
        scratch_shapes=scratch,
        compiler_params=_cparams(("parallel", "arbitrary", "arbitrary"), vmem),
        name="dil_attention",
    )(*inputs)
    return matmul_residual(o, w_out, x)


def _hgrn_body(q_ref, f_ref, i_ref, gt_ref, lbl_ref, gain_ref, o_ref, st_scr, *, layer):
    @pl.when(pl.program_id(2) == 0)
    def _():
        st_scr[...] = jnp.zeros_like(st_scr)

    tt = q_ref.shape[0]
    lbl = lbl_ref[...]
    e = jnp.exp(lbl - jnp.max(lbl, axis=0, keepdims=True))
    p = e / jnp.sum(e, axis=0, keepdims=True)
    lb = jnp.sum(p[1:layer + 1, :], axis=0, keepdims=True)
    forget = lb + (1.0 - lb) * jax.nn.sigmoid(f_ref[...])
    k = 1.0 - forget
    gl = jnp.log(forget)
    rows = lax.broadcasted_iota(jnp.int32, (tt, tt), 0)
    cols = lax.broadcasted_iota(jnp.int32, (tt, tt), 1)
    same = _chunk_of(rows, HGRN_CHUNK) == _chunk_of(cols, HGRN_CHUNK)
    tri = same & (cols <= rows)
    a_cum = _dot_exact(tri.astype(F32), gl)
    a_tot = _dot_exact(same.astype(F32), gl)
    q_dec = (q_ref[...] * jnp.exp(a_cum)).astype(BF16)
    k_in = (k * jnp.exp(-a_cum)).astype(BF16)
    k_end = (k * jnp.exp(a_tot - a_cum)).astype(BF16)
    v = i_ref[...].astype(BF16)
    att = jnp.where(tri, _dot_nt(q_dec, k_in), 0.0)
    o = _dot(att.astype(BF16), v)
    dec = jnp.exp(a_tot)
    st = st_scr[...]
    inter = []
    for c in range(tt // HGRN_CHUNK):
        sl = slice(c * HGRN_CHUNK, (c + 1) * HGRN_CHUNK)
        inter.append(_dot_nt(q_dec[sl], st.astype(BF16)))
        st = st * dec[c * HGRN_CHUNK:c * HGRN_CHUNK + 1, :] + _dot_tn(v[sl], k_end[sl])
    st_scr[...] = st
    o = o + jnp.concatenate(inter, axis=0)
    o_ref[...] = (_rms(o, gain_ref[...]) * _silu(gt_ref[...])).astype(o_ref.dtype)


def hgrn_mixer(x, B, S, layer, gain, w_in, lb_logits, norm_gain, w_out, *, tt=256):
    T, D = x.shape
    dh = D // HGRN_HEADS
    proj = norm_matmul(x, gain, w_in)
    tt = min(tt, S)
    nt = S // tt

    def part(pidx):
        return pl.BlockSpec((tt, dh), lambda b, h, s: (b * nt + s, pidx * HGRN_HEADS + h))

    o = pl.pallas_call(
        functools.partial(_hgrn_body, layer=layer),
        grid=(B, HGRN_HEADS, nt),
        in_specs=[part(0), part(1), part(2), part(3),
                  pl.BlockSpec((DEPTH, dh), lambda b, h, s: (0, h)),
                  pl.BlockSpec((1, dh), lambda b, h, s: (0, 0))],
        out_specs=pl.BlockSpec((tt, dh), lambda b, h, s: (b * nt + s, h)),
        out_shape=jax.ShapeDtypeStruct((T, D), BF16),
        scratch_shapes=[pltpu.VMEM((dh, dh), F32)],
        compiler_params=_cparams(("parallel", "parallel", "arbitrary"), 16 * tt * dh * 4 + 8 * tt * tt * 4),
        name="hgrn_core",
    )(proj, proj, proj, proj, lb_logits, norm_gain.reshape(1, dh))
    return matmul_residual(o, w_out, x)


def _rwkv_proj_body(x_ref, xp_ref, gn_ref, mu_ref, wrkv_ref, w0_ref, w1_ref, w2_ref, a0_ref, a1_ref, a2_ref,
                    g1_ref, g2_ref, rkv_ref, lw_ref, a_ref, g_ref, mix_scr, *, tiles_per_seq, n_col_tiles):
    n = pl.program_id(1)
    first = (pl.program_id(0) % tiles_per_seq) == 0

    @pl.when(n == 0)
    def _():
        gn = gn_ref[...]
        h = _rms(x_ref[...], gn)
        last = _rms(xp_ref[...], gn)[V7X_SUBLANES - 1:V7X_SUBLANES, :]
        last = jnp.where(first, 0.0, last)
        row = lax.broadcasted_iota(jnp.int32, h.shape, 0)
        d = jnp.where(row == 0, last, pltpu.roll(h, 1, 0)) - h
        mu = mu_ref[...]
        for m in range(3):
            mix_scr[m] = (h + d * mu[m:m + 1, :]).astype(BF16)
        xw = (h + d * mu[3:4, :]).astype(BF16)
        xa = (h + d * mu[4:5, :]).astype(BF16)
        xg = (h + d * mu[5:6, :]).astype(BF16)
        z = -(w0_ref[...] + _dot(jnp.tanh(_dot(xw, w1_ref[...])).astype(BF16), w2_ref[...]))
        softplus = jnp.maximum(z, 0.0) + jnp.log1p(jnp.exp(-jnp.abs(z)))
        lw_ref[...] = -jnp.exp(-softplus - 0.5)
        a_ref[...] = jax.nn.sigmoid(a0_ref[...] + _dot(_dot(xa, a1_ref[...]).astype(BF16), a2_ref[...]))
        g_ref[...] = _dot(jax.nn.sigmoid(_dot(xg, g1_ref[...])).astype(BF16), g2_ref[...])

    rkv_ref[...] = _dot(mix_scr[n // n_col_tiles], wrkv_ref[...])


def _rwkv_core_body(r_ref, k_ref, v_ref, lw_ref, a_ref, g_ref, kk_ref, ka_ref, rk_ref, lnw_ref, lnb_ref,
                    o_ref, h_scr):
    @pl.when(pl.program_id(2) == 0)
    def _():
        h_scr[...] = jnp.zeros_like(h_scr)

    tt = r_ref.shape[0]
    C, N = RWKV_CHUNK, RWKV_HEAD_DIM
    lw = lw_ref[...]
    rows = lax.broadcasted_iota(jnp.int32, (tt, tt), 0)
    cols = lax.broadcasted_iota(jnp.int32, (tt, tt), 1)
    same = _chunk_of(rows, C) == _chunk_of(cols, C)
    g_cum = _dot_exact((same & (cols <= rows)).astype(F32), lw)
    g_tot = _dot_exact(same.astype(F32), lw)
    r, k, v, a = r_ref[...], k_ref[...], v_ref[...], a_ref[...]
    left = lax.broadcasted_iota(jnp.int32, (tt, 2 * N), 1) < N

    def head_sum(t):
        return jnp.where(left, jnp.sum(jnp.where(left, t, 0.0), -1, keepdims=True),
                         jnp.sum(jnp.where(left, 0.0, t), -1, keepdims=True))

    kk = k * kk_ref[...]
    kk = kk * lax.rsqrt(jnp.maximum(head_sum(kk * kk), 1e-24))
    kmod = k * (1.0 + (a - 1.0) * ka_ref[...])
    bv = kk * a
    e_neg = jnp.exp(-g_cum)
    e_end = jnp.exp(g_tot - g_cum)
    a_t = (-kk) * jnp.exp(g_cum - lw)
    r_t = r * jnp.exp(g_cum)
    k_t, b_t = kmod * e_neg, bv * e_neg
    k_h, b_h = kmod * e_end, bv * e_end
    dec = jnp.exp(g_tot)
    i64 = lax.broadcasted_iota(jnp.int32, (C, C), 0)
    j64 = lax.broadcasted_iota(jnp.int32, (C, C), 1)
    strict, incl, eye = j64 < i64, j64 <= i64, (i64 == j64).astype(F32)

    y_heads = []
    for hh in range(2):
        sl = slice(hh * N, (hh + 1) * N)
        ht = h_scr[hh]
        ys = []
        for c in range(tt // C):
            rs = slice(c * C, (c + 1) * C)
            a_c, r_c, v_c = a_t[rs, sl], r_t[rs, sl], v[rs, sl].astype(BF16)
            bh_c, kh_c = b_h[rs, sl].astype(BF16), k_h[rs, sl].astype(BF16)
            lhs = jnp.concatenate([a_c, r_c], axis=0).astype(BF16)
            rhs = jnp.concatenate([b_t[rs, sl], k_t[rs, sl]], axis=0).astype(BF16)
            prod = _dot_nt(lhs, rhs)
            n_ab = jnp.where(strict, prod[:C, :C], 0.0)
            n_ak = jnp.where(strict, prod[:C, C:], 0.0).astype(BF16)
            t_rb = jnp.where(incl, prod[C:, :C], 0.0).astype(BF16)
            t_rk = jnp.where(incl, prod[C:, C:], 0.0).astype(BF16)
            inv, pw = eye + n_ab, n_ab
            for _ in range(5):
                pw16 = pw.astype(BF16)
                pw = _dot(pw16, pw16)
                inv = inv + _dot(inv.astype(BF16), pw.astype(BF16))
            inv16 = inv.astype(BF16)
            a_p = _dot(inv16, a_c.astype(BF16)).astype(BF16)
            u0 = _dot(inv16, _dot(n_ak, v_c).astype(BF16)).astype(BF16)
            r_p = (r_c + _dot(t_rb, a_p)).astype(BF16)
            y0 = _dot(t_rb, u0) + _dot(t_rk, v_c)
            trans_t = _dot_tn(a_p, bh_c).astype(BF16)
            h_add = _dot_tn(u0, bh_c) + _dot_tn(v_c, kh_c)
            ht16 = ht.astype(BF16)
            ys.append(_dot_nt(r_p, ht16) + y0)
            ht = ht * dec[c * C:c * C + 1, sl] + _dot(ht16, trans_t) + h_add
        h_scr[hh] = ht
        y = jnp.concatenate(ys, axis=0)
        mean = jnp.mean(y, -1, keepdims=True)
        var = jnp.mean(jnp.square(y - mean), -1, keepdims=True)
        y_heads.append((y - mean) * lax.rsqrt(var + RWKV_GN_EPS))
    yn = jnp.concatenate(y_heads, axis=1) * lnw_ref[...] + lnb_ref[...]
    bonus = head_sum(r * kmod * rk_ref[...]) * v
    o_ref[...] = ((yn + bonus) * g_ref[...]).astype(o_ref.dtype)


def rwkv_mixer(x, B, S, gain, mu, w_rkv, w0, w1, w2, a0, a1, a2, g1, g2, k_k, k_a, r_k, ln_w, ln_b, w_out,
               *, tm=256, tn=512, tt=256):
    T, D = x.shape
    tm, tt = min(tm, S), min(tt, S)
    nct = D // tn
    r8 = tm // V7X_SUBLANES
    lora = w1.shape[1]
    pad = (-lora) % V7X_LANES
    w1p, a1p = jnp.pad(w1, ((0, 0), (0, pad))), jnp.pad(a1, ((0, 0), (0, pad)))
    w2p, a2p = jnp.pad(w2, ((0, pad), (0, 0))), jnp.pad(a2, ((0, pad), (0, 0)))
    lp, gl = lora + pad, g1.shape[1]
    row = lambda i, n: (0, 0)
    vmem = (2 * tm * D * 4 + 3 * tm * D * 2 + 6 * tm * D * 4 + 2 * D * tn * 2 + 2 * tm * tn * 4
            + 2 * 2 * (2 * D * lp + D * gl) * 2 + 6 * tm * D * 4)
    rkv, lw, a, g = pl.pallas_call(
        functools.partial(_rwkv_proj_body, tiles_per_seq=S // tm, n_col_tiles=nct),
        grid=(T // tm, 3 * nct),
        in_specs=[
            pl.BlockSpec((tm, D), lambda i, n: (i, 0)),
            pl.BlockSpec((V7X_SUBLANES, D), lambda i, n: (jnp.maximum(i * r8 - 1, 0), 0)),
            pl.BlockSpec((1, D), row),
            pl.BlockSpec((6, D), row),
            pl.BlockSpec((None, D, tn), lambda i, n: (n // nct, 0, n % nct)),
            pl.BlockSpec((1, D), row), pl.BlockSpec((D, lp), row), pl.BlockSpec((lp, D), row),
            pl.BlockSpec((1, D), row), pl.BlockSpec((D, lp), row), pl.BlockSpec((lp, D), row),
            pl.BlockSpec((D, gl), row), pl.BlockSpec((gl, D), row),
        ],
        out_specs=[
            pl.BlockSpec((tm, tn), lambda i, n: (i, n)),
            pl.BlockSpec((tm, D), lambda i, n: (i, 0)),
            pl.BlockSpec((tm, D), lambda i, n: (i, 0)),
            pl.BlockSpec((tm, D), lambda i, n: (i, 0)),
        ],
        out_shape=[jax.ShapeDtypeStruct((T, 3 * D), F32), jax.ShapeDtypeStruct((T, D), F32),
                   jax.ShapeDtypeStruct((T, D), F32), jax.ShapeDtypeStruct((T, D), F32)],
        scratch_shapes=[pltpu.VMEM((3, tm, D), BF16)],
        compiler_params=_cparams(("parallel", "arbitrary"), vmem),
        name="rwkv_proj",
    )(x, x, gain.reshape(1, D), mu, w_rkv, w0.reshape(1, D), w1p, w2p, a0.reshape(1, D), a1p, a2p, g1, g2)

    nt = S // tt
    pw = 2 * RWKV_HEAD_DIM
    npair = D // pw

    def tok(col0):
        return pl.BlockSpec((tt, pw), lambda b, p, s: (b * nt + s, col0 + p))

    par = pl.BlockSpec((1, pw), lambda b, p, s: (0, p))
    o = pl.pallas_call(
        _rwkv_core_body,
        grid=(B, npair, nt),
        in_specs=[tok(0), tok(npair), tok(2 * npair), tok(0), tok(0), tok(0), par, par, par, par, par],
        out_specs=pl.BlockSpec((tt, pw), lambda b, p, s: (b * nt + s, p)),
        out_shape=jax.ShapeDtypeStruct((T, D), BF16),
        scratch_shapes=[pltpu.VMEM((2, RWKV_HEAD_DIM, RWKV_HEAD_DIM), F32)],
        compiler_params=_cparams(("parallel", "parallel", "arbitrary"), 40 * tt * pw * 4 + 8 * tt * tt * 4),
        name="rwkv_core",
    )(rkv, rkv, rkv, lw, a, g, k_k.reshape(1, D), k_a.reshape(1, D), r_k.reshape(1, D),
      ln_w.reshape(1, D), ln_b.reshape(1, D))
    return matmul_residual(o, w_out, x)


def kernel(x, mem, positions, ffn_norm, ffn_w_gate, ffn_w_up, ffn_w_down, mix_norm, xattn_norm, mem_norm, xattn_wq, xattn_wkv, xattn_wo, xattn_q_gain, xattn_k_gain, conv_w_in, conv_w, conv_w_out, dil_w_qkv, dil_q_gain, dil_k_gain, dil_w_out, hgrn_w_in, hgrn_lb_logits, hgrn_norm, hgrn_w_out, rwkv_mu, rwkv_w_rkv, rwkv_w0, rwkv_w1, rwkv_w2, rwkv_a0, rwkv_a1, rwkv_a2, rwkv_g1, rwkv_g2, rwkv_k_k, rwkv_k_a, rwkv_r_k, rwkv_ln_w, rwkv_ln_b, rwkv_w_out):
    B, S, D = x.shape
    assert D == D_MODEL and S % (DIL_BLOCK * DIL_PATTERNS[-1][1]) == 0
    depth = ffn_norm.shape[0]
    xf = x.reshape(B * S, D)
    memf = mem.reshape(B * MEM_LEN, D)
    bf = lambda w: w.astype(BF16)
    for i in range(depth):
        kind, j = i % N_MIXERS, i // N_MIXERS
        xf = ffn_half(xf, ffn_norm[i, 0], bf(ffn_w_gate[i, 0]), bf(ffn_w_up[i, 0]), bf(ffn_w_down[i, 0]))
        if kind == 0:
            xf = conv_mixer(xf, S, mix_norm[i], bf(conv_w_in[j]), conv_w[j], bf(conv_w_out[j]))
        elif kind == 1:
            xf = dilated_mixer(xf, B, S, positions, mix_norm[i], bf(dil_w_qkv[j]), dil_q_gain[j], dil_k_gain[j],
                               bf(dil_w_out[j]))
        elif kind == 2:
            xf = hgrn_mixer(xf, B, S, i, mix_norm[i], bf(hgrn_w_in[j]), hgrn_lb_logits, hgrn_norm[j],
                            bf(hgrn_w_out[j]))
        else:
            xf = rwkv_mixer(xf, B, S, mix_norm[i], rwkv_mu[j], bf(rwkv_w_rkv[j]), rwkv_w0[j], bf(rwkv_w1[j]),
                            bf(rwkv_w2[j]), rwkv_a0[j], bf(rwkv_a1[j]), bf(rwkv_a2[j]), bf(rwkv_g1[j]),
                            bf(rwkv_g2[j]), rwkv_k_k[j], rwkv_k_a[j], rwkv_r_k[j], rwkv_ln_w[j], rwkv_ln_b[j],
                            bf(rwkv_w_out[j]))
        xf = cross_attention(xf, S, memf, xattn_norm[i], mem_norm[i], bf(xattn_wq[i]), bf(xattn_wkv[i]),
                             bf(xattn_wo[i]), xattn_q_gain[i], xattn_k_gain[i])
        xf = ffn_half(xf, ffn_norm[i, 1], bf(ffn_w_gate[i, 1]), bf(ffn_w_up[i, 1]), bf(ffn_w_down[i, 1]))
    return xf.reshape(B, S, D)
```

```python
import functools

import jax
import jax.numpy as jnp
from jax import lax
from jax.experimental import pallas as pl
from jax.experimental.pallas import tpu as pltpu

F32 = jnp.float32
BF16 = jnp.bfloat16

D_MODEL = 2048
DEPTH = 4
N_MIXERS = 4
MEM_LEN = 256
FFN_DIM = 5632
NORM_EPS = 1e-6
NEG_INF = -1e30
ROPE_THETA = 500000.0
DIL_PATTERNS = ((128, 1), (512, 4), (2048, 16))
DIL_GROUPS = 3
DIL_HEADS = 8
DIL_HEAD_DIM = 128
DIL_BLOCK = 128
HGRN_CHUNK = 16
HGRN_HEADS = 16
RWKV_HEAD_DIM = 64
RWKV_HEADS = 32
RWKV_CHUNK = 64
RWKV_GN_EPS = 64e-5
XATTN_HEADS = 4
XATTN_HEAD_DIM = 512

V7X_LANES = 128
V7X_SUBLANES = 8
V7X_VMEM_BYTES = 64 * 2**20
V7X_VMEM_CAP = 56 * 2**20


def _cparams(sem, vmem_bytes):
    limit = min(int(vmem_bytes * 1.25) + (4 << 20), V7X_VMEM_CAP)
    return pltpu.CompilerParams(dimension_semantics=sem, vmem_limit_bytes=limit)


def _rms(x, gain):
    return x * lax.rsqrt(jnp.mean(x * x, axis=-1, keepdims=True) + NORM_EPS) * gain


def _dot(a, b):
    return jnp.dot(a, b, preferred_element_type=F32)


def _dot_nt(a, b):
    return lax.dot_general(a, b, (((1,), (1,)), ((), ())), preferred_element_type=F32)


def _dot_tn(a, b):
    return lax.dot_general(a, b, (((0,), (0,)), ((), ())), preferred_element_type=F32)


def _dot_exact(a, b):
    return jnp.dot(a, b, precision=lax.Precision.HIGHEST, preferred_element_type=F32)


def _chunk_of(idx, chunk):
    return jnp.right_shift(idx, chunk.bit_length() - 1)


def _silu(x):
    return x * jax.nn.sigmoid(x)


def _norm_matmul_body(x_ref, g_ref, w_ref, o_ref, h_scr):
    @pl.when(pl.program_id(1) == 0)
    def _():
        h_scr[...] = _rms(x_ref[...], g_ref[...]).astype(BF16)

    o_ref[...] = _dot(h_scr[...], w_ref[...]).astype(o_ref.dtype)


def norm_matmul(x, gain, w, *, tm=1024, tn=1024, out_dtype=F32):
    M, K = x.shape
    N = w.shape[1]
    tm, tn = min(tm, M), min(tn, N)
    ob = jnp.dtype(out_dtype).itemsize
    vmem = 2 * tm * K * 4 + tm * K * 2 + 2 * K * tn * 2 + 2 * tm * tn * ob
    return pl.pallas_call(
        _norm_matmul_body,
        grid=(M // tm, N // tn),
        in_specs=[
            pl.BlockSpec((tm, K), lambda i, j: (i, 0)),
            pl.BlockSpec((1, K), lambda i, j: (0, 0)),
            pl.BlockSpec((K, tn), lambda i, j: (0, j)),
        ],
        out_specs=pl.BlockSpec((tm, tn), lambda i, j: (i, j)),
        out_shape=jax.ShapeDtypeStruct((M, N), out_dtype),
        scratch_shapes=[pltpu.VMEM((tm, K), BF16)],
        compiler_params=_cparams(("parallel", "arbitrary"), vmem),
        name="norm_matmul",
    )(x, gain.reshape(1, K), w)


def _ffn_body(x_ref, g_ref, wg_ref, wu_ref, wd_ref, o_ref, h_scr):
    @pl.when(pl.program_id(1) == 0)
    def _():
        x = x_ref[...]
        h_scr[...] = _rms(x, g_ref[...]).astype(BF16)
        o_ref[...] = x

    h = h_scr[...]
    act = _silu(_dot(h, wg_ref[...])) * _dot(h, wu_ref[...])
    o_ref[...] += 0.5 * _dot(act.astype(BF16), wd_ref[...])


def ffn_half(x, gain, wg, wu, wd, *, tm=1024, tf=512):
    M, D = x.shape
    F = wg.shape[1]
    tm = min(tm, M)
    vmem = 3 * tm * D * 4 + tm * D * 2 + 2 * 3 * D * tf * 2 + 3 * tm * tf * 4
    return pl.pallas_call(
        _ffn_body,
        grid=(M // tm, F // tf),
        in_specs=[
            pl.BlockSpec((tm, D), lambda i, f: (i, 0), pipeline_mode=pl.Buffered(1)),
            pl.BlockSpec((1, D), lambda i, f: (0, 0)),
            pl.BlockSpec((D, tf), lambda i, f: (0, f)),
            pl.BlockSpec((D, tf), lambda i, f: (0, f)),
            pl.BlockSpec((tf, D), lambda i, f: (f, 0)),
        ],
        out_specs=pl.BlockSpec((tm, D), lambda i, f: (i, 0)),
        out_shape=jax.ShapeDtypeStruct((M, D), F32),
        scratch_shapes=[pltpu.VMEM((tm, D), BF16)],
        compiler_params=_cparams(("parallel", "arbitrary"), vmem),
        name="ffn_half",
    )(x, gain.reshape(1, D), wg, wu, wd)


def _matmul_res_body(a_ref, w_ref, r_ref, o_ref):
    o_ref[...] = r_ref[...] + _dot(a_ref[...], w_ref[...])


def matmul_residual(a, w, res, *, tm=1024, tn=1024):
    M, K = a.shape
    N = w.shape[1]
    tm = min(tm, M)
    vmem = 2 * tm * K * 2 + 2 * K * tn * 2 + 4 * tm * tn * 4
    return pl.pallas_call(
        _matmul_res_body,
        grid=(M // tm, N // tn),
        in_specs=[
            pl.BlockSpec((tm, K), lambda i, j: (i, 0)),
            pl.BlockSpec((K, tn), lambda i, j: (0, j)),
            pl.BlockSpec((tm, tn), lambda i, j: (i, j)),
        ],
        out_specs=pl.BlockSpec((tm, tn), lambda i, j: (i, j)),
        out_shape=jax.ShapeDtypeStruct((M, N), F32),
        compiler_params=_cparams(("parallel", "arbitrary"), vmem),
        name="matmul_residual",
    )(a, w, res)


def _prologue_matmul_res_body(prologue, n_in, *refs):
    in_refs = refs[:n_in]
    w_ref, r_ref, o_ref, lhs_scr = refs[n_in:]
    row_tile = pl.program_id(0)

    @pl.when(pl.program_id(1) == 0)
    def _():
        prologue(row_tile, *in_refs, lhs_scr)

    o_ref[...] = r_ref[...] + _dot(lhs_scr[...], w_ref[...])


def prologue_matmul_residual(prologue, inputs, in_specs, w, res, *, tm, tn, in_vmem, name):
    M, N = res.shape
    K = w.shape[0]
    vmem = in_vmem + tm * K * 2 + 2 * K * tn * 2 + 4 * tm * tn * 4
    return pl.pallas_call(
        functools.partial(_prologue_matmul_res_body, prologue, len(inputs)),
        grid=(M // tm, N // tn),
        in_specs=list(in_specs) + [
            pl.BlockSpec((K, tn), lambda i, j: (0, j)),
            pl.BlockSpec((tm, tn), lambda i, j: (i, j)),
        ],
        out_specs=pl.BlockSpec((tm, tn), lambda i, j: (i, j)),
        out_shape=jax.ShapeDtypeStruct((M, N), F32),
        scratch_shapes=[pltpu.VMEM((tm, K), BF16)],
        compiler_params=_cparams(("parallel", "arbitrary"), vmem),
        name=name,
    )(*inputs, w, res)


def _conv_prologue(tiles_per_seq, row_tile, b_ref, c_ref, u_ref, cp_ref, up_ref, cw_ref, lhs_scr):
    cu = c_ref[...] * u_ref[...]
    prev = cp_ref[...] * up_ref[...]
    first = (row_tile % tiles_per_seq) == 0
    prev = jnp.where(first, 0.0, prev)
    p1, p2 = prev[7:8, :], prev[6:7, :]
    row = lax.broadcasted_iota(jnp.int32, cu.shape, 0)
    s1 = jnp.where(row == 0, p1, pltpu.roll(cu, 1, 0))
    s2 = jnp.where(row == 0, p2, jnp.where(row == 1, p1, pltpu.roll(cu, 2, 0)))
    w = cw_ref[...]
    y = w[0:1, :] * s2 + w[1:2, :] * s1 + w[2:3, :] * cu
    lhs_scr[...] = (b_ref[...] * y).astype(BF16)


def conv_mixer(x, S, gain, w_in, conv_w, w_out, *, tm=512):
    T, D = x.shape
    tm = min(tm, S)
    bcu = norm_matmul(x, gain, w_in)
    r8 = tm // V7X_SUBLANES

    def prev_map(col):
        return lambda i, j: (jnp.maximum(i * r8 - 1, 0), col)

    in_specs = [
        pl.BlockSpec((tm, D), lambda i, j: (i, 0)),
        pl.BlockSpec((tm, D), lambda i, j: (i, 1)),
        pl.BlockSpec((tm, D), lambda i, j: (i, 2)),
        pl.BlockSpec((V7X_SUBLANES, D), prev_map(1)),
        pl.BlockSpec((V7X_SUBLANES, D), prev_map(2)),
        pl.BlockSpec((3, D), lambda i, j: (0, 0)),
    ]
    return prologue_matmul_residual(
        functools.partial(_conv_prologue, S // tm), (bcu, bcu, bcu, bcu, bcu, conv_w), in_specs, w_out, x,
        tm=tm, tn=1024, in_vmem=2 * 3 * tm * D * 4 + 4 * V7X_SUBLANES * D * 4, name="conv_mixer_out")


def _xattn_prologue(row_tile, q_ref, kv_ref, qg_ref, kg_ref, lhs_scr):
    del row_tile
    scale = XATTN_HEAD_DIM ** -0.5
    D = XATTN_HEADS * XATTN_HEAD_DIM
    for h in range(XATTN_HEADS):
        sl = slice(h * XATTN_HEAD_DIM, (h + 1) * XATTN_HEAD_DIM)
        qn = _rms(q_ref[:, sl], qg_ref[...]).astype(BF16)
        kn = _rms(kv_ref[:, sl], kg_ref[...]).astype(BF16)
        v = kv_ref[:, D + h * XATTN_HEAD_DIM:D + (h + 1) * XATTN_HEAD_DIM].astype(BF16)
        s = _dot_nt(qn, kn) * scale
        p = jnp.exp(s - jnp.max(s, axis=-1, keepdims=True))
        l = jnp.sum(p, axis=-1, keepdims=True)
        lhs_scr[:, sl] = (_dot(p.astype(BF16), v) / l).astype(BF16)


def cross_attention(x, S, mem, xgain, mgain, wq, wkv, wo, q_gain, k_gain, *, tm=512):
    T, D = x.shape
    tm = min(tm, S)
    q = norm_matmul(x, xgain, wq)
    kv = norm_matmul(mem, mgain, wkv)
    tps = S // tm
    in_specs = [
        pl.BlockSpec((tm, D), lambda i, j: (i, 0)),
        pl.BlockSpec((MEM_LEN, 2 * D), lambda i, j: (i // tps, 0)),
        pl.BlockSpec((1, XATTN_HEAD_DIM), lambda i, j: (0, 0)),
        pl.BlockSpec((1, XATTN_HEAD_DIM), lambda i, j: (0, 0)),
    ]
    return prologue_matmul_residual(
        _xattn_prologue, (q, kv, q_gain.reshape(1, -1), k_gain.reshape(1, -1)), in_specs, wo, x,
        tm=tm, tn=1024, in_vmem=2 * tm * D * 4 + 2 * MEM_LEN * 2 * D * 4, name="xattn_out")


def _dil_prep_body(x_ref, pos_ref, invf_ref, qg_ref, kg_ref, o_ref):
    ang = pos_ref[...] * invf_ref[...]
    lane = lax.broadcasted_iota(jnp.int32, ang.shape, 1)
    half = DIL_HEAD_DIM // 8
    cos, sin = jnp.cos(ang), jnp.sin(ang)
    sin_lo = jnp.where(lane < half, -sin, 0.0)
    sin_hi = jnp.where((lane >= half) & (lane < 2 * half), sin, 0.0)
    for part, g_ref in ((0, qg_ref), (1, kg_ref)):
        for g in range(DIL_GROUPS):
            gain = g_ref[g:g + 1, :]
            for h in range(DIL_HEADS):
                col = ((part * DIL_GROUPS + g) * DIL_HEADS + h) * DIL_HEAD_DIM
                xn = _rms(x_ref[:, col:col + DIL_HEAD_DIM], gain)
                o_ref[:, col:col + DIL_HEAD_DIM] = (
                    xn * cos + pltpu.roll(xn, DIL_HEAD_DIM - half, 1) * sin_lo + pltpu.roll(xn, half, 1) * sin_hi)


def _dil_attn_body(*refs, n_chunks):
    ins, o_ref, scr = refs[:15], refs[15], refs[16:]
    c = pl.program_id(1)
    scale = DIL_HEAD_DIM ** -0.5
    ii = lax.broadcasted_iota(jnp.int32, (DIL_BLOCK, DIL_BLOCK), 0)
    jj = lax.broadcasted_iota(jnp.int32, (DIL_BLOCK, DIL_BLOCK), 1)
    cur_mask = jj <= ii
    prev_mask = jj >= ii
    ch = o_ref.shape[0]
    for g, (_, dil) in enumerate(DIL_PATTERNS):
        q_ref, k_ref, v_ref, kh_ref, vh_ref = ins[5 * g:5 * g + 5]
        kf, vf, og, lg = scr[4 * g:4 * g + 4]
        hist = DIL_BLOCK * dil
        kf[0:hist, :] = kh_ref[...]
        vf[0:hist, :] = vh_ref[...]
        kf[hist:hist + ch, :] = k_ref[...]
        vf[hist:hist + ch, :] = v_ref[...]
        for blk in range(ch // hist):
            for r in range(dil):
                q0 = blk * hist + r
                rows_q = pl.ds(q0, DIL_BLOCK, stride=dil) if dil > 1 else pl.ds(q0, DIL_BLOCK)
                rows_c = pl.ds(hist + q0, DIL_BLOCK, stride=dil) if dil > 1 else pl.ds(hist + q0, DIL_BLOCK)
                rows_p = pl.ds(q0, DIL_BLOCK, stride=dil) if dil > 1 else pl.ds(q0, DIL_BLOCK)
                qv = q_ref[rows_q, :].astype(BF16)
                sc = jnp.where(cur_mask, _dot_nt(qv, kf[rows_c, :].astype(BF16)) * scale, NEG_INF)
                sp = _dot_nt(qv, kf[rows_p, :].astype(BF16)) * scale
                pm = prev_mask if blk > 0 else prev_mask & (c > 0)
                sp = jnp.where(pm, sp, NEG_INF)
                m = jnp.maximum(jnp.max(sc, -1, keepdims=True), jnp.max(sp, -1, keepdims=True))
                pc, pp = jnp.exp(sc - m), jnp.exp(sp - m)
                l = jnp.sum(pc, -1, keepdims=True) + jnp.sum(pp, -1, keepdims=True)
                o = (_dot(pc.astype(BF16), vf[rows_c, :].astype(BF16))
                     + _dot(pp.astype(BF16), vf[rows_p, :].astype(BF16))) / l
                og[rows_q, :] = o
                lg[rows_q, :] = jnp.broadcast_to(m + jnp.log(l), (DIL_BLOCK, DIL_HEAD_DIM))
    l0, l1, l2 = scr[3][...], scr[7][...], scr[11][...]
    mx = jnp.maximum(jnp.maximum(l0, l1), l2)
    e0, e1, e2 = jnp.exp(l0 - mx), jnp.exp(l1 - mx), jnp.exp(l2 - mx)
    o_ref[...] = ((e0 * scr[2][...] + e1 * scr[6][...] + e2 * scr[10][...]) / (e0 + e1 + e2)).astype(o_ref.dtype)


def dilated_mixer(x, B, S, positions, gain, w_qkv, q_gain, k_gain, w_out):
    T, D = x.shape
    nh = DIL_GROUPS * DIL_HEADS
    qkv = norm_matmul(x, gain, w_qkv)
    rot = DIL_HEAD_DIM // 4
    inv_freq = ROPE_THETA ** (-jnp.arange(0, rot, 2, dtype=F32) / rot)
    invf = jnp.concatenate([inv_freq, inv_freq, jnp.zeros((DIL_HEAD_DIM - rot,), F32)]).reshape(1, DIL_HEAD_DIM)
    pos = positions.astype(F32).reshape(T, 1)
    tp = 256
    qk_cols = 2 * nh * DIL_HEAD_DIM
    qk = pl.pallas_call(
        _dil_prep_body,
        grid=(T // tp,),
        in_specs=[
            pl.BlockSpec((tp, qk_cols), lambda i: (i, 0)),
            pl.BlockSpec((tp, 1), lambda i: (i, 0)),
            pl.BlockSpec((1, DIL_HEAD_DIM), lambda i: (0, 0)),
            pl.BlockSpec((DIL_GROUPS, DIL_HEAD_DIM), lambda i: (0, 0)),
            pl.BlockSpec((DIL_GROUPS, DIL_HEAD_DIM), lambda i: (0, 0)),
        ],
        out_specs=pl.BlockSpec((tp, qk_cols), lambda i: (i, 0)),
        out_shape=jax.ShapeDtypeStruct((T, qk_cols), F32),
        compiler_params=_cparams(("parallel",), 4 * tp * qk_cols * 4 + 2 * tp * V7X_LANES * 4),
        name="dil_qk_prep",
    )(qkv, pos, invf, q_gain, k_gain)

    ch = DIL_BLOCK * DIL_PATTERNS[-1][1]
    n_chunks = S // ch
    inputs, in_specs, scratch = [], [], []
    vmem = 2 * ch * DIL_HEAD_DIM * 2
    for g, (_, dil) in enumerate(DIL_PATTERNS):
        hist = DIL_BLOCK * dil
        per = ch // hist

        def cur_map(col):
            return lambda b, c, h: (b * n_chunks + c, col + h)

        def hist_map(col, per=per):
            return lambda b, c, h: (jnp.maximum((b * n_chunks + c) * per - 1, 0), col + h)

        inputs += [qk, qk, qkv, qk, qkv]
        in_specs += [
            pl.BlockSpec((ch, DIL_HEAD_DIM), cur_map(g * DIL_HEADS)),
            pl.BlockSpec((ch, DIL_HEAD_DIM), cur_map(nh + g * DIL_HEADS)),
            pl.BlockSpec((ch, DIL_HEAD_DIM), cur_map(2 * nh + g * DIL_HEADS)),
            pl.BlockSpec((hist, DIL_HEAD_DIM), hist_map(nh + g * DIL_HEADS)),
            pl.BlockSpec((hist, DIL_HEAD_DIM), hist_map(2 * nh + g * DIL_HEADS)),
        ]
        scratch += [pltpu.VMEM((hist + ch, DIL_HEAD_DIM), F32), pltpu.VMEM((hist + ch, DIL_HEAD_DIM), F32),
                    pltpu.VMEM((ch, DIL_HEAD_DIM), F32), pltpu.VMEM((ch, DIL_HEAD_DIM), F32)]
        vmem += (2 * (3 * ch + 2 * hist) + 2 * (hist + ch) + 2 * ch) * DIL_HEAD_DIM * 4
    o = pl.pallas_call(
        functools.partial(_dil_attn_body, n_chunks=n_chunks),
        grid=(B, n_chunks, DIL_HEADS),
        in_specs=in_specs,
        out_specs=pl.BlockSpec((ch, DIL_HEAD_DIM), lambda b, c, h: (b * n_chunks + c, h)),
        out_shape=jax.ShapeDtypeStruct((T, DIL_HEADS * DIL_HEAD_DIM), BF16),
        scratch_shapes=scratch,
        compiler_params=_cparams(("parallel", "arbitrary", "arbitrary"), vmem),
        name="dil_attention",
    )(*inputs)
    return matmul_residual(o, w_out, x)


def _hgrn_body(q_ref, f_ref, i_ref, gt_ref, lbl_ref, gain_ref, o_ref, st_scr, *, layer):
    @pl.when(pl.program_id(2) == 0)
    def _():
        st_scr[...] = jnp.zeros_like(st_scr)

    tt = q_ref.shape[0]
    lbl = lbl_ref[...]
    e = jnp.exp(lbl - jnp.max(lbl, axis=0, keepdims=True))
    p = e / jnp.sum(e, axis=0, keepdims=True)
    lb = jnp.sum(p[1:layer + 1, :], axis=0, keepdims=True)
    forget = lb + (1.0 - lb) * jax.nn.sigmoid(f_ref[...])
    k = 1.0 - forget
    gl = jnp.log(forget)
    rows = lax.broadcasted_iota(jnp.int32, (tt, tt), 0)
    cols = lax.broadcasted_iota(jnp.int32, (tt, tt), 1)
    same = _chunk_of(rows, HGRN_CHUNK) == _chunk_of(cols, HGRN_CHUNK)
    tri = same & (cols <= rows)
    a_cum = _dot_exact(tri.astype(F32), gl)
    a_tot = _dot_exact(same.astype(F32), gl)
    q_dec = (q_ref[...] * jnp.exp(a_cum)).astype(BF16)
    k_in = (k * jnp.exp(-a_cum)).astype(BF16)
    k_end = (k * jnp.exp(a_tot - a_cum)).astype(BF16)
    v = i_ref[...].astype(BF16)
    att = jnp.where(tri, _dot_nt(q_dec, k_in), 0.0)
    o = _dot(att.astype(BF16), v)
    dec = jnp.exp(a_tot)
    st = st_scr[...]
    chunks = [slice(c * HGRN_CHUNK, (c + 1) * HGRN_CHUNK) for c in range(tt // HGRN_CHUNK)]
    upd = [_dot_tn(v[sl], k_end[sl]) for sl in chunks]
    inter = []
    for sl, u in zip(chunks, upd):
        inter.append(_dot_nt(q_dec[sl], st.astype(BF16)))
        st = st * dec[sl.start:sl.start + 1, :] + u
    st_scr[...] = st
    o = o + jnp.concatenate(inter, axis=0)
    o_ref[...] = (_rms(o, gain_ref[...]) * _silu(gt_ref[...])).astype(o_ref.dtype)


def hgrn_mixer(x, B, S, layer, gain, w_in, lb_logits, norm_gain, w_out, *, tt=256):
    T, D = x.shape
    dh = D // HGRN_HEADS
    proj = norm_matmul(x, gain, w_in)
    tt = min(tt, S)
    nt = S // tt

    def part(pidx):
        return pl.BlockSpec((tt, dh), lambda b, h, s: (b * nt + s, pidx * HGRN_HEADS + h))

    o = pl.pallas_call(
        functools.partial(_hgrn_body, layer=layer),
        grid=(B, HGRN_HEADS, nt),
        in_specs=[part(0), part(1), part(2), part(3),
                  pl.BlockSpec((DEPTH, dh), lambda b, h, s: (0, h)),
                  pl.BlockSpec((1, dh), lambda b, h, s: (0, 0))],
        out_specs=pl.BlockSpec((tt, dh), lambda b, h, s: (b * nt + s, h)),
        out_shape=jax.ShapeDtypeStruct((T, D), BF16),
        scratch_shapes=[pltpu.VMEM((dh, dh), F32)],
        compiler_params=_cparams(("parallel", "parallel", "arbitrary"), 16 * tt * dh * 4 + 8 * tt * tt * 4),
        name="hgrn_core",
    )(proj, proj, proj, proj, lb_logits, norm_gain.reshape(1, dh))
    return matmul_residual(o, w_out, x)


def _rwkv_shift_mix(x_ref, xp_ref, gn_ref, first):
    gn = gn_ref[...]
    h = _rms(x_ref[...], gn)
    last = _rms(xp_ref[...], gn)[V7X_SUBLANES - 1:V7X_SUBLANES, :]
    last = jnp.where(first, 0.0, last)
    row = lax.broadcasted_iota(jnp.int32, h.shape, 0)
    return h, jnp.where(row == 0, last, pltpu.roll(h, 1, 0)) - h


def _rwkv_rkv_body(x_ref, xp_ref, gn_ref, mu_ref, wrkv_ref, rkv_ref, mix_scr, *, tiles_per_seq, n_col_tiles):
    n = pl.program_id(1)
    first = (pl.program_id(0) % tiles_per_seq) == 0

    @pl.when(n == 0)
    def _():
        h, d = _rwkv_shift_mix(x_ref, xp_ref, gn_ref, first)
        mu = mu_ref[...]
        for m in range(3):
            mix_scr[m] = (h + d * mu[m:m + 1, :]).astype(BF16)

    rkv_ref[...] = _dot(mix_scr[n // n_col_tiles], wrkv_ref[...])


def _rwkv_lora_body(x_ref, xp_ref, gn_ref, mu_ref, w0_ref, w1_ref, w2_ref, a0_ref, a1_ref, a2_ref, g1_ref, g2_ref,
                    lw_ref, a_ref, g_ref, *, tiles_per_seq):
    h, d = _rwkv_shift_mix(x_ref, xp_ref, gn_ref, (pl.program_id(0) % tiles_per_seq) == 0)
    mu = mu_ref[...]
    xw = (h + d * mu[3:4, :]).astype(BF16)
    xa = (h + d * mu[4:5, :]).astype(BF16)
    xg = (h + d * mu[5:6, :]).astype(BF16)
    z = -(w0_ref[...] + _dot(jnp.tanh(_dot(xw, w1_ref[...])).astype(BF16), w2_ref[...]))
    softplus = jnp.maximum(z, 0.0) + jnp.log1p(jnp.exp(-jnp.abs(z)))
    lw_ref[...] = -jnp.exp(-softplus - 0.5)
    a_ref[...] = jax.nn.sigmoid(a0_ref[...] + _dot(_dot(xa, a1_ref[...]).astype(BF16), a2_ref[...]))
    g_ref[...] = _dot(jax.nn.sigmoid(_dot(xg, g1_ref[...])).astype(BF16), g2_ref[...])


def _rwkv_core_body(r_ref, k_ref, v_ref, lw_ref, a_ref, g_ref, kk_ref, ka_ref, rk_ref, lnw_ref, lnb_ref,
                    o_ref, h_scr):
    @pl.when(pl.program_id(2) == 0)
    def _():
        h_scr[...] = jnp.zeros_like(h_scr)

    tt = r_ref.shape[0]
    C, N = RWKV_CHUNK, RWKV_HEAD_DIM
    lw = lw_ref[...]
    rows = lax.broadcasted_iota(jnp.int32, (tt, tt), 0)
    cols = lax.broadcasted_iota(jnp.int32, (tt, tt), 1)
    same = _chunk_of(rows, C) == _chunk_of(cols, C)
    g_cum = _dot_exact((same & (cols <= rows)).astype(F32), lw)
    g_tot = _dot_exact(same.astype(F32), lw)
    r, k, v, a = r_ref[...], k_ref[...], v_ref[...], a_ref[...]
    left = lax.broadcasted_iota(jnp.int32, (tt, 2 * N), 1) < N

    def head_sum(t):
        return jnp.where(left, jnp.sum(jnp.where(left, t, 0.0), -1, keepdims=True),
                         jnp.sum(jnp.where(left, 0.0, t), -1, keepdims=True))

    kk = k * kk_ref[...]
    kk = kk * lax.rsqrt(jnp.maximum(head_sum(kk * kk), 1e-24))
    kmod = k * (1.0 + (a - 1.0) * ka_ref[...])
    bv = kk * a
    e_neg = jnp.exp(-g_cum)
    e_end = jnp.exp(g_tot - g_cum)
    a_t = (-kk) * jnp.exp(g_cum - lw)
    r_t = r * jnp.exp(g_cum)
    k_t, b_t = kmod * e_neg, bv * e_neg
    k_h, b_h = kmod * e_end, bv * e_end
    dec = jnp.exp(g_tot)
    i64 = lax.broadcasted_iota(jnp.int32, (C, C), 0)
    j64 = lax.broadcasted_iota(jnp.int32, (C, C), 1)
    strict, incl, eye = j64 < i64, j64 <= i64, (i64 == j64).astype(F32)

    nc = tt // C
    units = [(hh, c) for hh in range(2) for c in range(nc)]

    def cut(t, u):
        return t[u[1] * C:(u[1] + 1) * C, u[0] * N:(u[0] + 1) * N]

    a_c = [cut(a_t, u).astype(BF16) for u in units]
    r_c = [cut(r_t, u) for u in units]
    v_c = [cut(v, u).astype(BF16) for u in units]
    bh_c = [cut(b_h, u).astype(BF16) for u in units]
    kh_c = [cut(k_h, u).astype(BF16) for u in units]
    prod = [_dot_nt(jnp.concatenate([a, rr.astype(BF16)], axis=0),
                    jnp.concatenate([cut(b_t, u), cut(k_t, u)], axis=0).astype(BF16))
            for a, rr, u in zip(a_c, r_c, units)]
    n_ab = [jnp.where(strict, p[:C, :C], 0.0) for p in prod]
    n_ak = [jnp.where(strict, p[:C, C:], 0.0).astype(BF16) for p in prod]
    t_rb = [jnp.where(incl, p[C:, :C], 0.0).astype(BF16) for p in prod]
    t_rk = [jnp.where(incl, p[C:, C:], 0.0).astype(BF16) for p in prod]
    n16 = [n.astype(BF16) for n in n_ab]
    pw = [_dot(n, n) for n in n16]
    nakv = [_dot(n, vv).astype(BF16) for n, vv in zip(n_ak, v_c)]
    inv = [eye + n for n in n_ab]
    for step in range(5):
        pw16 = [p.astype(BF16) for p in pw]
        inv16 = [i.astype(BF16) for i in inv]
        if step < 4:
            pw = [_dot(p, p) for p in pw16]
        inv = [i + _dot(i16, p) for i, i16, p in zip(inv, inv16, pw16)]
    inv16 = [i.astype(BF16) for i in inv]
    a_p = [_dot(i, a).astype(BF16) for i, a in zip(inv16, a_c)]
    u0 = [_dot(i, n).astype(BF16) for i, n in zip(inv16, nakv)]
    r_p = [(rr + _dot(t, a)).astype(BF16) for rr, t, a in zip(r_c, t_rb, a_p)]
    y0 = [_dot(tb, u) + _dot(tk, vv) for tb, u, tk, vv in zip(t_rb, u0, t_rk, v_c)]
    trans_t = [_dot_tn(a, b).astype(BF16) for a, b in zip(a_p, bh_c)]
    h_add = [_dot_tn(u, b) + _dot_tn(vv, kk_) for u, b, vv, kk_ in zip(u0, bh_c, v_c, kh_c)]

    ht = [h_scr[hh] for hh in range(2)]
    ys = [[], []]
    for c in range(nc):
        for hh in range(2):
            i = hh * nc + c
            ht16 = ht[hh].astype(BF16)
            ys[hh].append(_dot_nt(r_p[i], ht16) + y0[i])
            ht[hh] = ht[hh] * dec[c * C:c * C + 1, hh * N:(hh + 1) * N] + _dot(ht16, trans_t[i]) + h_add[i]
    y_heads = []
    for hh in range(2):
        h_scr[hh] = ht[hh]
        y = jnp.concatenate(ys[hh], axis=0)
        mean = jnp.mean(y, -1, keepdims=True)
        var = jnp.mean(jnp.square(y - mean), -1, keepdims=True)
        y_heads.append((y - mean) * lax.rsqrt(var + RWKV_GN_EPS))
    yn = jnp.concatenate(y_heads, axis=1) * lnw_ref[...] + lnb_ref[...]
    bonus = head_sum(r * kmod * rk_ref[...]) * v
    o_ref[...] = ((yn + bonus) * g_ref[...]).astype(o_ref.dtype)


def rwkv_mixer(x, B, S, gain, mu, w_rkv, w0, w1, w2, a0, a1, a2, g1, g2, k_k, k_a, r_k, ln_w, ln_b, w_out,
               *, tm=512, tn=1024, tl=256, tt=256):
    T, D = x.shape
    tm, tl, tt = min(tm, S), min(tl, S), min(tt, S)
    nct = D // tn
    row = lambda i, n: (0, 0)

    def prev_rows(t):
        return lambda i, *_: (jnp.maximum(i * (t // V7X_SUBLANES) - 1, 0), 0)

    rkv = pl.pallas_call(
        functools.partial(_rwkv_rkv_body, tiles_per_seq=S // tm, n_col_tiles=nct),
        grid=(T // tm, 3 * nct),
        in_specs=[
            pl.BlockSpec((tm, D), lambda i, n: (i, 0)),
            pl.BlockSpec((V7X_SUBLANES, D), prev_rows(tm)),
            pl.BlockSpec((1, D), row),
            pl.BlockSpec((6, D), row),
            pl.BlockSpec((None, D, tn), lambda i, n: (n // nct, 0, n % nct)),
        ],
        out_specs=pl.BlockSpec((tm, tn), lambda i, n: (i, n)),
        out_shape=jax.ShapeDtypeStruct((T, 3 * D), F32),
        scratch_shapes=[pltpu.VMEM((3, tm, D), BF16)],
        compiler_params=_cparams(("parallel", "arbitrary"),
                                 2 * tm * D * 4 + 3 * tm * D * 2 + 2 * D * tn * 2 + 2 * tm * tn * 4 + 3 * tm * D * 4),
        name="rwkv_rkv",
    )(x, x, gain.reshape(1, D), mu, w_rkv)

    lora = w1.shape[1]
    pad = (-lora) % V7X_LANES
    w1p, a1p = jnp.pad(w1, ((0, 0), (0, pad))), jnp.pad(a1, ((0, 0), (0, pad)))
    w2p, a2p = jnp.pad(w2, ((0, pad), (0, 0))), jnp.pad(a2, ((0, pad), (0, 0)))
    lp, gl = lora + pad, g1.shape[1]
    one = lambda i: (0, 0)
    tok_l = pl.BlockSpec((tl, D), lambda i: (i, 0))
    lw, a, g = pl.pallas_call(
        functools.partial(_rwkv_lora_body, tiles_per_seq=S // tl),
        grid=(T // tl,),
        in_specs=[
            tok_l,
            pl.BlockSpec((V7X_SUBLANES, D), prev_rows(tl)),
            pl.BlockSpec((1, D), one),
            pl.BlockSpec((6, D), one),
            pl.BlockSpec((1, D), one), pl.BlockSpec((D, lp), one), pl.BlockSpec((lp, D), one),
            pl.BlockSpec((1, D), one), pl.BlockSpec((D, lp), one), pl.BlockSpec((lp, D), one),
            pl.BlockSpec((D, gl), one), pl.BlockSpec((gl, D), one),
        ],
        out_specs=[tok_l, tok_l, tok_l],
        out_shape=[jax.ShapeDtypeStruct((T, D), F32)] * 3,
        compiler_params=_cparams(("parallel",), 14 * tl * D * 4 + 4 * (2 * D * lp + D * gl) * 2),
        name="rwkv_lora",
    )(x, x, gain.reshape(1, D), mu, w0.reshape(1, D), w1p, w2p, a0.reshape(1, D), a1p, a2p, g1, g2)

    nt = S // tt
    pw = 2 * RWKV_HEAD_DIM
    npair = D // pw

    def tok(col0):
        return pl.BlockSpec((tt, pw), lambda b, p, s: (b * nt + s, col0 + p))

    par = pl.BlockSpec((1, pw), lambda b, p, s: (0, p))
    o = pl.pallas_call(
        _rwkv_core_body,
        grid=(B, npair, nt),
        in_specs=[tok(0), tok(npair), tok(2 * npair), tok(0), tok(0), tok(0), par, par, par, par, par],
        out_specs=pl.BlockSpec((tt, pw), lambda b, p, s: (b * nt + s, p)),
        out_shape=jax.ShapeDtypeStruct((T, D), BF16),
        scratch_shapes=[pltpu.VMEM((2, RWKV_HEAD_DIM, RWKV_HEAD_DIM), F32)],
        compiler_params=_cparams(("parallel", "parallel", "arbitrary"), 40 * tt * pw * 4 + 8 * tt * tt * 4),
        name="rwkv_core",
    )(rkv, rkv, rkv, lw, a, g, k_k.reshape(1, D), k_a.reshape(1, D), r_k.reshape(1, D),
      ln_w.reshape(1, D), ln_b.reshape(1, D))
    return matmul_residual(o, w_out, x)


def kernel(x, mem, positions, ffn_norm, ffn_w_gate, ffn_w_up, ffn_w_down, mix_norm, xattn_norm, mem_norm, xattn_wq, xattn_wkv, xattn_wo, xattn_q_gain, xattn_k_gain, conv_w_in, conv_w, conv_w_out, dil_w_qkv, dil_q_gain, dil_k_gain, dil_w_out, hgrn_w_in, hgrn_lb_logits, hgrn_norm, hgrn_w_out, rwkv_mu, rwkv_w_rkv, rwkv_w0, rwkv_w1, rwkv_w2, rwkv_a0, rwkv_a1, rwkv_a2, rwkv_g1, rwkv_g2, rwkv_k_k, rwkv_k_a, rwkv_r_k, rwkv_ln_w, rwkv_ln_b, rwkv_w_out):
    B, S, D = x.shape
    assert D == D_MODEL and S % (DIL_BLOCK * DIL_PATTERNS[-1][1]) == 0
    depth = ffn_norm.shape[0]
    xf = x.reshape(B * S, D)
    memf = mem.reshape(B * MEM_LEN, D)
    bf = lambda w: w.astype(BF16)
    for i in range(depth):
        kind, j = i % N_MIXERS, i // N_MIXERS
        xf = ffn_half(xf, ffn_norm[i, 0], bf(ffn_w_gate[i, 0]), bf(ffn_w_up[i, 0]), bf(ffn_w_down[i, 0]))
        if kind == 0:
            xf = conv_mixer(xf, S, mix_norm[i], bf(conv_w_in[j]), conv_w[j], bf(conv_w_out[j]))
        elif kind == 1:
            xf = dilated_mixer(xf, B, S, positions, mix_norm[i], bf(dil_w_qkv[j]), dil_q_gain[j], dil_k_gain[j],
                               bf(dil_w_out[j]))
        elif kind == 2:
            xf = hgrn_mixer(xf, B, S, i, mix_norm[i], bf(hgrn_w_in[j]), hgrn_lb_logits, hgrn_norm[j],
                            bf(hgrn_w_out[j]))
        else:
            xf = rwkv_mixer(xf, B, S, mix_norm[i], rwkv_mu[j], bf(rwkv_w_rkv[j]), rwkv_w0[j], bf(rwkv_w1[j]),
                            bf(rwkv_w2[j]), rwkv_a0[j], bf(rwkv_a1[j]), bf(rwkv_a2[j]), bf(rwkv_g1[j]),
                            bf(rwkv_g2[j]), rwkv_k_k[j], rwkv_k_a[j], rwkv_r_k[j], rwkv_ln_w[j], rwkv_ln_b[j],
                            bf(rwkv_w_out[j]))
        xf = cross_attention(xf, S, memf, xattn_norm[i], mem_norm[i], bf(xattn_wq[i]), bf(xattn_wkv[i]),
                             bf(xattn_wo[i]), xattn_q_gain[i], xattn_k_gain[i])
        xf = ffn_half(xf, ffn_norm[i, 1], bf(ffn_w_gate[i, 1]), bf(ffn_w_up[i, 1]), bf(ffn_w_down[i, 1]))
    return xf.reshape(B, S, D)
```

```python
import functools

import jax
import jax.numpy as jnp
from jax import lax
from jax.experimental import pallas as pl
from jax.experimental.pallas import tpu as pltpu

F32 = jnp.float32
BF16 = jnp.bfloat16

D_MODEL = 2048
DEPTH = 4
N_MIXERS = 4
MEM_LEN = 256
FFN_DIM = 5632
NORM_EPS = 1e-6
NEG_INF = -1e30
ROPE_THETA = 500000.0
DIL_PATTERNS = ((128, 1), (512, 4), (2048, 16))
DIL_GROUPS = 3
DIL_HEADS = 8
DIL_HEAD_DIM = 128
DIL_BLOCK = 128
HGRN_CHUNK = 16
HGRN_HEADS = 16
RWKV_HEAD_DIM = 64
RWKV_HEADS = 32
RWKV_CHUNK = 64
RWKV_GN_EPS = 64e-5
XATTN_HEADS = 4
XATTN_HEAD_DIM = 512

V7X_LANES = 128
V7X_SUBLANES = 8
V7X_VMEM_BYTES = 64 * 2**20
V7X_VMEM_CAP = 56 * 2**20


def _cparams(sem, vmem_bytes):
    limit = min(int(vmem_bytes * 1.25) + (4 << 20), V7X_VMEM_CAP)
    return pltpu.CompilerParams(dimension_semantics=sem, vmem_limit_bytes=limit)


def _rms(x, gain):
    return x * lax.rsqrt(jnp.mean(x * x, axis=-1, keepdims=True) + NORM_EPS) * gain


def _dot(a, b):
    return jnp.dot(a, b, preferred_element_type=F32)


def _dot_nt(a, b):
    return lax.dot_general(a, b, (((1,), (1,)), ((), ())), preferred_element_type=F32)


def _dot_tn(a, b):
    return lax.dot_general(a, b, (((0,), (0,)), ((), ())), preferred_element_type=F32)


def _wshape(w):
    arr, lead = w
    return arr.shape[len(lead):]


def _wspec(w, block, tail):
    lead = tuple(w[1])
    return pl.BlockSpec((None,) * len(lead) + tuple(block), lambda *g: lead + tuple(tail(*g)))


def _chunk_sums(x, chunk):
    t, w = x.shape
    rows = lax.broadcasted_iota(jnp.int32, (t, t), 0)
    cols = lax.broadcasted_iota(jnp.int32, (t, t), 1)
    same = _chunk_of(rows, chunk) == _chunk_of(cols, chunk)
    tri = same & (cols <= rows)
    sel = jnp.concatenate([tri.astype(BF16), same.astype(BF16)], axis=0)
    hi = x.astype(BF16)
    r1 = x - hi.astype(F32)
    mid = r1.astype(BF16)
    lo = (r1 - mid.astype(F32)).astype(BF16)
    s = _dot(sel, jnp.concatenate([hi, mid, lo], axis=1))
    s = s[:, :w] + s[:, w:2 * w] + s[:, 2 * w:]
    return s[:t], s[t:], tri


def _chunk_of(idx, chunk):
    return jnp.right_shift(idx, chunk.bit_length() - 1)


def _silu(x):
    return x * jax.nn.sigmoid(x)


def _norm_matmul_body(x_ref, g_ref, w_ref, o_ref, h_scr):
    @pl.when(pl.program_id(1) == 0)
    def _():
        h_scr[...] = _rms(x_ref[...], g_ref[...]).astype(BF16)

    o_ref[...] = _dot(h_scr[...], w_ref[...]).astype(o_ref.dtype)


def norm_matmul(x, gain, w, *, tm=1024, tn=1024, out_dtype=F32):
    M, K = x.shape
    N = _wshape(w)[1]
    tm, tn = min(tm, M), min(tn, N)
    ob = jnp.dtype(out_dtype).itemsize
    vmem = 2 * tm * K * 4 + tm * K * 2 + 2 * K * tn * 2 + 2 * tm * tn * ob
    return pl.pallas_call(
        _norm_matmul_body,
        grid=(M // tm, N // tn),
        in_specs=[
            pl.BlockSpec((tm, K), lambda i, j: (i, 0)),
            pl.BlockSpec((1, K), lambda i, j: (0, 0)),
            _wspec(w, (K, tn), lambda i, j: (0, j)),
        ],
        out_specs=pl.BlockSpec((tm, tn), lambda i, j: (i, j)),
        out_shape=jax.ShapeDtypeStruct((M, N), out_dtype),
        scratch_shapes=[pltpu.VMEM((tm, K), BF16)],
        compiler_params=_cparams(("parallel", "arbitrary"), vmem),
        name="norm_matmul",
    )(x, gain.reshape(1, K), w[0])


def _ffn_body(x_ref, g_ref, wg_ref, wu_ref, wd_ref, o_ref, h_scr):
    @pl.when(pl.program_id(1) == 0)
    def _():
        x = x_ref[...]
        h_scr[...] = _rms(x, g_ref[...]).astype(BF16)
        o_ref[...] = x

    h = h_scr[...]
    act = _silu(_dot(h, wg_ref[...])) * _dot(h, wu_ref[...])
    o_ref[...] += 0.5 * _dot(act.astype(BF16), wd_ref[...])


def ffn_half(x, gain, wg, wu, wd, *, tm=1024, tf=512):
    M, D = x.shape
    F = _wshape(wg)[1]
    tm = min(tm, M)
    vmem = 3 * tm * D * 4 + tm * D * 2 + 2 * 3 * D * tf * 2 + 3 * tm * tf * 4
    return pl.pallas_call(
        _ffn_body,
        grid=(M // tm, F // tf),
        in_specs=[
            pl.BlockSpec((tm, D), lambda i, f: (i, 0), pipeline_mode=pl.Buffered(1)),
            pl.BlockSpec((1, D), lambda i, f: (0, 0)),
            _wspec(wg, (D, tf), lambda i, f: (0, f)),
            _wspec(wu, (D, tf), lambda i, f: (0, f)),
            _wspec(wd, (tf, D), lambda i, f: (f, 0)),
        ],
        out_specs=pl.BlockSpec((tm, D), lambda i, f: (i, 0)),
        out_shape=jax.ShapeDtypeStruct((M, D), F32),
        scratch_shapes=[pltpu.VMEM((tm, D), BF16)],
        compiler_params=_cparams(("parallel", "arbitrary"), vmem),
        name="ffn_half",
    )(x, gain.reshape(1, D), wg[0], wu[0], wd[0])


def _matmul_res_body(a_ref, w_ref, r_ref, o_ref):
    o_ref[...] = r_ref[...] + _dot(a_ref[...], w_ref[...])


def matmul_residual(a, w, res, *, tm=1024, tn=1024):
    M, K = a.shape
    N = _wshape(w)[1]
    tm = min(tm, M)
    vmem = 2 * tm * K * 2 + 2 * K * tn * 2 + 4 * tm * tn * 4
    return pl.pallas_call(
        _matmul_res_body,
        grid=(M // tm, N // tn),
        in_specs=[
            pl.BlockSpec((tm, K), lambda i, j: (i, 0)),
            _wspec(w, (K, tn), lambda i, j: (0, j)),
            pl.BlockSpec((tm, tn), lambda i, j: (i, j)),
        ],
        out_specs=pl.BlockSpec((tm, tn), lambda i, j: (i, j)),
        out_shape=jax.ShapeDtypeStruct((M, N), F32),
        compiler_params=_cparams(("parallel", "arbitrary"), vmem),
        name="matmul_residual",
    )(a, w[0], res)


def _prologue_matmul_res_body(prologue, n_in, *refs):
    in_refs = refs[:n_in]
    w_ref, r_ref, o_ref, lhs_scr = refs[n_in:]
    row_tile = pl.program_id(0)

    @pl.when(pl.program_id(1) == 0)
    def _():
        prologue(row_tile, *in_refs, lhs_scr)

    o_ref[...] = r_ref[...] + _dot(lhs_scr[...], w_ref[...])


def prologue_matmul_residual(prologue, inputs, in_specs, w, res, *, tm, tn, in_vmem, name):
    M, N = res.shape
    K = _wshape(w)[0]
    vmem = in_vmem + tm * K * 2 + 2 * K * tn * 2 + 4 * tm * tn * 4
    return pl.pallas_call(
        functools.partial(_prologue_matmul_res_body, prologue, len(inputs)),
        grid=(M // tm, N // tn),
        in_specs=list(in_specs) + [
            _wspec(w, (K, tn), lambda i, j: (0, j)),
            pl.BlockSpec((tm, tn), lambda i, j: (i, j)),
        ],
        out_specs=pl.BlockSpec((tm, tn), lambda i, j: (i, j)),
        out_shape=jax.ShapeDtypeStruct((M, N), F32),
        scratch_shapes=[pltpu.VMEM((tm, K), BF16)],
        compiler_params=_cparams(("parallel", "arbitrary"), vmem),
        name=name,
    )(*inputs, w[0], res)


def _conv_prologue(tiles_per_seq, row_tile, b_ref, c_ref, u_ref, cp_ref, up_ref, cw_ref, lhs_scr):
    cu = c_ref[...] * u_ref[...]
    prev = cp_ref[...] * up_ref[...]
    first = (row_tile % tiles_per_seq) == 0
    prev = jnp.where(first, 0.0, prev)
    p1, p2 = prev[7:8, :], prev[6:7, :]
    row = lax.broadcasted_iota(jnp.int32, cu.shape, 0)
    s1 = jnp.where(row == 0, p1, pltpu.roll(cu, 1, 0))
    s2 = jnp.where(row == 0, p2, jnp.where(row == 1, p1, pltpu.roll(cu, 2, 0)))
    w = cw_ref[...]
    y = w[0:1, :] * s2 + w[1:2, :] * s1 + w[2:3, :] * cu
    lhs_scr[...] = (b_ref[...] * y).astype(BF16)


def conv_mixer(x, S, gain, w_in, conv_w, w_out, *, tm=512):
    T, D = x.shape
    tm = min(tm, S)
    bcu = norm_matmul(x, gain, w_in)
    r8 = tm // V7X_SUBLANES

    def prev_map(col):
        return lambda i, j: (jnp.maximum(i * r8 - 1, 0), col)

    in_specs = [
        pl.BlockSpec((tm, D), lambda i, j: (i, 0)),
        pl.BlockSpec((tm, D), lambda i, j: (i, 1)),
        pl.BlockSpec((tm, D), lambda i, j: (i, 2)),
        pl.BlockSpec((V7X_SUBLANES, D), prev_map(1)),
        pl.BlockSpec((V7X_SUBLANES, D), prev_map(2)),
        pl.BlockSpec((3, D), lambda i, j: (0, 0)),
    ]
    return prologue_matmul_residual(
        functools.partial(_conv_prologue, S // tm), (bcu, bcu, bcu, bcu, bcu, conv_w), in_specs, w_out, x,
        tm=tm, tn=1024, in_vmem=2 * 3 * tm * D * 4 + 4 * V7X_SUBLANES * D * 4, name="conv_mixer_out")


def _xattn_prologue(row_tile, q_ref, kv_ref, qg_ref, kg_ref, lhs_scr):
    del row_tile
    scale = XATTN_HEAD_DIM ** -0.5
    D = XATTN_HEADS * XATTN_HEAD_DIM
    for h in range(XATTN_HEADS):
        sl = slice(h * XATTN_HEAD_DIM, (h + 1) * XATTN_HEAD_DIM)
        qn = _rms(q_ref[:, sl], qg_ref[...]).astype(BF16)
        kn = _rms(kv_ref[:, sl], kg_ref[...]).astype(BF16)
        v = kv_ref[:, D + h * XATTN_HEAD_DIM:D + (h + 1) * XATTN_HEAD_DIM].astype(BF16)
        s = _dot_nt(qn, kn) * scale
        p = jnp.exp(s - jnp.max(s, axis=-1, keepdims=True))
        l = jnp.sum(p, axis=-1, keepdims=True)
        lhs_scr[:, sl] = (_dot(p.astype(BF16), v) / l).astype(BF16)


def cross_attention(x, S, mem, xgain, mgain, wq, wkv, wo, q_gain, k_gain, *, tm=512):
    T, D = x.shape
    tm = min(tm, S)
    q = norm_matmul(x, xgain, wq)
    kv = norm_matmul(mem, mgain, wkv)
    tps = S // tm
    in_specs = [
        pl.BlockSpec((tm, D), lambda i, j: (i, 0)),
        pl.BlockSpec((MEM_LEN, 2 * D), lambda i, j: (i // tps, 0)),
        pl.BlockSpec((1, XATTN_HEAD_DIM), lambda i, j: (0, 0)),
        pl.BlockSpec((1, XATTN_HEAD_DIM), lambda i, j: (0, 0)),
    ]
    return prologue_matmul_residual(
        _xattn_prologue, (q, kv, q_gain.reshape(1, -1), k_gain.reshape(1, -1)), in_specs, wo, x,
        tm=tm, tn=1024, in_vmem=2 * tm * D * 4 + 2 * MEM_LEN * 2 * D * 4, name="xattn_out")


def _dil_prep_body(x_ref, pos_ref, invf_ref, qg_ref, kg_ref, o_ref):
    ang = pos_ref[...] * invf_ref[...]
    lane = lax.broadcasted_iota(jnp.int32, ang.shape, 1)
    half = DIL_HEAD_DIM // 8
    cos, sin = jnp.cos(ang), jnp.sin(ang)
    sin_lo = jnp.where(lane < half, -sin, 0.0)
    sin_hi = jnp.where((lane >= half) & (lane < 2 * half), sin, 0.0)
    for part, g_ref in ((0, qg_ref), (1, kg_ref)):
        for g in range(DIL_GROUPS):
            gain = g_ref[g:g + 1, :]
            for h in range(DIL_HEADS):
                col = ((part * DIL_GROUPS + g) * DIL_HEADS + h) * DIL_HEAD_DIM
                xn = _rms(x_ref[:, col:col + DIL_HEAD_DIM], gain)
                o_ref[:, col:col + DIL_HEAD_DIM] = (
                    xn * cos + pltpu.roll(xn, DIL_HEAD_DIM - half, 1) * sin_lo + pltpu.roll(xn, half, 1) * sin_hi)


def _dil_attn_body(*refs, n_chunks):
    ins, o_ref, scr = refs[:15], refs[15], refs[16:]
    c = pl.program_id(1)
    scale = DIL_HEAD_DIM ** -0.5
    ii = lax.broadcasted_iota(jnp.int32, (DIL_BLOCK, DIL_BLOCK), 0)
    jj = lax.broadcasted_iota(jnp.int32, (DIL_BLOCK, DIL_BLOCK), 1)
    cur_mask = jj <= ii
    prev_mask = jj >= ii
    ch = o_ref.shape[0]
    for g, (_, dil) in enumerate(DIL_PATTERNS):
        q_ref, k_ref, v_ref, kh_ref, vh_ref = ins[5 * g:5 * g + 5]
        kf, vf, og, lg = scr[4 * g:4 * g + 4]
        hist = DIL_BLOCK * dil
        kf[0:hist, :] = kh_ref[...]
        vf[0:hist, :] = vh_ref[...]
        kf[hist:hist + ch, :] = k_ref[...]
        vf[hist:hist + ch, :] = v_ref[...]
        for blk in range(ch // hist):
            for r in range(dil):
                q0 = blk * hist + r
                rows_q = pl.ds(q0, DIL_BLOCK, stride=dil) if dil > 1 else pl.ds(q0, DIL_BLOCK)
                rows_c = pl.ds(hist + q0, DIL_BLOCK, stride=dil) if dil > 1 else pl.ds(hist + q0, DIL_BLOCK)
                rows_p = pl.ds(q0, DIL_BLOCK, stride=dil) if dil > 1 else pl.ds(q0, DIL_BLOCK)
                qv = q_ref[rows_q, :].astype(BF16)
                sc = jnp.where(cur_mask, _dot_nt(qv, kf[rows_c, :].astype(BF16)) * scale, NEG_INF)
                sp = _dot_nt(qv, kf[rows_p, :].astype(BF16)) * scale
                pm = prev_mask if blk > 0 else prev_mask & (c > 0)
                sp = jnp.where(pm, sp, NEG_INF)
                m = jnp.maximum(jnp.max(sc, -1, keepdims=True), jnp.max(sp, -1, keepdims=True))
                pc, pp = jnp.exp(sc - m), jnp.exp(sp - m)
                l = jnp.sum(pc, -1, keepdims=True) + jnp.sum(pp, -1, keepdims=True)
                o = (_dot(pc.astype(BF16), vf[rows_c, :].astype(BF16))
                     + _dot(pp.astype(BF16), vf[rows_p, :].astype(BF16))) / l
                og[rows_q, :] = o
                lg[rows_q, :] = jnp.broadcast_to(m + jnp.log(l), (DIL_BLOCK, DIL_HEAD_DIM))
    l0, l1, l2 = scr[3][...], scr[7][...], scr[11][...]
    mx = jnp.maximum(jnp.maximum(l0, l1), l2)
    e0, e1, e2 = jnp.exp(l0 - mx), jnp.exp(l1 - mx), jnp.exp(l2 - mx)
    o_ref[...] = ((e0 * scr[2][...] + e1 * scr[6][...] + e2 * scr[10][...]) / (e0 + e1 + e2)).astype(o_ref.dtype)


def dilated_mixer(x, B, S, positions, gain, w_qkv, q_gain, k_gain, w_out):
    T, D = x.shape
    nh = DIL_GROUPS * DIL_HEADS
    qkv = norm_matmul(x, gain, w_qkv)
    rot = DIL_HEAD_DIM // 4
    inv_freq = ROPE_THETA ** (-jnp.arange(0, rot, 2, dtype=F32) / rot)
    invf = jnp.concatenate([inv_freq, inv_freq, jnp.zeros((DIL_HEAD_DIM - rot,), F32)]).reshape(1, DIL_HEAD_DIM)
    pos = positions.astype(F32).reshape(T, 1)
    tp = 256
    qk_cols = 2 * nh * DIL_HEAD_DIM
    qk = pl.pallas_call(
        _dil_prep_body,
        grid=(T // tp,),
        in_specs=[
            pl.BlockSpec((tp, qk_cols), lambda i: (i, 0)),
            pl.BlockSpec((tp, 1), lambda i: (i, 0)),
            pl.BlockSpec((1, DIL_HEAD_DIM), lambda i: (0, 0)),
            pl.BlockSpec((DIL_GROUPS, DIL_HEAD_DIM), lambda i: (0, 0)),
            pl.BlockSpec((DIL_GROUPS, DIL_HEAD_DIM), lambda i: (0, 0)),
        ],
        out_specs=pl.BlockSpec((tp, qk_cols), lambda i: (i, 0)),
        out_shape=jax.ShapeDtypeStruct((T, qk_cols), F32),
        compiler_params=_cparams(("parallel",), 4 * tp * qk_cols * 4 + 2 * tp * V7X_LANES * 4),
        name="dil_qk_prep",
    )(qkv, pos, invf, q_gain, k_gain)

    ch = DIL_BLOCK * DIL_PATTERNS[-1][1]
    n_chunks = S // ch
    inputs, in_specs, scratch = [], [], []
    vmem = 2 * ch * DIL_HEAD_DIM * 2
    for g, (_, dil) in enumerate(DIL_PATTERNS):
        hist = DIL_BLOCK * dil
        per = ch // hist

        def cur_map(col):
            return lambda b, c, h: (b * n_chunks + c, col + h)

        def hist_map(col, per=per):
            return lambda b, c, h: (jnp.maximum((b * n_chunks + c) * per - 1, 0), col + h)

        inputs += [qk, qk, qkv, qk, qkv]
        in_specs += [
            pl.BlockSpec((ch, DIL_HEAD_DIM), cur_map(g * DIL_HEADS)),
            pl.BlockSpec((ch, DIL_HEAD_DIM), cur_map(nh + g * DIL_HEADS)),
            pl.BlockSpec((ch, DIL_HEAD_DIM), cur_map(2 * nh + g * DIL_HEADS)),
            pl.BlockSpec((hist, DIL_HEAD_DIM), hist_map(nh + g * DIL_HEADS)),
            pl.BlockSpec((hist, DIL_HEAD_DIM), hist_map(2 * nh + g * DIL_HEADS)),
        ]
        scratch += [pltpu.VMEM((hist + ch, DIL_HEAD_DIM), F32), pltpu.VMEM((hist + ch, DIL_HEAD_DIM), F32),
                    pltpu.VMEM((ch, DIL_HEAD_DIM), F32), pltpu.VMEM((ch, DIL_HEAD_DIM), F32)]
        vmem += (2 * (3 * ch + 2 * hist) + 2 * (hist + ch) + 2 * ch) * DIL_HEAD_DIM * 4
    o = pl.pallas_call(
        functools.partial(_dil_attn_body, n_chunks=n_chunks),
        grid=(B, n_chunks, DIL_HEADS),
        in_specs=in_specs,
        out_specs=pl.BlockSpec((ch, DIL_HEAD_DIM), lambda b, c, h: (b * n_chunks + c, h)),
        out_shape=jax.ShapeDtypeStruct((T, DIL_HEADS * DIL_HEAD_DIM), BF16),
        scratch_shapes=scratch,
        compiler_params=_cparams(("parallel", "arbitrary", "arbitrary"), vmem),
        name="dil_attention",
    )(*inputs)
    return matmul_residual(o, w_out, x)


def _hgrn_body(q_ref, f_ref, i_ref, gt_ref, lbl_ref, gain_ref, o_ref, st_scr, *, layer):
    @pl.when(pl.program_id(2) == 0)
    def _():
        st_scr[...] = jnp.zeros_like(st_scr)

    tt = q_ref.shape[0]
    lbl = lbl_ref[...]
    e = jnp.exp(lbl - jnp.max(lbl, axis=0, keepdims=True))
    p = e / jnp.sum(e, axis=0, keepdims=True)
    lb = jnp.sum(p[1:layer + 1, :], axis=0, keepdims=True)
    forget = lb + (1.0 - lb) * jax.nn.sigmoid(f_ref[...])
    k = 1.0 - forget
    gl = jnp.log(forget)
    a_cum, a_tot, tri = _chunk_sums(gl, HGRN_CHUNK)
    q_dec = (q_ref[...] * jnp.exp(a_cum)).astype(BF16)
    k_in = (k * jnp.exp(-a_cum)).astype(BF16)
    k_end = (k * jnp.exp(a_tot - a_cum)).astype(BF16)
    v = i_ref[...].astype(BF16)
    att = jnp.where(tri, _dot_nt(q_dec, k_in), 0.0)
    o = _dot(att.astype(BF16), v)
    dec = jnp.exp(a_tot)
    st = st_scr[...]
    chunks = [slice(c * HGRN_CHUNK, (c + 1) * HGRN_CHUNK) for c in range(tt // HGRN_CHUNK)]
    upd = [_dot_tn(v[sl], k_end[sl]) for sl in chunks]
    inter = []
    for sl, u in zip(chunks, upd):
        inter.append(_dot_nt(q_dec[sl], st.astype(BF16)))
        st = st * dec[sl.start:sl.start + 1, :] + u
    st_scr[...] = st
    o = o + jnp.concatenate(inter, axis=0)
    o_ref[...] = (_rms(o, gain_ref[...]) * _silu(gt_ref[...])).astype(o_ref.dtype)


def hgrn_mixer(x, B, S, layer, gain, w_in, lb_logits, norm_gain, w_out, *, tt=256):
    T, D = x.shape
    dh = D // HGRN_HEADS
    proj = norm_matmul(x, gain, w_in)
    tt = min(tt, S)
    nt = S // tt

    def part(pidx):
        return pl.BlockSpec((tt, dh), lambda b, h, s: (b * nt + s, pidx * HGRN_HEADS + h))

    o = pl.pallas_call(
        functools.partial(_hgrn_body, layer=layer),
        grid=(B, HGRN_HEADS, nt),
        in_specs=[part(0), part(1), part(2), part(3),
                  pl.BlockSpec((DEPTH, dh), lambda b, h, s: (0, h)),
                  pl.BlockSpec((1, dh), lambda b, h, s: (0, 0))],
        out_specs=pl.BlockSpec((tt, dh), lambda b, h, s: (b * nt + s, h)),
        out_shape=jax.ShapeDtypeStruct((T, D), BF16),
        scratch_shapes=[pltpu.VMEM((dh, dh), F32)],
        compiler_params=_cparams(("parallel", "parallel", "arbitrary"), 16 * tt * dh * 4 + 8 * tt * tt * 4),
        name="hgrn_core",
    )(proj, proj, proj, proj, lb_logits, norm_gain.reshape(1, dh))
    return matmul_residual(o, w_out, x)


def _rwkv_shift_mix(x_ref, xp_ref, gn_ref, first):
    gn = gn_ref[...]
    h = _rms(x_ref[...], gn)
    last = _rms(xp_ref[...], gn)[V7X_SUBLANES - 1:V7X_SUBLANES, :]
    last = jnp.where(first, 0.0, last)
    row = lax.broadcasted_iota(jnp.int32, h.shape, 0)
    return h, jnp.where(row == 0, last, pltpu.roll(h, 1, 0)) - h


def _rwkv_rkv_body(x_ref, xp_ref, gn_ref, mu_ref, wrkv_ref, rkv_ref, mix_scr, *, tiles_per_seq, n_col_tiles):
    n = pl.program_id(1)
    first = (pl.program_id(0) % tiles_per_seq) == 0

    @pl.when(n == 0)
    def _():
        h, d = _rwkv_shift_mix(x_ref, xp_ref, gn_ref, first)
        mu = mu_ref[...]
        for m in range(3):
            mix_scr[m] = (h + d * mu[m:m + 1, :]).astype(BF16)

    rkv_ref[...] = _dot(mix_scr[n // n_col_tiles], wrkv_ref[...])


def _rwkv_lora_body(x_ref, xp_ref, gn_ref, mu_ref, w0_ref, w1_ref, w2_ref, a0_ref, a1_ref, a2_ref, g1_ref, g2_ref,
                    lw_ref, a_ref, g_ref, *, tiles_per_seq):
    h, d = _rwkv_shift_mix(x_ref, xp_ref, gn_ref, (pl.program_id(0) % tiles_per_seq) == 0)
    mu = mu_ref[...]
    xw = (h + d * mu[3:4, :]).astype(BF16)
    xa = (h + d * mu[4:5, :]).astype(BF16)
    xg = (h + d * mu[5:6, :]).astype(BF16)
    z = -(w0_ref[...] + _dot(jnp.tanh(_dot(xw, w1_ref[...])).astype(BF16), w2_ref[...]))
    softplus = jnp.maximum(z, 0.0) + jnp.log1p(jnp.exp(-jnp.abs(z)))
    lw_ref[...] = -jnp.exp(-softplus - 0.5)
    a_ref[...] = jax.nn.sigmoid(a0_ref[...] + _dot(_dot(xa, a1_ref[...]).astype(BF16), a2_ref[...]))
    g_ref[...] = _dot(jax.nn.sigmoid(_dot(xg, g1_ref[...])).astype(BF16), g2_ref[...])


def _rwkv_core_body(r_ref, k_ref, v_ref, lw_ref, a_ref, g_ref, kk_ref, ka_ref, rk_ref, lnw_ref, lnb_ref,
                    o_ref, h_scr):
    @pl.when(pl.program_id(2) == 0)
    def _():
        h_scr[...] = jnp.zeros_like(h_scr)

    tt = r_ref.shape[0]
    C, N = RWKV_CHUNK, RWKV_HEAD_DIM
    lw = lw_ref[...]
    g_cum, g_tot, _ = _chunk_sums(lw, C)
    r, k, v, a = r_ref[...], k_ref[...], v_ref[...], a_ref[...]
    left = lax.broadcasted_iota(jnp.int32, (tt, 2 * N), 1) < N

    def head_sum(t):
        return jnp.where(left, jnp.sum(jnp.where(left, t, 0.0), -1, keepdims=True),
                         jnp.sum(jnp.where(left, 0.0, t), -1, keepdims=True))

    kk = k * kk_ref[...]
    kk = kk * lax.rsqrt(jnp.maximum(head_sum(kk * kk), 1e-24))
    kmod = k * (1.0 + (a - 1.0) * ka_ref[...])
    bv = kk * a
    e_neg = jnp.exp(-g_cum)
    e_end = jnp.exp(g_tot - g_cum)
    a_t = (-kk) * jnp.exp(g_cum - lw)
    r_t = r * jnp.exp(g_cum)
    k_t, b_t = kmod * e_neg, bv * e_neg
    k_h, b_h = kmod * e_end, bv * e_end
    dec = jnp.exp(g_tot)
    i64 = lax.broadcasted_iota(jnp.int32, (C, C), 0)
    j64 = lax.broadcasted_iota(jnp.int32, (C, C), 1)
    strict, incl, eye = j64 < i64, j64 <= i64, (i64 == j64).astype(F32)

    nc = tt // C
    units = [(hh, c) for hh in range(2) for c in range(nc)]

    def cut(t, u):
        return t[u[1] * C:(u[1] + 1) * C, u[0] * N:(u[0] + 1) * N]

    a_c = [cut(a_t, u).astype(BF16) for u in units]
    r_c = [cut(r_t, u) for u in units]
    v_c = [cut(v, u).astype(BF16) for u in units]
    bh_c = [cut(b_h, u).astype(BF16) for u in units]
    kh_c = [cut(k_h, u).astype(BF16) for u in units]
    prod = [_dot_nt(jnp.concatenate([a, rr.astype(BF16)], axis=0),
                    jnp.concatenate([cut(b_t, u), cut(k_t, u)], axis=0).astype(BF16))
            for a, rr, u in zip(a_c, r_c, units)]
    n_ab = [jnp.where(strict, p[:C, :C], 0.0) for p in prod]
    n_ak = [jnp.where(strict, p[:C, C:], 0.0).astype(BF16) for p in prod]
    t_rb = [jnp.where(incl, p[C:, :C], 0.0).astype(BF16) for p in prod]
    t_rk = [jnp.where(incl, p[C:, C:], 0.0).astype(BF16) for p in prod]
    n16 = [n.astype(BF16) for n in n_ab]
    pw = [_dot(n, n) for n in n16]
    nakv = [_dot(n, vv).astype(BF16) for n, vv in zip(n_ak, v_c)]
    inv = [eye + n for n in n_ab]
    for step in range(5):
        pw16 = [p.astype(BF16) for p in pw]
        inv16 = [i.astype(BF16) for i in inv]
        if step < 4:
            pw = [_dot(p, p) for p in pw16]
        inv = [i + _dot(i16, p) for i, i16, p in zip(inv, inv16, pw16)]
    inv16 = [i.astype(BF16) for i in inv]
    a_p = [_dot(i, a).astype(BF16) for i, a in zip(inv16, a_c)]
    u0 = [_dot(i, n).astype(BF16) for i, n in zip(inv16, nakv)]
    r_p = [(rr + _dot(t, a)).astype(BF16) for rr, t, a in zip(r_c, t_rb, a_p)]
    y0 = [_dot(tb, u) + _dot(tk, vv) for tb, u, tk, vv in zip(t_rb, u0, t_rk, v_c)]
    trans_t = [_dot_tn(a, b).astype(BF16) for a, b in zip(a_p, bh_c)]
    h_add = [_dot_tn(u, b) + _dot_tn(vv, kk_) for u, b, vv, kk_ in zip(u0, bh_c, v_c, kh_c)]

    ht = [h_scr[hh] for hh in range(2)]
    ys = [[], []]
    for c in range(nc):
        for hh in range(2):
            i = hh * nc + c
            ht16 = ht[hh].astype(BF16)
            ys[hh].append(_dot_nt(r_p[i], ht16) + y0[i])
            ht[hh] = ht[hh] * dec[c * C:c * C + 1, hh * N:(hh + 1) * N] + _dot(ht16, trans_t[i]) + h_add[i]
    y_heads = []
    for hh in range(2):
        h_scr[hh] = ht[hh]
        y = jnp.concatenate(ys[hh], axis=0)
        mean = jnp.mean(y, -1, keepdims=True)
        var = jnp.mean(jnp.square(y - mean), -1, keepdims=True)
        y_heads.append((y - mean) * lax.rsqrt(var + RWKV_GN_EPS))
    yn = jnp.concatenate(y_heads, axis=1) * lnw_ref[...] + lnb_ref[...]
    bonus = head_sum(r * kmod * rk_ref[...]) * v
    o_ref[...] = ((yn + bonus) * g_ref[...]).astype(o_ref.dtype)


def rwkv_mixer(x, B, S, gain, mu, w_rkv, w0, w1, w2, a0, a1, a2, g1, g2, k_k, k_a, r_k, ln_w, ln_b, w_out,
               *, tm=512, tn=1024, tl=256, tt=512):
    T, D = x.shape
    tm, tl, tt = min(tm, S), min(tl, S), min(tt, S)
    nct = D // tn
    row = lambda i, n: (0, 0)

    def prev_rows(t):
        return lambda i, *_: (jnp.maximum(i * (t // V7X_SUBLANES) - 1, 0), 0)

    rkv = pl.pallas_call(
        functools.partial(_rwkv_rkv_body, tiles_per_seq=S // tm, n_col_tiles=nct),
        grid=(T // tm, 3 * nct),
        in_specs=[
            pl.BlockSpec((tm, D), lambda i, n: (i, 0)),
            pl.BlockSpec((V7X_SUBLANES, D), prev_rows(tm)),
            pl.BlockSpec((1, D), row),
            pl.BlockSpec((6, D), row),
            _wspec(w_rkv, (None, D, tn), lambda i, n: (n // nct, 0, n % nct)),
        ],
        out_specs=pl.BlockSpec((tm, tn), lambda i, n: (i, n)),
        out_shape=jax.ShapeDtypeStruct((T, 3 * D), F32),
        scratch_shapes=[pltpu.VMEM((3, tm, D), BF16)],
        compiler_params=_cparams(("parallel", "arbitrary"),
                                 2 * tm * D * 4 + 3 * tm * D * 2 + 2 * D * tn * 2 + 2 * tm * tn * 4 + 3 * tm * D * 4),
        name="rwkv_rkv",
    )(x, x, gain.reshape(1, D), mu, w_rkv[0])

    lora = w1.shape[1]
    pad = (-lora) % V7X_LANES
    w1p, a1p = jnp.pad(w1, ((0, 0), (0, pad))), jnp.pad(a1, ((0, 0), (0, pad)))
    w2p, a2p = jnp.pad(w2, ((0, pad), (0, 0))), jnp.pad(a2, ((0, pad), (0, 0)))
    lp, gl = lora + pad, g1.shape[1]
    one = lambda i: (0, 0)
    tok_l = pl.BlockSpec((tl, D), lambda i: (i, 0))
    lw, a, g = pl.pallas_call(
        functools.partial(_rwkv_lora_body, tiles_per_seq=S // tl),
        grid=(T // tl,),
        in_specs=[
            tok_l,
            pl.BlockSpec((V7X_SUBLANES, D), prev_rows(tl)),
            pl.BlockSpec((1, D), one),
            pl.BlockSpec((6, D), one),
            pl.BlockSpec((1, D), one), pl.BlockSpec((D, lp), one), pl.BlockSpec((lp, D), one),
            pl.BlockSpec((1, D), one), pl.BlockSpec((D, lp), one), pl.BlockSpec((lp, D), one),
            pl.BlockSpec((D, gl), one), pl.BlockSpec((gl, D), one),
        ],
        out_specs=[tok_l, tok_l, tok_l],
        out_shape=[jax.ShapeDtypeStruct((T, D), F32)] * 3,
        compiler_params=_cparams(("parallel",), 14 * tl * D * 4 + 4 * (2 * D * lp + D * gl) * 2),
        name="rwkv_lora",
    )(x, x, gain.reshape(1, D), mu, w0.reshape(1, D), w1p, w2p, a0.reshape(1, D), a1p, a2p, g1, g2)

    nt = S // tt
    pw = 2 * RWKV_HEAD_DIM
    npair = D // pw

    def tok(col0):
        return pl.BlockSpec((tt, pw), lambda b, p, s: (b * nt + s, col0 + p))

    par = pl.BlockSpec((1, pw), lambda b, p, s: (0, p))
    o = pl.pallas_call(
        _rwkv_core_body,
        grid=(B, npair, nt),
        in_specs=[tok(0), tok(npair), tok(2 * npair), tok(0), tok(0), tok(0), par, par, par, par, par],
        out_specs=pl.BlockSpec((tt, pw), lambda b, p, s: (b * nt + s, p)),
        out_shape=jax.ShapeDtypeStruct((T, D), BF16),
        scratch_shapes=[pltpu.VMEM((2, RWKV_HEAD_DIM, RWKV_HEAD_DIM), F32)],
        compiler_params=_cparams(("parallel", "parallel", "arbitrary"), 40 * tt * pw * 4 + 8 * tt * tt * 4),
        name="rwkv_core",
    )(rkv, rkv, rkv, lw, a, g, k_k.reshape(1, D), k_a.reshape(1, D), r_k.reshape(1, D),
      ln_w.reshape(1, D), ln_b.reshape(1, D))
    return matmul_residual(o, w_out, x)


def kernel(x, mem, positions, ffn_norm, ffn_w_gate, ffn_w_up, ffn_w_down, mix_norm, xattn_norm, mem_norm, xattn_wq, xattn_wkv, xattn_wo, xattn_q_gain, xattn_k_gain, conv_w_in, conv_w, conv_w_out, dil_w_qkv, dil_q_gain, dil_k_gain, dil_w_out, hgrn_w_in, hgrn_lb_logits, hgrn_norm, hgrn_w_out, rwkv_mu, rwkv_w_rkv, rwkv_w0, rwkv_w1, rwkv_w2, rwkv_a0, rwkv_a1, rwkv_a2, rwkv_g1, rwkv_g2, rwkv_k_k, rwkv_k_a, rwkv_r_k, rwkv_ln_w, rwkv_ln_b, rwkv_w_out):
    B, S, D = x.shape
    assert D == D_MODEL and S % (DIL_BLOCK * DIL_PATTERNS[-1][1]) == 0
    depth = ffn_norm.shape[0]
    xf = x.reshape(B * S, D)
    memf = mem.reshape(B * MEM_LEN, D)
    bf = lambda w: w.astype(BF16)
    wg_all, wu_all, wd_all = bf(ffn_w_gate), bf(ffn_w_up), bf(ffn_w_down)
    wq_all, wkv_all, wo_all = bf(xattn_wq), bf(xattn_wkv), bf(xattn_wo)
    conv_in_all, conv_out_all = bf(conv_w_in), bf(conv_w_out)
    dil_qkv_all, dil_out_all = bf(dil_w_qkv), bf(dil_w_out)
    hgrn_in_all, hgrn_out_all = bf(hgrn_w_in), bf(hgrn_w_out)
    rkv_all, rwkv_out_all = bf(rwkv_w_rkv), bf(rwkv_w_out)
    for i in range(depth):
        kind, j = i % N_MIXERS, i // N_MIXERS
        xf = ffn_half(xf, ffn_norm[i, 0], (wg_all, (i, 0)), (wu_all, (i, 0)), (wd_all, (i, 0)))
        if kind == 0:
            xf = conv_mixer(xf, S, mix_norm[i], (conv_in_all, (j,)), conv_w[j], (conv_out_all, (j,)))
        elif kind == 1:
            xf = dilated_mixer(xf, B, S, positions, mix_norm[i], (dil_qkv_all, (j,)), dil_q_gain[j], dil_k_gain[j],
                               (dil_out_all, (j,)))
        elif kind == 2:
            xf = hgrn_mixer(xf, B, S, i, mix_norm[i], (hgrn_in_all, (j,)), hgrn_lb_logits, hgrn_norm[j],
                            (hgrn_out_all, (j,)))
        else:
            xf = rwkv_mixer(xf, B, S, mix_norm[i], rwkv_mu[j], (rkv_all, (j,)), rwkv_w0[j], bf(rwkv_w1[j]),
                            bf(rwkv_w2[j]), rwkv_a0[j], bf(rwkv_a1[j]), bf(rwkv_a2[j]), bf(rwkv_g1[j]),
                            bf(rwkv_g2[j]), rwkv_k_k[j], rwkv_k_a[j], rwkv_r_k[j], rwkv_ln_w[j], rwkv_ln_b[j],
                            (rwkv_out_all, (j,)))
        xf = cross_attention(xf, S, memf, xattn_norm[i], mem_norm[i], (wq_all, (i,)), (wkv_all, (i,)),
                             (wo_all, (i,)), xattn_q_gain[i], xattn_k_gain[i])
        xf = ffn_half(xf, ffn_norm[i, 1], (wg_all, (i, 1)), (wu_all, (i, 1)), (wd_all, (i, 1)))
    return xf.reshape(B, S, D)
```

```python
import functools

import jax
import jax.numpy as jnp
from jax import lax
from jax.experimental import pallas as pl
from jax.experimental.pallas import tpu as pltpu

F32 = jnp.float32
BF16 = jnp.bfloat16

D_MODEL = 2048
DEPTH = 4
N_MIXERS = 4
MEM_LEN = 256
FFN_DIM = 5632
NORM_EPS = 1e-6
NEG_INF = -1e30
ROPE_THETA = 500000.0
DIL_PATTERNS = ((128, 1), (512, 4), (2048, 16))
DIL_GROUPS = 3
DIL_HEADS = 8
DIL_HEAD_DIM = 128
DIL_BLOCK = 128
DIL_UNITS_IN_FLIGHT = 8
HGRN_CHUNK = 16
HGRN_HEADS = 16
RWKV_HEAD_DIM = 64
RWKV_HEADS = 32
RWKV_CHUNK = 64
RWKV_GN_EPS = 64e-5
XATTN_HEADS = 4
XATTN_HEAD_DIM = 512

V7X_LANES = 128
V7X_SUBLANES = 8
V7X_VMEM_BYTES = 64 * 2**20
V7X_VMEM_CAP = 56 * 2**20


def _cparams(sem, vmem_bytes):
    limit = min(int(vmem_bytes * 1.25) + (4 << 20), V7X_VMEM_CAP)
    return pltpu.CompilerParams(dimension_semantics=sem, vmem_limit_bytes=limit)


def _rms(x, gain):
    return x * lax.rsqrt(jnp.mean(x * x, axis=-1, keepdims=True) + NORM_EPS) * gain


def _dot(a, b):
    return jnp.dot(a, b, preferred_element_type=F32)


def _dot_nt(a, b):
    return lax.dot_general(a, b, (((1,), (1,)), ((), ())), preferred_element_type=F32)


def _dot_tn(a, b):
    return lax.dot_general(a, b, (((0,), (0,)), ((), ())), preferred_element_type=F32)


def _wshape(w):
    arr, lead = w
    return arr.shape[len(lead):]


def _wspec(w, block, tail):
    lead = tuple(w[1])
    return pl.BlockSpec((None,) * len(lead) + tuple(block), lambda *g: lead + tuple(tail(*g)))


def _chunk_sums(x, chunk):
    t, w = x.shape
    rows = lax.broadcasted_iota(jnp.int32, (t, t), 0)
    cols = lax.broadcasted_iota(jnp.int32, (t, t), 1)
    same = _chunk_of(rows, chunk) == _chunk_of(cols, chunk)
    tri = same & (cols <= rows)
    sel = jnp.concatenate([tri.astype(BF16), same.astype(BF16)], axis=0)
    hi = x.astype(BF16)
    r1 = x - hi.astype(F32)
    mid = r1.astype(BF16)
    lo = (r1 - mid.astype(F32)).astype(BF16)
    s = _dot(sel, jnp.concatenate([hi, mid, lo], axis=1))
    s = s[:, :w] + s[:, w:2 * w] + s[:, 2 * w:]
    return s[:t], s[t:], tri


def _chunk_of(idx, chunk):
    return jnp.right_shift(idx, chunk.bit_length() - 1)


def _silu(x):
    return x * jax.nn.sigmoid(x)


def _norm_matmul_body(x_ref, g_ref, w_ref, o_ref, h_scr):
    @pl.when(pl.program_id(1) == 0)
    def _():
        h_scr[...] = _rms(x_ref[...], g_ref[...]).astype(BF16)

    o_ref[...] = _dot(h_scr[...], w_ref[...]).astype(o_ref.dtype)


def norm_matmul(x, gain, w, *, tm=1024, tn=1024, out_dtype=F32):
    M, K = x.shape
    N = _wshape(w)[1]
    tm, tn = min(tm, M), min(tn, N)
    ob = jnp.dtype(out_dtype).itemsize
    vmem = 2 * tm * K * 4 + tm * K * 2 + 2 * K * tn * 2 + 2 * tm * tn * ob
    return pl.pallas_call(
        _norm_matmul_body,
        grid=(M // tm, N // tn),
        in_specs=[
            pl.BlockSpec((tm, K), lambda i, j: (i, 0)),
            pl.BlockSpec((1, K), lambda i, j: (0, 0)),
            _wspec(w, (K, tn), lambda i, j: (0, j)),
        ],
        out_specs=pl.BlockSpec((tm, tn), lambda i, j: (i, j)),
        out_shape=jax.ShapeDtypeStruct((M, N), out_dtype),
        scratch_shapes=[pltpu.VMEM((tm, K), BF16)],
        compiler_params=_cparams(("parallel", "arbitrary"), vmem),
        name="norm_matmul",
    )(x, gain.reshape(1, K), w[0])


def _ffn_body(x_ref, g_ref, wg_ref, wu_ref, wd_ref, o_ref, h_scr):
    @pl.when(pl.program_id(1) == 0)
    def _():
        x = x_ref[...]
        h_scr[...] = _rms(x, g_ref[...]).astype(BF16)
        o_ref[...] = x

    h = h_scr[...]
    act = _silu(_dot(h, wg_ref[...])) * _dot(h, wu_ref[...])
    o_ref[...] += 0.5 * _dot(act.astype(BF16), wd_ref[...])


def ffn_half(x, gain, wg, wu, wd, *, tm=1024, tf=512):
    M, D = x.shape
    F = _wshape(wg)[1]
    tm = min(tm, M)
    vmem = 3 * tm * D * 4 + tm * D * 2 + 2 * 3 * D * tf * 2 + 3 * tm * tf * 4
    return pl.pallas_call(
        _ffn_body,
        grid=(M // tm, F // tf),
        in_specs=[
            pl.BlockSpec((tm, D), lambda i, f: (i, 0), pipeline_mode=pl.Buffered(1)),
            pl.BlockSpec((1, D), lambda i, f: (0, 0)),
            _wspec(wg, (D, tf), lambda i, f: (0, f)),
            _wspec(wu, (D, tf), lambda i, f: (0, f)),
            _wspec(wd, (tf, D), lambda i, f: (f, 0)),
        ],
        out_specs=pl.BlockSpec((tm, D), lambda i, f: (i, 0)),
        out_shape=jax.ShapeDtypeStruct((M, D), F32),
        scratch_shapes=[pltpu.VMEM((tm, D), BF16)],
        compiler_params=_cparams(("parallel", "arbitrary"), vmem),
        name="ffn_half",
    )(x, gain.reshape(1, D), wg[0], wu[0], wd[0])


def _matmul_res_body(a_ref, w_ref, r_ref, o_ref):
    o_ref[...] = r_ref[...] + _dot(a_ref[...], w_ref[...])


def matmul_residual(a, w, res, *, tm=1024, tn=1024):
    M, K = a.shape
    N = _wshape(w)[1]
    tm = min(tm, M)
    vmem = 2 * tm * K * 2 + 2 * K * tn * 2 + 4 * tm * tn * 4
    return pl.pallas_call(
        _matmul_res_body,
        grid=(M // tm, N // tn),
        in_specs=[
            pl.BlockSpec((tm, K), lambda i, j: (i, 0)),
            _wspec(w, (K, tn), lambda i, j: (0, j)),
            pl.BlockSpec((tm, tn), lambda i, j: (i, j)),
        ],
        out_specs=pl.BlockSpec((tm, tn), lambda i, j: (i, j)),
        out_shape=jax.ShapeDtypeStruct((M, N), F32),
        compiler_params=_cparams(("parallel", "arbitrary"), vmem),
        name="matmul_residual",
    )(a, w[0], res)


def _prologue_matmul_res_body(prologue, n_in, *refs):
    in_refs = refs[:n_in]
    w_ref, r_ref, o_ref, lhs_scr = refs[n_in:]
    row_tile = pl.program_id(0)

    @pl.when(pl.program_id(1) == 0)
    def _():
        prologue(row_tile, *in_refs, lhs_scr)

    o_ref[...] = r_ref[...] + _dot(lhs_scr[...], w_ref[...])


def prologue_matmul_residual(prologue, inputs, in_specs, w, res, *, tm, tn, in_vmem, name):
    M, N = res.shape
    K = _wshape(w)[0]
    vmem = in_vmem + tm * K * 2 + 2 * K * tn * 2 + 4 * tm * tn * 4
    return pl.pallas_call(
        functools.partial(_prologue_matmul_res_body, prologue, len(inputs)),
        grid=(M // tm, N // tn),
        in_specs=list(in_specs) + [
            _wspec(w, (K, tn), lambda i, j: (0, j)),
            pl.BlockSpec((tm, tn), lambda i, j: (i, j)),
        ],
        out_specs=pl.BlockSpec((tm, tn), lambda i, j: (i, j)),
        out_shape=jax.ShapeDtypeStruct((M, N), F32),
        scratch_shapes=[pltpu.VMEM((tm, K), BF16)],
        compiler_params=_cparams(("parallel", "arbitrary"), vmem),
        name=name,
    )(*inputs, w[0], res)


def _conv_prologue(tiles_per_seq, row_tile, b_ref, c_ref, u_ref, cp_ref, up_ref, cw_ref, lhs_scr):
    cu = c_ref[...] * u_ref[...]
    prev = cp_ref[...] * up_ref[...]
    first = (row_tile % tiles_per_seq) == 0
    prev = jnp.where(first, 0.0, prev)
    p1, p2 = prev[7:8, :], prev[6:7, :]
    row = lax.broadcasted_iota(jnp.int32, cu.shape, 0)
    s1 = jnp.where(row == 0, p1, pltpu.roll(cu, 1, 0))
    s2 = jnp.where(row == 0, p2, jnp.where(row == 1, p1, pltpu.roll(cu, 2, 0)))
    w = cw_ref[...]
    y = w[0:1, :] * s2 + w[1:2, :] * s1 + w[2:3, :] * cu
    lhs_scr[...] = (b_ref[...] * y).astype(BF16)


def conv_mixer(x, S, gain, w_in, conv_w, w_out, *, tm=512):
    T, D = x.shape
    tm = min(tm, S)
    bcu = norm_matmul(x, gain, w_in)
    r8 = tm // V7X_SUBLANES

    def prev_map(col):
        return lambda i, j: (jnp.maximum(i * r8 - 1, 0), col)

    in_specs = [
        pl.BlockSpec((tm, D), lambda i, j: (i, 0)),
        pl.BlockSpec((tm, D), lambda i, j: (i, 1)),
        pl.BlockSpec((tm, D), lambda i, j: (i, 2)),
        pl.BlockSpec((V7X_SUBLANES, D), prev_map(1)),
        pl.BlockSpec((V7X_SUBLANES, D), prev_map(2)),
        pl.BlockSpec((3, D), lambda i, j: (0, 0)),
    ]
    return prologue_matmul_residual(
        functools.partial(_conv_prologue, S // tm), (bcu, bcu, bcu, bcu, bcu, conv_w), in_specs, w_out, x,
        tm=tm, tn=1024, in_vmem=2 * 3 * tm * D * 4 + 4 * V7X_SUBLANES * D * 4, name="conv_mixer_out")


def _xattn_body(x_ref, kv_ref, xg_ref, qg_ref, kg_ref, wq_ref, wo_ref, o_ref, attn_scr):
    scale = XATTN_HEAD_DIM ** -0.5
    D = XATTN_HEADS * XATTN_HEAD_DIM
    x = x_ref[...]
    q = _dot(_rms(x, xg_ref[...]).astype(BF16), wq_ref[...])
    for h in range(XATTN_HEADS):
        sl = slice(h * XATTN_HEAD_DIM, (h + 1) * XATTN_HEAD_DIM)
        qn = _rms(q[:, sl], qg_ref[...]).astype(BF16)
        kn = _rms(kv_ref[:, sl], kg_ref[...]).astype(BF16)
        v = kv_ref[:, D + h * XATTN_HEAD_DIM:D + (h + 1) * XATTN_HEAD_DIM].astype(BF16)
        s = _dot_nt(qn, kn) * scale
        p = jnp.exp(s - jnp.max(s, axis=-1, keepdims=True))
        l = jnp.sum(p, axis=-1, keepdims=True)
        attn_scr[:, sl] = (_dot(p.astype(BF16), v) / l).astype(BF16)
    o_ref[...] = x + _dot(attn_scr[...], wo_ref[...])


def cross_attention(x, S, mem, xgain, mgain, wq, wkv, wo, q_gain, k_gain, *, tm=512):
    T, D = x.shape
    tm = min(tm, S)
    kv = norm_matmul(mem, mgain, wkv)
    tps = S // tm
    once = pl.Buffered(1)
    vmem = 4 * tm * D * 4 + 2 * MEM_LEN * 2 * D * 4 + 2 * D * D * 2 + tm * D * 2 + 3 * tm * D * 4
    return pl.pallas_call(
        _xattn_body,
        grid=(T // tm,),
        in_specs=[
            pl.BlockSpec((tm, D), lambda i: (i, 0)),
            pl.BlockSpec((MEM_LEN, 2 * D), lambda i: (i // tps, 0)),
            pl.BlockSpec((1, D), lambda i: (0, 0)),
            pl.BlockSpec((1, XATTN_HEAD_DIM), lambda i: (0, 0)),
            pl.BlockSpec((1, XATTN_HEAD_DIM), lambda i: (0, 0)),
            pl.BlockSpec((None,) * len(wq[1]) + (D, D), lambda i: tuple(wq[1]) + (0, 0), pipeline_mode=once),
            pl.BlockSpec((None,) * len(wo[1]) + (D, D), lambda i: tuple(wo[1]) + (0, 0), pipeline_mode=once),
        ],
        out_specs=pl.BlockSpec((tm, D), lambda i: (i, 0)),
        out_shape=jax.ShapeDtypeStruct((T, D), F32),
        scratch_shapes=[pltpu.VMEM((tm, D), BF16)],
        compiler_params=_cparams(("parallel",), vmem),
        name="xattn",
    )(x, kv, xgain.reshape(1, D), q_gain.reshape(1, -1), k_gain.reshape(1, -1), wq[0], wo[0])


def _dil_prep_body(x_ref, pos_ref, invf_ref, qg_ref, kg_ref, o_ref):
    ang = pos_ref[...] * invf_ref[...]
    lane = lax.broadcasted_iota(jnp.int32, ang.shape, 1)
    half = DIL_HEAD_DIM // 8
    cos, sin = jnp.cos(ang), jnp.sin(ang)
    sin_lo = jnp.where(lane < half, -sin, 0.0)
    sin_hi = jnp.where((lane >= half) & (lane < 2 * half), sin, 0.0)
    r = lax.broadcasted_iota(jnp.int32, (2 * DIL_HEAD_DIM, 2 * DIL_HEAD_DIM), 0)
    c = lax.broadcasted_iota(jnp.int32, (2 * DIL_HEAD_DIM, 2 * DIL_HEAD_DIM), 1)
    mean_mat = jnp.where((r < DIL_HEAD_DIM) == (c < DIL_HEAD_DIM), 1.0 / DIL_HEAD_DIM, 0.0).astype(BF16)
    for part, g_ref in ((0, qg_ref), (1, kg_ref)):
        for g in range(DIL_GROUPS):
            gain = g_ref[g:g + 1, :]
            for pair in range(DIL_HEADS // 2):
                col = ((part * DIL_GROUPS + g) * DIL_HEADS + 2 * pair) * DIL_HEAD_DIM
                x2 = x_ref[:, col:col + 2 * DIL_HEAD_DIM]
                inv = lax.rsqrt(_dot((x2 * x2).astype(BF16), mean_mat) + NORM_EPS)
                for hh in range(2):
                    sl = slice(hh * DIL_HEAD_DIM, (hh + 1) * DIL_HEAD_DIM)
                    xn = x2[:, sl] * inv[:, sl] * gain
                    o_ref[:, col + sl.start:col + sl.stop] = (
                        xn * cos + pltpu.roll(xn, DIL_HEAD_DIM - half, 1) * sin_lo
                        + pltpu.roll(xn, half, 1) * sin_hi)


def _dil_attn_body(*refs, n_chunks):
    ins, o_ref, scr = refs[:15], refs[15], refs[16:]
    c = pl.program_id(1)
    scale = DIL_HEAD_DIM ** -0.5
    ii = lax.broadcasted_iota(jnp.int32, (DIL_BLOCK, DIL_BLOCK), 0)
    jj = lax.broadcasted_iota(jnp.int32, (DIL_BLOCK, DIL_BLOCK), 1)
    cur_mask = jj <= ii
    prev_mask = jj >= ii
    ch = o_ref.shape[0]
    for g, (_, dil) in enumerate(DIL_PATTERNS):
        q_ref, k_ref, v_ref, kh_ref, vh_ref = ins[5 * g:5 * g + 5]
        kf, vf, og, lg = scr[4 * g:4 * g + 4]
        hist = DIL_BLOCK * dil
        kf[0:hist, :] = kh_ref[...]
        vf[0:hist, :] = vh_ref[...]
        kf[hist:hist + ch, :] = k_ref[...]
        vf[hist:hist + ch, :] = v_ref[...]
        def rows(start):
            return pl.ds(start, DIL_BLOCK, stride=dil) if dil > 1 else pl.ds(start, DIL_BLOCK)

        units = [(blk, blk * hist + r) for blk in range(ch // hist) for r in range(dil)]
        for b0 in range(0, len(units), DIL_UNITS_IN_FLIGHT):
            batch = units[b0:b0 + DIL_UNITS_IN_FLIGHT]
            qv = [q_ref[rows(q0), :].astype(BF16) for _, q0 in batch]
            sc = [_dot_nt(q, kf[rows(hist + q0), :].astype(BF16)) for q, (_, q0) in zip(qv, batch)]
            sp = [_dot_nt(q, kf[rows(q0), :].astype(BF16)) for q, (_, q0) in zip(qv, batch)]
            sc = [jnp.where(cur_mask, s * scale, NEG_INF) for s in sc]
            sp = [jnp.where(prev_mask if blk > 0 else prev_mask & (c > 0), s * scale, NEG_INF)
                  for s, (blk, _) in zip(sp, batch)]
            m = [jnp.maximum(jnp.max(a, -1, keepdims=True), jnp.max(b, -1, keepdims=True)) for a, b in zip(sc, sp)]
            pc = [jnp.exp(a - mm) for a, mm in zip(sc, m)]
            pp = [jnp.exp(b - mm) for b, mm in zip(sp, m)]
            l = [jnp.sum(a, -1, keepdims=True) + jnp.sum(b, -1, keepdims=True) for a, b in zip(pc, pp)]
            o = [_dot(a.astype(BF16), vf[rows(hist + q0), :].astype(BF16))
                 + _dot(b.astype(BF16), vf[rows(q0), :].astype(BF16)) for a, b, (_, q0) in zip(pc, pp, batch)]
            for oo, ll, mm, (_, q0) in zip(o, l, m, batch):
                og[rows(q0), :] = oo / ll
                lg[rows(q0), :] = jnp.broadcast_to(mm + jnp.log(ll), (DIL_BLOCK, DIL_HEAD_DIM))
    l0, l1, l2 = scr[3][...], scr[7][...], scr[11][...]
    mx = jnp.maximum(jnp.maximum(l0, l1), l2)
    e0, e1, e2 = jnp.exp(l0 - mx), jnp.exp(l1 - mx), jnp.exp(l2 - mx)
    o_ref[...] = ((e0 * scr[2][...] + e1 * scr[6][...] + e2 * scr[10][...]) / (e0 + e1 + e2)).astype(o_ref.dtype)


def dilated_mixer(x, B, S, positions, gain, w_qkv, q_gain, k_gain, w_out):
    T, D = x.shape
    nh = DIL_GROUPS * DIL_HEADS
    rot = DIL_HEAD_DIM // 4
    inv_freq = ROPE_THETA ** (-jnp.arange(0, rot, 2, dtype=F32) / rot)
    invf = jnp.concatenate([inv_freq, inv_freq, jnp.zeros((DIL_HEAD_DIM - rot,), F32)]).reshape(1, DIL_HEAD_DIM)
    pos = positions.astype(F32).reshape(T, 1)
    qkv = norm_matmul(x, gain, w_qkv)
    tp = 256
    qk_cols = 2 * nh * DIL_HEAD_DIM
    qk = pl.pallas_call(
        _dil_prep_body,
        grid=(T // tp,),
        in_specs=[
            pl.BlockSpec((tp, qk_cols), lambda i: (i, 0)),
            pl.BlockSpec((tp, 1), lambda i: (i, 0)),
            pl.BlockSpec((1, DIL_HEAD_DIM), lambda i: (0, 0)),
            pl.BlockSpec((DIL_GROUPS, DIL_HEAD_DIM), lambda i: (0, 0)),
            pl.BlockSpec((DIL_GROUPS, DIL_HEAD_DIM), lambda i: (0, 0)),
        ],
        out_specs=pl.BlockSpec((tp, qk_cols), lambda i: (i, 0)),
        out_shape=jax.ShapeDtypeStruct((T, qk_cols), F32),
        compiler_params=_cparams(("parallel",), 4 * tp * qk_cols * 4 + 2 * tp * V7X_LANES * 4),
        name="dil_qk_prep",
    )(qkv, pos, invf, q_gain, k_gain)

    ch = DIL_BLOCK * DIL_PATTERNS[-1][1]
    n_chunks = S // ch
    inputs, in_specs, scratch = [], [], []
    vmem = 2 * ch * DIL_HEAD_DIM * 2
    for g, (_, dil) in enumerate(DIL_PATTERNS):
        hist = DIL_BLOCK * dil
        per = ch // hist

        def cur_map(col):
            return lambda b, c, h: (b * n_chunks + c, col + h)

        def hist_map(col, per=per):
            return lambda b, c, h: (jnp.maximum((b * n_chunks + c) * per - 1, 0), col + h)

        inputs += [qk, qk, qkv, qk, qkv]
        in_specs += [
            pl.BlockSpec((ch, DIL_HEAD_DIM), cur_map(g * DIL_HEADS)),
            pl.BlockSpec((ch, DIL_HEAD_DIM), cur_map(nh + g * DIL_HEADS)),
            pl.BlockSpec((ch, DIL_HEAD_DIM), cur_map(2 * nh + g * DIL_HEADS)),
            pl.BlockSpec((hist, DIL_HEAD_DIM), hist_map(nh + g * DIL_HEADS)),
            pl.BlockSpec((hist, DIL_HEAD_DIM), hist_map(2 * nh + g * DIL_HEADS)),
        ]
        scratch += [pltpu.VMEM((hist + ch, DIL_HEAD_DIM), F32), pltpu.VMEM((hist + ch, DIL_HEAD_DIM), F32),
                    pltpu.VMEM((ch, DIL_HEAD_DIM), F32), pltpu.VMEM((ch, DIL_HEAD_DIM), F32)]
        vmem += (2 * (3 * ch + 2 * hist) + 2 * (hist + ch) + 2 * ch) * DIL_HEAD_DIM * 4
    o = pl.pallas_call(
        functools.partial(_dil_attn_body, n_chunks=n_chunks),
        grid=(B, n_chunks, DIL_HEADS),
        in_specs=in_specs,
        out_specs=pl.BlockSpec((ch, DIL_HEAD_DIM), lambda b, c, h: (b * n_chunks + c, h)),
        out_shape=jax.ShapeDtypeStruct((T, DIL_HEADS * DIL_HEAD_DIM), BF16),
        scratch_shapes=scratch,
        compiler_params=_cparams(("parallel", "arbitrary", "arbitrary"), vmem),
        name="dil_attention",
    )(*inputs)
    return matmul_residual(o, w_out, x)


def _hgrn_body(q_ref, f_ref, i_ref, gt_ref, lbl_ref, gain_ref, o_ref, st_scr, *, layer):
    @pl.when(pl.program_id(2) == 0)
    def _():
        st_scr[...] = jnp.zeros_like(st_scr)

    tt = q_ref.shape[0]
    lbl = lbl_ref[...]
    e = jnp.exp(lbl - jnp.max(lbl, axis=0, keepdims=True))
    p = e / jnp.sum(e, axis=0, keepdims=True)
    lb = jnp.sum(p[1:layer + 1, :], axis=0, keepdims=True)
    forget = lb + (1.0 - lb) * jax.nn.sigmoid(f_ref[...])
    k = 1.0 - forget
    gl = jnp.log(forget)
    a_cum, a_tot, tri = _chunk_sums(gl, HGRN_CHUNK)
    q_dec = (q_ref[...] * jnp.exp(a_cum)).astype(BF16)
    k_in = (k * jnp.exp(-a_cum)).astype(BF16)
    k_end = (k * jnp.exp(a_tot - a_cum)).astype(BF16)
    v = i_ref[...].astype(BF16)
    att = jnp.where(tri, _dot_nt(q_dec, k_in), 0.0)
    o = _dot(att.astype(BF16), v)
    dec = jnp.exp(a_tot)
    st = st_scr[...]
    chunks = [slice(c * HGRN_CHUNK, (c + 1) * HGRN_CHUNK) for c in range(tt // HGRN_CHUNK)]
    upd = [_dot_tn(v[sl], k_end[sl]) for sl in chunks]
    inter = []
    for sl, u in zip(chunks, upd):
        inter.append(_dot_nt(q_dec[sl], st.astype(BF16)))
        st = st * dec[sl.start:sl.start + 1, :] + u
    st_scr[...] = st
    o = o + jnp.concatenate(inter, axis=0)
    o_ref[...] = (_rms(o, gain_ref[...]) * _silu(gt_ref[...])).astype(o_ref.dtype)


def hgrn_mixer(x, B, S, layer, gain, w_in, lb_logits, norm_gain, w_out, *, tt=256):
    T, D = x.shape
    dh = D // HGRN_HEADS
    proj = norm_matmul(x, gain, w_in)
    tt = min(tt, S)
    nt = S // tt

    def part(pidx):
        return pl.BlockSpec((tt, dh), lambda b, h, s: (b * nt + s, pidx * HGRN_HEADS + h))

    o = pl.pallas_call(
        functools.partial(_hgrn_body, layer=layer),
        grid=(B, HGRN_HEADS, nt),
        in_specs=[part(0), part(1), part(2), part(3),
                  pl.BlockSpec((DEPTH, dh), lambda b, h, s: (0, h)),
                  pl.BlockSpec((1, dh), lambda b, h, s: (0, 0))],
        out_specs=pl.BlockSpec((tt, dh), lambda b, h, s: (b * nt + s, h)),
        out_shape=jax.ShapeDtypeStruct((T, D), BF16),
        scratch_shapes=[pltpu.VMEM((dh, dh), F32)],
        compiler_params=_cparams(("parallel", "parallel", "arbitrary"), 16 * tt * dh * 4 + 8 * tt * tt * 4),
        name="hgrn_core",
    )(proj, proj, proj, proj, lb_logits, norm_gain.reshape(1, dh))
    return matmul_residual(o, w_out, x)


def _rwkv_shift_mix(x_ref, xp_ref, gn_ref, first):
    gn = gn_ref[...]
    h = _rms(x_ref[...], gn)
    last = _rms(xp_ref[...], gn)[V7X_SUBLANES - 1:V7X_SUBLANES, :]
    last = jnp.where(first, 0.0, last)
    row = lax.broadcasted_iota(jnp.int32, h.shape, 0)
    return h, jnp.where(row == 0, last, pltpu.roll(h, 1, 0)) - h


def _rwkv_rkv_body(x_ref, xp_ref, gn_ref, mu_ref, wrkv_ref, rkv_ref, mix_scr, *, tiles_per_seq, n_col_tiles):
    n = pl.program_id(1)
    first = (pl.program_id(0) % tiles_per_seq) == 0

    @pl.when(n == 0)
    def _():
        h, d = _rwkv_shift_mix(x_ref, xp_ref, gn_ref, first)
        mu = mu_ref[...]
        for m in range(3):
            mix_scr[m] = (h + d * mu[m:m + 1, :]).astype(BF16)

    rkv_ref[...] = _dot(mix_scr[n // n_col_tiles], wrkv_ref[...])


def _rwkv_lora_body(x_ref, xp_ref, gn_ref, mu_ref, w0_ref, w1_ref, w2_ref, a0_ref, a1_ref, a2_ref, g1_ref, g2_ref,
                    lw_ref, a_ref, g_ref, *, tiles_per_seq):
    h, d = _rwkv_shift_mix(x_ref, xp_ref, gn_ref, (pl.program_id(0) % tiles_per_seq) == 0)
    mu = mu_ref[...]
    xw = (h + d * mu[3:4, :]).astype(BF16)
    xa = (h + d * mu[4:5, :]).astype(BF16)
    xg = (h + d * mu[5:6, :]).astype(BF16)
    z = -(w0_ref[...] + _dot(jnp.tanh(_dot(xw, w1_ref[...])).astype(BF16), w2_ref[...]))
    softplus = jnp.maximum(z, 0.0) + jnp.log1p(jnp.exp(-jnp.abs(z)))
    lw_ref[...] = -jnp.exp(-softplus - 0.5)
    a_ref[...] = jax.nn.sigmoid(a0_ref[...] + _dot(_dot(xa, a1_ref[...]).astype(BF16), a2_ref[...]))
    g_ref[...] = _dot(jax.nn.sigmoid(_dot(xg, g1_ref[...])).astype(BF16), g2_ref[...])


def _rwkv_core_body(r_ref, k_ref, v_ref, lw_ref, a_ref, g_ref, kk_ref, ka_ref, rk_ref, lnw_ref, lnb_ref,
                    o_ref, h_scr):
    @pl.when(pl.program_id(2) == 0)
    def _():
        h_scr[...] = jnp.zeros_like(h_scr)

    tt = r_ref.shape[0]
    C, N = RWKV_CHUNK, RWKV_HEAD_DIM
    lw = lw_ref[...]
    g_cum, g_tot, _ = _chunk_sums(lw, C)
    r, k, v, a = r_ref[...], k_ref[...], v_ref[...], a_ref[...]
    left = lax.broadcasted_iota(jnp.int32, (tt, 2 * N), 1) < N

    def head_sum(t):
        return jnp.where(left, jnp.sum(jnp.where(left, t, 0.0), -1, keepdims=True),
                         jnp.sum(jnp.where(left, 0.0, t), -1, keepdims=True))

    kk = k * kk_ref[...]
    kk = kk * lax.rsqrt(jnp.maximum(head_sum(kk * kk), 1e-24))
    kmod = k * (1.0 + (a - 1.0) * ka_ref[...])
    bv = kk * a
    e_neg = jnp.exp(-g_cum)
    e_end = jnp.exp(g_tot - g_cum)
    a_t = (-kk) * jnp.exp(g_cum - lw)
    r_t = r * jnp.exp(g_cum)
    k_t, b_t = kmod * e_neg, bv * e_neg
    k_h, b_h = kmod * e_end, bv * e_end
    dec = jnp.exp(g_tot)
    i64 = lax.broadcasted_iota(jnp.int32, (C, C), 0)
    j64 = lax.broadcasted_iota(jnp.int32, (C, C), 1)
    strict, incl, eye = j64 < i64, j64 <= i64, (i64 == j64).astype(F32)

    nc = tt // C
    units = [(hh, c) for hh in range(2) for c in range(nc)]

    def cut(t, u):
        return t[u[1] * C:(u[1] + 1) * C, u[0] * N:(u[0] + 1) * N]

    a_c = [cut(a_t, u).astype(BF16) for u in units]
    r_c = [cut(r_t, u) for u in units]
    v_c = [cut(v, u).astype(BF16) for u in units]
    bh_c = [cut(b_h, u).astype(BF16) for u in units]
    kh_c = [cut(k_h, u).astype(BF16) for u in units]
    prod = [_dot_nt(jnp.concatenate([a, rr.astype(BF16)], axis=0),
                    jnp.concatenate([cut(b_t, u), cut(k_t, u)], axis=0).astype(BF16))
            for a, rr, u in zip(a_c, r_c, units)]
    n_ab = [jnp.where(strict, p[:C, :C], 0.0) for p in prod]
    n_ak = [jnp.where(strict, p[:C, C:], 0.0).astype(BF16) for p in prod]
    t_rb = [jnp.where(incl, p[C:, :C], 0.0).astype(BF16) for p in prod]
    t_rk = [jnp.where(incl, p[C:, C:], 0.0).astype(BF16) for p in prod]
    n16 = [n.astype(BF16) for n in n_ab]
    pw = [_dot(n, n) for n in n16]
    nakv = [_dot(n, vv).astype(BF16) for n, vv in zip(n_ak, v_c)]
    inv = [eye + n for n in n_ab]
    for step in range(5):
        pw16 = [p.astype(BF16) for p in pw]
        inv16 = [i.astype(BF16) for i in inv]
        if step < 4:
            pw = [_dot(p, p) for p in pw16]
        inv = [i + _dot(i16, p) for i, i16, p in zip(inv, inv16, pw16)]
    inv16 = [i.astype(BF16) for i in inv]
    a_p = [_dot(i, a).astype(BF16) for i, a in zip(inv16, a_c)]
    u0 = [_dot(i, n).astype(BF16) for i, n in zip(inv16, nakv)]
    r_p = [(rr + _dot(t, a)).astype(BF16) for rr, t, a in zip(r_c, t_rb, a_p)]
    y0 = [_dot(tb, u) + _dot(tk, vv) for tb, u, tk, vv in zip(t_rb, u0, t_rk, v_c)]
    trans_t = [_dot_tn(a, b).astype(BF16) for a, b in zip(a_p, bh_c)]
    h_add = [_dot_tn(u, b) + _dot_tn(vv, kk_) for u, b, vv, kk_ in zip(u0, bh_c, v_c, kh_c)]

    ht = [h_scr[hh] for hh in range(2)]
    ys = [[], []]
    for c in range(nc):
        for hh in range(2):
            i = hh * nc + c
            ht16 = ht[hh].astype(BF16)
            ys[hh].append(_dot_nt(r_p[i], ht16) + y0[i])
            ht[hh] = ht[hh] * dec[c * C:c * C + 1, hh * N:(hh + 1) * N] + _dot(ht16, trans_t[i]) + h_add[i]
    y_heads = []
    for hh in range(2):
        h_scr[hh] = ht[hh]
        y = jnp.concatenate(ys[hh], axis=0)
        mean = jnp.mean(y, -1, keepdims=True)
        var = jnp.mean(jnp.square(y - mean), -1, keepdims=True)
        y_heads.append((y - mean) * lax.rsqrt(var + RWKV_GN_EPS))
    yn = jnp.concatenate(y_heads, axis=1) * lnw_ref[...] + lnb_ref[...]
    bonus = head_sum(r * kmod * rk_ref[...]) * v
    o_ref[...] = ((yn + bonus) * g_ref[...]).astype(o_ref.dtype)


def rwkv_mixer(x, B, S, gain, mu, w_rkv, w0, w1, w2, a0, a1, a2, g1, g2, k_k, k_a, r_k, ln_w, ln_b, w_out,
               *, tm=512, tn=1024, tl=256, tt=512):
    T, D = x.shape
    tm, tl, tt = min(tm, S), min(tl, S), min(tt, S)
    nct = D // tn
    row = lambda i, n: (0, 0)

    def prev_rows(t):
        return lambda i, *_: (jnp.maximum(i * (t // V7X_SUBLANES) - 1, 0), 0)

    rkv = pl.pallas_call(
        functools.partial(_rwkv_rkv_body, tiles_per_seq=S // tm, n_col_tiles=nct),
        grid=(T // tm, 3 * nct),
        in_specs=[
            pl.BlockSpec((tm, D), lambda i, n: (i, 0)),
            pl.BlockSpec((V7X_SUBLANES, D), prev_rows(tm)),
            pl.BlockSpec((1, D), row),
            pl.BlockSpec((6, D), row),
            _wspec(w_rkv, (None, D, tn), lambda i, n: (n // nct, 0, n % nct)),
        ],
        out_specs=pl.BlockSpec((tm, tn), lambda i, n: (i, n)),
        out_shape=jax.ShapeDtypeStruct((T, 3 * D), F32),
        scratch_shapes=[pltpu.VMEM((3, tm, D), BF16)],
        compiler_params=_cparams(("parallel", "arbitrary"),
                                 2 * tm * D * 4 + 3 * tm * D * 2 + 2 * D * tn * 2 + 2 * tm * tn * 4 + 3 * tm * D * 4),
        name="rwkv_rkv",
    )(x, x, gain.reshape(1, D), mu, w_rkv[0])

    lora = w1.shape[1]
    pad = (-lora) % V7X_LANES
    w1p, a1p = jnp.pad(w1, ((0, 0), (0, pad))), jnp.pad(a1, ((0, 0), (0, pad)))
    w2p, a2p = jnp.pad(w2, ((0, pad), (0, 0))), jnp.pad(a2, ((0, pad), (0, 0)))
    lp, gl = lora + pad, g1.shape[1]
    one = lambda i: (0, 0)
    tok_l = pl.BlockSpec((tl, D), lambda i: (i, 0))
    lw, a, g = pl.pallas_call(
        functools.partial(_rwkv_lora_body, tiles_per_seq=S // tl),
        grid=(T // tl,),
        in_specs=[
            tok_l,
            pl.BlockSpec((V7X_SUBLANES, D), prev_rows(tl)),
            pl.BlockSpec((1, D), one),
            pl.BlockSpec((6, D), one),
            pl.BlockSpec((1, D), one), pl.BlockSpec((D, lp), one), pl.BlockSpec((lp, D), one),
            pl.BlockSpec((1, D), one), pl.BlockSpec((D, lp), one), pl.BlockSpec((lp, D), one),
            pl.BlockSpec((D, gl), one), pl.BlockSpec((gl, D), one),
        ],
        out_specs=[tok_l, tok_l, tok_l],
        out_shape=[jax.ShapeDtypeStruct((T, D), F32)] * 3,
        compiler_params=_cparams(("parallel",), 14 * tl * D * 4 + 4 * (2 * D * lp + D * gl) * 2),
        name="rwkv_lora",
    )(x, x, gain.reshape(1, D), mu, w0.reshape(1, D), w1p, w2p, a0.reshape(1, D), a1p, a2p, g1, g2)

    nt = S // tt
    pw = 2 * RWKV_HEAD_DIM
    npair = D // pw

    def tok(col0):
        return pl.BlockSpec((tt, pw), lambda b, p, s: (b * nt + s, col0 + p))

    par = pl.BlockSpec((1, pw), lambda b, p, s: (0, p))
    o = pl.pallas_call(
        _rwkv_core_body,
        grid=(B, npair, nt),
        in_specs=[tok(0), tok(npair), tok(2 * npair), tok(0), tok(0), tok(0), par, par, par, par, par],
        out_specs=pl.BlockSpec((tt, pw), lambda b, p, s: (b * nt + s, p)),
        out_shape=jax.ShapeDtypeStruct((T, D), BF16),
        scratch_shapes=[pltpu.VMEM((2, RWKV_HEAD_DIM, RWKV_HEAD_DIM), F32)],
        compiler_params=_cparams(("parallel", "parallel", "arbitrary"), 40 * tt * pw * 4 + 8 * tt * tt * 4),
        name="rwkv_core",
    )(rkv, rkv, rkv, lw, a, g, k_k.reshape(1, D), k_a.reshape(1, D), r_k.reshape(1, D),
      ln_w.reshape(1, D), ln_b.reshape(1, D))
    return matmul_residual(o, w_out, x)


def kernel(x, mem, positions, ffn_norm, ffn_w_gate, ffn_w_up, ffn_w_down, mix_norm, xattn_norm, mem_norm, xattn_wq, xattn_wkv, xattn_wo, xattn_q_gain, xattn_k_gain, conv_w_in, conv_w, conv_w_out, dil_w_qkv, dil_q_gain, dil_k_gain, dil_w_out, hgrn_w_in, hgrn_lb_logits, hgrn_norm, hgrn_w_out, rwkv_mu, rwkv_w_rkv, rwkv_w0, rwkv_w1, rwkv_w2, rwkv_a0, rwkv_a1, rwkv_a2, rwkv_g1, rwkv_g2, rwkv_k_k, rwkv_k_a, rwkv_r_k, rwkv_ln_w, rwkv_ln_b, rwkv_w_out):
    B, S, D = x.shape
    assert D == D_MODEL and S % (DIL_BLOCK * DIL_PATTERNS[-1][1]) == 0
    depth = ffn_norm.shape[0]
    xf = x.reshape(B * S, D)
    memf = mem.reshape(B * MEM_LEN, D)
    bf = lambda w: w.astype(BF16)
    wg_all, wu_all, wd_all = bf(ffn_w_gate), bf(ffn_w_up), bf(ffn_w_down)
    wq_all, wkv_all, wo_all = bf(xattn_wq), bf(xattn_wkv), bf(xattn_wo)
    conv_in_all, conv_out_all = bf(conv_w_in), bf(conv_w_out)
    dil_qkv_all, dil_out_all = bf(dil_w_qkv), bf(dil_w_out)
    hgrn_in_all, hgrn_out_all = bf(hgrn_w_in), bf(hgrn_w_out)
    rkv_all, rwkv_out_all = bf(rwkv_w_rkv), bf(rwkv_w_out)
    for i in range(depth):
        kind, j = i % N_MIXERS, i // N_MIXERS
        xf = ffn_half(xf, ffn_norm[i, 0], (wg_all, (i, 0)), (wu_all, (i, 0)), (wd_all, (i, 0)))
        if kind == 0:
            xf = conv_mixer(xf, S, mix_norm[i], (conv_in_all, (j,)), conv_w[j], (conv_out_all, (j,)))
        elif kind == 1:
            xf = dilated_mixer(xf, B, S, positions, mix_norm[i], (dil_qkv_all, (j,)), dil_q_gain[j], dil_k_gain[j],
                               (dil_out_all, (j,)))
        elif kind == 2:
            xf = hgrn_mixer(xf, B, S, i, mix_norm[i], (hgrn_in_all, (j,)), hgrn_lb_logits, hgrn_norm[j],
                            (hgrn_out_all, (j,)))
        else:
            xf = rwkv_mixer(xf, B, S, mix_norm[i], rwkv_mu[j], (rkv_all, (j,)), rwkv_w0[j], bf(rwkv_w1[j]),
                            bf(rwkv_w2[j]), rwkv_a0[j], bf(rwkv_a1[j]), bf(rwkv_a2[j]), bf(rwkv_g1[j]),
                            bf(rwkv_g2[j]), rwkv_k_k[j], rwkv_k_a[j], rwkv_r_k[j], rwkv_ln_w[j], rwkv_ln_b[j],
                            (rwkv_out_all, (j,)))
        xf = cross_attention(xf, S, memf, xattn_norm[i], mem_norm[i], (wq_all, (i,)), (wkv_all, (i,)),
                             (wo_all, (i,)), xattn_q_gain[i], xattn_k_gain[i])
        xf = ffn_half(xf, ffn_norm[i, 1], (wg_all, (i, 1)), (wu_all, (i, 1)), (wd_all, (i, 1)))
    return xf.reshape(B, S, D)
```

```python
import functools

import jax
import jax.numpy as jnp
from jax import lax
from jax.experimental import pallas as pl
from jax.experimental.pallas import tpu as pltpu

F32 = jnp.float32
BF16 = jnp.bfloat16

D_MODEL = 2048
DEPTH = 4
N_MIXERS = 4
MEM_LEN = 256
FFN_DIM = 5632
NORM_EPS = 1e-6
NEG_INF = -1e30
ROPE_THETA = 500000.0
DIL_PATTERNS = ((128, 1), (512, 4), (2048, 16))
DIL_GROUPS = 3
DIL_HEADS = 8
DIL_HEAD_DIM = 128
DIL_BLOCK = 128
DIL_UNITS_IN_FLIGHT = 8
HGRN_CHUNK = 16
HGRN_HEADS = 16
RWKV_HEAD_DIM = 64
RWKV_HEADS = 32
RWKV_CHUNK = 64
RWKV_GN_EPS = 64e-5
XATTN_HEADS = 4
XATTN_HEAD_DIM = 512

V7X_LANES = 128
V7X_SUBLANES = 8
V7X_VMEM_BYTES = 64 * 2**20
V7X_VMEM_CAP = 56 * 2**20


def _cparams(sem, vmem_bytes):
    limit = min(int(vmem_bytes * 1.25) + (4 << 20), V7X_VMEM_CAP)
    return pltpu.CompilerParams(dimension_semantics=sem, vmem_limit_bytes=limit)


def _rms(x, gain):
    return x * lax.rsqrt(jnp.mean(x * x, axis=-1, keepdims=True) + NORM_EPS) * gain


def _dot(a, b):
    return jnp.dot(a, b, preferred_element_type=F32)


def _dot_nt(a, b):
    return lax.dot_general(a, b, (((1,), (1,)), ((), ())), preferred_element_type=F32)


def _dot_tn(a, b):
    return lax.dot_general(a, b, (((0,), (0,)), ((), ())), preferred_element_type=F32)


def _wshape(w):
    arr, lead = w
    return arr.shape[len(lead):]


def _wspec(w, block, tail):
    lead = tuple(w[1])
    return pl.BlockSpec((None,) * len(lead) + tuple(block), lambda *g: lead + tuple(tail(*g)))


def _chunk_sums(x, chunk):
    t, w = x.shape
    rows = lax.broadcasted_iota(jnp.int32, (t, t), 0)
    cols = lax.broadcasted_iota(jnp.int32, (t, t), 1)
    same = _chunk_of(rows, chunk) == _chunk_of(cols, chunk)
    tri = same & (cols <= rows)
    sel = jnp.concatenate([tri.astype(BF16), same.astype(BF16)], axis=0)
    hi = x.astype(BF16)
    r1 = x - hi.astype(F32)
    mid = r1.astype(BF16)
    lo = (r1 - mid.astype(F32)).astype(BF16)
    s = _dot(sel, jnp.concatenate([hi, mid, lo], axis=1))
    s = s[:, :w] + s[:, w:2 * w] + s[:, 2 * w:]
    return s[:t], s[t:], tri


def _chunk_of(idx, chunk):
    return jnp.right_shift(idx, chunk.bit_length() - 1)


def _silu(x):
    return x * jax.nn.sigmoid(x)


def _norm_matmul_body(x_ref, g_ref, w_ref, o_ref, h_scr):
    @pl.when(pl.program_id(1) == 0)
    def _():
        h_scr[...] = _rms(x_ref[...], g_ref[...]).astype(BF16)

    o_ref[...] = _dot(h_scr[...], w_ref[...]).astype(o_ref.dtype)


def norm_matmul(x, gain, w, *, tm=1024, tn=1024, out_dtype=F32):
    M, K = x.shape
    N = _wshape(w)[1]
    tm, tn = min(tm, M), min(tn, N)
    ob = jnp.dtype(out_dtype).itemsize
    vmem = 2 * tm * K * 4 + tm * K * 2 + 2 * K * tn * 2 + 2 * tm * tn * ob
    return pl.pallas_call(
        _norm_matmul_body,
        grid=(M // tm, N // tn),
        in_specs=[
            pl.BlockSpec((tm, K), lambda i, j: (i, 0)),
            pl.BlockSpec((1, K), lambda i, j: (0, 0)),
            _wspec(w, (K, tn), lambda i, j: (0, j)),
        ],
        out_specs=pl.BlockSpec((tm, tn), lambda i, j: (i, j)),
        out_shape=jax.ShapeDtypeStruct((M, N), out_dtype),
        scratch_shapes=[pltpu.VMEM((tm, K), BF16)],
        compiler_params=_cparams(("parallel", "arbitrary"), vmem),
        name="norm_matmul",
    )(x, gain.reshape(1, K), w[0])


def _ffn_body(x_ref, g_ref, wg_ref, wu_ref, wd_ref, o_ref, h_scr):
    @pl.when(pl.program_id(1) == 0)
    def _():
        x = x_ref[...]
        h_scr[...] = _rms(x, g_ref[...]).astype(BF16)
        o_ref[...] = x

    h = h_scr[...]
    act = _silu(_dot(h, wg_ref[...])) * _dot(h, wu_ref[...])
    o_ref[...] += 0.5 * _dot(act.astype(BF16), wd_ref[...])


def ffn_half(x, gain, wg, wu, wd, *, tm=1024, tf=512):
    M, D = x.shape
    F = _wshape(wg)[1]
    tm = min(tm, M)
    vmem = 3 * tm * D * 4 + tm * D * 2 + 2 * 3 * D * tf * 2 + 3 * tm * tf * 4
    return pl.pallas_call(
        _ffn_body,
        grid=(M // tm, F // tf),
        in_specs=[
            pl.BlockSpec((tm, D), lambda i, f: (i, 0), pipeline_mode=pl.Buffered(1)),
            pl.BlockSpec((1, D), lambda i, f: (0, 0)),
            _wspec(wg, (D, tf), lambda i, f: (0, f)),
            _wspec(wu, (D, tf), lambda i, f: (0, f)),
            _wspec(wd, (tf, D), lambda i, f: (f, 0)),
        ],
        out_specs=pl.BlockSpec((tm, D), lambda i, f: (i, 0)),
        out_shape=jax.ShapeDtypeStruct((M, D), F32),
        scratch_shapes=[pltpu.VMEM((tm, D), BF16)],
        compiler_params=_cparams(("parallel", "arbitrary"), vmem),
        name="ffn_half",
    )(x, gain.reshape(1, D), wg[0], wu[0], wd[0])


def _matmul_res_body(a_ref, w_ref, r_ref, o_ref):
    o_ref[...] = r_ref[...] + _dot(a_ref[...], w_ref[...])


def matmul_residual(a, w, res, *, tm=1024, tn=1024):
    M, K = a.shape
    N = _wshape(w)[1]
    tm = min(tm, M)
    vmem = 2 * tm * K * 2 + 2 * K * tn * 2 + 4 * tm * tn * 4
    return pl.pallas_call(
        _matmul_res_body,
        grid=(M // tm, N // tn),
        in_specs=[
            pl.BlockSpec((tm, K), lambda i, j: (i, 0)),
            _wspec(w, (K, tn), lambda i, j: (0, j)),
            pl.BlockSpec((tm, tn), lambda i, j: (i, j)),
        ],
        out_specs=pl.BlockSpec((tm, tn), lambda i, j: (i, j)),
        out_shape=jax.ShapeDtypeStruct((M, N), F32),
        compiler_params=_cparams(("parallel", "arbitrary"), vmem),
        name="matmul_residual",
    )(a, w[0], res)


def _prologue_matmul_res_body(prologue, n_in, *refs):
    in_refs = refs[:n_in]
    w_ref, r_ref, o_ref, lhs_scr = refs[n_in:]
    row_tile = pl.program_id(0)

    @pl.when(pl.program_id(1) == 0)
    def _():
        prologue(row_tile, *in_refs, lhs_scr)

    o_ref[...] = r_ref[...] + _dot(lhs_scr[...], w_ref[...])


def prologue_matmul_residual(prologue, inputs, in_specs, w, res, *, tm, tn, in_vmem, name):
    M, N = res.shape
    K = _wshape(w)[0]
    vmem = in_vmem + tm * K * 2 + 2 * K * tn * 2 + 4 * tm * tn * 4
    return pl.pallas_call(
        functools.partial(_prologue_matmul_res_body, prologue, len(inputs)),
        grid=(M // tm, N // tn),
        in_specs=list(in_specs) + [
            _wspec(w, (K, tn), lambda i, j: (0, j)),
            pl.BlockSpec((tm, tn), lambda i, j: (i, j)),
        ],
        out_specs=pl.BlockSpec((tm, tn), lambda i, j: (i, j)),
        out_shape=jax.ShapeDtypeStruct((M, N), F32),
        scratch_shapes=[pltpu.VMEM((tm, K), BF16)],
        compiler_params=_cparams(("parallel", "arbitrary"), vmem),
        name=name,
    )(*inputs, w[0], res)


def _conv_prologue(tiles_per_seq, row_tile, b_ref, c_ref, u_ref, cp_ref, up_ref, cw_ref, lhs_scr):
    cu = c_ref[...] * u_ref[...]
    prev = cp_ref[...] * up_ref[...]
    first = (row_tile % tiles_per_seq) == 0
    prev = jnp.where(first, 0.0, prev)
    p1, p2 = prev[7:8, :], prev[6:7, :]
    row = lax.broadcasted_iota(jnp.int32, cu.shape, 0)
    s1 = jnp.where(row == 0, p1, pltpu.roll(cu, 1, 0))
    s2 = jnp.where(row == 0, p2, jnp.where(row == 1, p1, pltpu.roll(cu, 2, 0)))
    w = cw_ref[...]
    y = w[0:1, :] * s2 + w[1:2, :] * s1 + w[2:3, :] * cu
    lhs_scr[...] = (b_ref[...] * y).astype(BF16)


def conv_mixer(x, S, gain, w_in, conv_w, w_out, *, tm=512):
    T, D = x.shape
    tm = min(tm, S)
    bcu = norm_matmul(x, gain, w_in)
    r8 = tm // V7X_SUBLANES

    def prev_map(col):
        return lambda i, j: (jnp.maximum(i * r8 - 1, 0), col)

    in_specs = [
        pl.BlockSpec((tm, D), lambda i, j: (i, 0)),
        pl.BlockSpec((tm, D), lambda i, j: (i, 1)),
        pl.BlockSpec((tm, D), lambda i, j: (i, 2)),
        pl.BlockSpec((V7X_SUBLANES, D), prev_map(1)),
        pl.BlockSpec((V7X_SUBLANES, D), prev_map(2)),
        pl.BlockSpec((3, D), lambda i, j: (0, 0)),
    ]
    return prologue_matmul_residual(
        functools.partial(_conv_prologue, S // tm), (bcu, bcu, bcu, bcu, bcu, conv_w), in_specs, w_out, x,
        tm=tm, tn=1024, in_vmem=2 * 3 * tm * D * 4 + 4 * V7X_SUBLANES * D * 4, name="conv_mixer_out")


def _xattn_body(x_ref, kv_ref, xg_ref, qg_ref, kg_ref, wq_ref, wo_ref, o_ref, attn_scr):
    scale = XATTN_HEAD_DIM ** -0.5
    D = XATTN_HEADS * XATTN_HEAD_DIM
    x = x_ref[...]
    q = _dot(_rms(x, xg_ref[...]).astype(BF16), wq_ref[...])
    for h in range(XATTN_HEADS):
        sl = slice(h * XATTN_HEAD_DIM, (h + 1) * XATTN_HEAD_DIM)
        qn = _rms(q[:, sl], qg_ref[...]).astype(BF16)
        kn = _rms(kv_ref[:, sl], kg_ref[...]).astype(BF16)
        v = kv_ref[:, D + h * XATTN_HEAD_DIM:D + (h + 1) * XATTN_HEAD_DIM].astype(BF16)
        s = _dot_nt(qn, kn) * scale
        p = jnp.exp(s - jnp.max(s, axis=-1, keepdims=True))
        l = jnp.sum(p, axis=-1, keepdims=True)
        attn_scr[:, sl] = (_dot(p.astype(BF16), v) / l).astype(BF16)
    o_ref[...] = x + _dot(attn_scr[...], wo_ref[...])


def cross_attention(x, S, mem, xgain, mgain, wq, wkv, wo, q_gain, k_gain, *, tm=512):
    T, D = x.shape
    tm = min(tm, S)
    kv = norm_matmul(mem, mgain, wkv)
    tps = S // tm
    once = pl.Buffered(1)
    vmem = 4 * tm * D * 4 + 2 * MEM_LEN * 2 * D * 4 + 2 * D * D * 2 + tm * D * 2 + 3 * tm * D * 4
    return pl.pallas_call(
        _xattn_body,
        grid=(T // tm,),
        in_specs=[
            pl.BlockSpec((tm, D), lambda i: (i, 0)),
            pl.BlockSpec((MEM_LEN, 2 * D), lambda i: (i // tps, 0)),
            pl.BlockSpec((1, D), lambda i: (0, 0)),
            pl.BlockSpec((1, XATTN_HEAD_DIM), lambda i: (0, 0)),
            pl.BlockSpec((1, XATTN_HEAD_DIM), lambda i: (0, 0)),
            pl.BlockSpec((None,) * len(wq[1]) + (D, D), lambda i: tuple(wq[1]) + (0, 0), pipeline_mode=once),
            pl.BlockSpec((None,) * len(wo[1]) + (D, D), lambda i: tuple(wo[1]) + (0, 0), pipeline_mode=once),
        ],
        out_specs=pl.BlockSpec((tm, D), lambda i: (i, 0)),
        out_shape=jax.ShapeDtypeStruct((T, D), F32),
        scratch_shapes=[pltpu.VMEM((tm, D), BF16)],
        compiler_params=_cparams(("parallel",), vmem),
        name="xattn",
    )(x, kv, xgain.reshape(1, D), q_gain.reshape(1, -1), k_gain.reshape(1, -1), wq[0], wo[0])


def _dil_prep_body(x_ref, pos_ref, invf_ref, qg_ref, kg_ref, o_ref):
    ang = pos_ref[...] * invf_ref[...]
    lane = lax.broadcasted_iota(jnp.int32, ang.shape, 1)
    half = DIL_HEAD_DIM // 8
    cos, sin = jnp.cos(ang), jnp.sin(ang)
    sin_lo = jnp.where(lane < half, -sin, 0.0)
    sin_hi = jnp.where((lane >= half) & (lane < 2 * half), sin, 0.0)
    r = lax.broadcasted_iota(jnp.int32, (2 * DIL_HEAD_DIM, 2 * DIL_HEAD_DIM), 0)
    c = lax.broadcasted_iota(jnp.int32, (2 * DIL_HEAD_DIM, 2 * DIL_HEAD_DIM), 1)
    mean_mat = jnp.where((r < DIL_HEAD_DIM) == (c < DIL_HEAD_DIM), 1.0 / DIL_HEAD_DIM, 0.0).astype(BF16)
    for part, g_ref in ((0, qg_ref), (1, kg_ref)):
        for g in range(DIL_GROUPS):
            gain = g_ref[g:g + 1, :]
            for pair in range(DIL_HEADS // 2):
                col = ((part * DIL_GROUPS + g) * DIL_HEADS + 2 * pair) * DIL_HEAD_DIM
                x2 = x_ref[:, col:col + 2 * DIL_HEAD_DIM]
                inv = lax.rsqrt(_dot((x2 * x2).astype(BF16), mean_mat) + NORM_EPS)
                for hh in range(2):
                    sl = slice(hh * DIL_HEAD_DIM, (hh + 1) * DIL_HEAD_DIM)
                    xn = x2[:, sl] * inv[:, sl] * gain
                    o_ref[:, col + sl.start:col + sl.stop] = (
                        xn * cos + pltpu.roll(xn, DIL_HEAD_DIM - half, 1) * sin_lo
                        + pltpu.roll(xn, half, 1) * sin_hi)


def _dil_attn_body(*refs, n_chunks):
    ins, o_ref, scr = refs[:15], refs[15], refs[16:]
    c = pl.program_id(1)
    scale = DIL_HEAD_DIM ** -0.5
    ii = lax.broadcasted_iota(jnp.int32, (DIL_BLOCK, DIL_BLOCK), 0)
    jj = lax.broadcasted_iota(jnp.int32, (DIL_BLOCK, DIL_BLOCK), 1)
    cur_mask = jj <= ii
    prev_mask = jj >= ii
    ch = o_ref.shape[0]
    for g, (_, dil) in enumerate(DIL_PATTERNS):
        q_ref, k_ref, v_ref, kh_ref, vh_ref = ins[5 * g:5 * g + 5]
        kf, vf, og, lg = scr[4 * g:4 * g + 4]
        hist = DIL_BLOCK * dil
        kf[0:hist, :] = kh_ref[...]
        vf[0:hist, :] = vh_ref[...]
        kf[hist:hist + ch, :] = k_ref[...]
        vf[hist:hist + ch, :] = v_ref[...]
        def rows(start):
            return pl.ds(start, DIL_BLOCK, stride=dil) if dil > 1 else pl.ds(start, DIL_BLOCK)

        units = [(blk, blk * hist + r) for blk in range(ch // hist) for r in range(dil)]
        for b0 in range(0, len(units), DIL_UNITS_IN_FLIGHT):
            batch = units[b0:b0 + DIL_UNITS_IN_FLIGHT]
            qv = [q_ref[rows(q0), :].astype(BF16) for _, q0 in batch]
            sc = [_dot_nt(q, kf[rows(hist + q0), :].astype(BF16)) for q, (_, q0) in zip(qv, batch)]
            sp = [_dot_nt(q, kf[rows(q0), :].astype(BF16)) for q, (_, q0) in zip(qv, batch)]
            sc = [jnp.where(cur_mask, s * scale, NEG_INF) for s in sc]
            sp = [jnp.where(prev_mask if blk > 0 else prev_mask & (c > 0), s * scale, NEG_INF)
                  for s, (blk, _) in zip(sp, batch)]
            m = [jnp.maximum(jnp.max(a, -1, keepdims=True), jnp.max(b, -1, keepdims=True)) for a, b in zip(sc, sp)]
            pc = [jnp.exp(a - mm) for a, mm in zip(sc, m)]
            pp = [jnp.exp(b - mm) for b, mm in zip(sp, m)]
            l = [jnp.sum(a, -1, keepdims=True) + jnp.sum(b, -1, keepdims=True) for a, b in zip(pc, pp)]
            o = [_dot(a.astype(BF16), vf[rows(hist + q0), :].astype(BF16))
                 + _dot(b.astype(BF16), vf[rows(q0), :].astype(BF16)) for a, b, (_, q0) in zip(pc, pp, batch)]
            for oo, ll, mm, (_, q0) in zip(o, l, m, batch):
                og[rows(q0), :] = oo / ll
                lg[rows(q0), :] = jnp.broadcast_to(mm + jnp.log(ll), (DIL_BLOCK, DIL_HEAD_DIM))
    l0, l1, l2 = scr[3][...], scr[7][...], scr[11][...]
    mx = jnp.maximum(jnp.maximum(l0, l1), l2)
    e0, e1, e2 = jnp.exp(l0 - mx), jnp.exp(l1 - mx), jnp.exp(l2 - mx)
    o_ref[...] = ((e0 * scr[2][...] + e1 * scr[6][...] + e2 * scr[10][...]) / (e0 + e1 + e2)).astype(o_ref.dtype)


def dilated_mixer(x, B, S, positions, gain, w_qkv, q_gain, k_gain, w_out):
    T, D = x.shape
    nh = DIL_GROUPS * DIL_HEADS
    rot = DIL_HEAD_DIM // 4
    inv_freq = ROPE_THETA ** (-jnp.arange(0, rot, 2, dtype=F32) / rot)
    invf = jnp.concatenate([inv_freq, inv_freq, jnp.zeros((DIL_HEAD_DIM - rot,), F32)]).reshape(1, DIL_HEAD_DIM)
    pos = positions.astype(F32).reshape(T, 1)
    qkv = norm_matmul(x, gain, w_qkv)
    tp = 256
    qk_cols = 2 * nh * DIL_HEAD_DIM
    qk = pl.pallas_call(
        _dil_prep_body,
        grid=(T // tp,),
        in_specs=[
            pl.BlockSpec((tp, qk_cols), lambda i: (i, 0)),
            pl.BlockSpec((tp, 1), lambda i: (i, 0)),
            pl.BlockSpec((1, DIL_HEAD_DIM), lambda i: (0, 0)),
            pl.BlockSpec((DIL_GROUPS, DIL_HEAD_DIM), lambda i: (0, 0)),
            pl.BlockSpec((DIL_GROUPS, DIL_HEAD_DIM), lambda i: (0, 0)),
        ],
        out_specs=pl.BlockSpec((tp, qk_cols), lambda i: (i, 0)),
        out_shape=jax.ShapeDtypeStruct((T, qk_cols), F32),
        compiler_params=_cparams(("parallel",), 4 * tp * qk_cols * 4 + 2 * tp * V7X_LANES * 4),
        name="dil_qk_prep",
    )(qkv, pos, invf, q_gain, k_gain)

    ch = DIL_BLOCK * DIL_PATTERNS[-1][1]
    n_chunks = S // ch
    inputs, in_specs, scratch = [], [], []
    vmem = 2 * ch * DIL_HEAD_DIM * 2
    for g, (_, dil) in enumerate(DIL_PATTERNS):
        hist = DIL_BLOCK * dil
        per = ch // hist

        def cur_map(col):
            return lambda b, c, h: (b * n_chunks + c, col + h)

        def hist_map(col, per=per):
            return lambda b, c, h: (jnp.maximum((b * n_chunks + c) * per - 1, 0), col + h)

        inputs += [qk, qk, qkv, qk, qkv]
        in_specs += [
            pl.BlockSpec((ch, DIL_HEAD_DIM), cur_map(g * DIL_HEADS)),
            pl.BlockSpec((ch, DIL_HEAD_DIM), cur_map(nh + g * DIL_HEADS)),
            pl.BlockSpec((ch, DIL_HEAD_DIM), cur_map(2 * nh + g * DIL_HEADS)),
            pl.BlockSpec((hist, DIL_HEAD_DIM), hist_map(nh + g * DIL_HEADS)),
            pl.BlockSpec((hist, DIL_HEAD_DIM), hist_map(2 * nh + g * DIL_HEADS)),
        ]
        scratch += [pltpu.VMEM((hist + ch, DIL_HEAD_DIM), F32), pltpu.VMEM((hist + ch, DIL_HEAD_DIM), F32),
                    pltpu.VMEM((ch, DIL_HEAD_DIM), F32), pltpu.VMEM((ch, DIL_HEAD_DIM), F32)]
        vmem += (2 * (3 * ch + 2 * hist) + 2 * (hist + ch) + 2 * ch) * DIL_HEAD_DIM * 4
    o = pl.pallas_call(
        functools.partial(_dil_attn_body, n_chunks=n_chunks),
        grid=(B, n_chunks, DIL_HEADS),
        in_specs=in_specs,
        out_specs=pl.BlockSpec((ch, DIL_HEAD_DIM), lambda b, c, h: (b * n_chunks + c, h)),
        out_shape=jax.ShapeDtypeStruct((T, DIL_HEADS * DIL_HEAD_DIM), BF16),
        scratch_shapes=scratch,
        compiler_params=_cparams(("parallel", "arbitrary", "arbitrary"), vmem),
        name="dil_attention",
    )(*inputs)
    return matmul_residual(o, w_out, x)


def _hgrn_body(q_ref, f_ref, i_ref, gt_ref, lbl_ref, gain_ref, o_ref, st_scr, *, layer):
    @pl.when(pl.program_id(2) == 0)
    def _():
        st_scr[...] = jnp.zeros_like(st_scr)

    tt = q_ref.shape[0]
    dh = gain_ref.shape[1]
    heads = [slice(h * dh, (h + 1) * dh) for h in range(q_ref.shape[1] // dh)]
    lbl = lbl_ref[...]
    e = jnp.exp(lbl - jnp.max(lbl, axis=0, keepdims=True))
    p = e / jnp.sum(e, axis=0, keepdims=True)
    lb = jnp.sum(p[1:layer + 1, :], axis=0, keepdims=True)
    forget = lb + (1.0 - lb) * jax.nn.sigmoid(f_ref[...])
    k = 1.0 - forget
    gl = jnp.log(forget)
    a_cum, a_tot, tri = _chunk_sums(gl, HGRN_CHUNK)
    q_dec = (q_ref[...] * jnp.exp(a_cum)).astype(BF16)
    k_in = (k * jnp.exp(-a_cum)).astype(BF16)
    k_end = (k * jnp.exp(a_tot - a_cum)).astype(BF16)
    v = i_ref[...].astype(BF16)
    dec = jnp.exp(a_tot)
    att = [jnp.where(tri, _dot_nt(q_dec[:, hs], k_in[:, hs]), 0.0).astype(BF16) for hs in heads]
    o = [_dot(a, v[:, hs]) for a, hs in zip(att, heads)]
    chunks = [slice(c * HGRN_CHUNK, (c + 1) * HGRN_CHUNK) for c in range(tt // HGRN_CHUNK)]
    upd = [[_dot_tn(v[sl, hs], k_end[sl, hs]) for sl in chunks] for hs in heads]
    st = [st_scr[h] for h in range(len(heads))]
    inter = [[] for _ in heads]
    for ci, sl in enumerate(chunks):
        for h, hs in enumerate(heads):
            inter[h].append(_dot_nt(q_dec[sl, hs], st[h].astype(BF16)))
            st[h] = st[h] * dec[sl.start:sl.start + 1, hs] + upd[h][ci]
    gt = gt_ref[...]
    for h, hs in enumerate(heads):
        st_scr[h] = st[h]
        oh = o[h] + jnp.concatenate(inter[h], axis=0)
        o_ref[:, hs] = (_rms(oh, gain_ref[...]) * _silu(gt[:, hs])).astype(o_ref.dtype)


def hgrn_mixer(x, B, S, layer, gain, w_in, lb_logits, norm_gain, w_out, *, tt=256, heads_per_step=8):
    T, D = x.shape
    dh = D // HGRN_HEADS
    proj = norm_matmul(x, gain, w_in)
    tt = min(tt, S)
    nt = S // tt
    ng = HGRN_HEADS // heads_per_step
    wd = heads_per_step * dh

    def part(pidx):
        return pl.BlockSpec((tt, wd), lambda b, h, s: (b * nt + s, pidx * ng + h))

    o = pl.pallas_call(
        functools.partial(_hgrn_body, layer=layer),
        grid=(B, ng, nt),
        in_specs=[part(0), part(1), part(2), part(3),
                  pl.BlockSpec((DEPTH, wd), lambda b, h, s: (0, h)),
                  pl.BlockSpec((1, dh), lambda b, h, s: (0, 0))],
        out_specs=pl.BlockSpec((tt, wd), lambda b, h, s: (b * nt + s, h)),
        out_shape=jax.ShapeDtypeStruct((T, D), BF16),
        scratch_shapes=[pltpu.VMEM((heads_per_step, dh, dh), F32)],
        compiler_params=_cparams(("parallel", "parallel", "arbitrary"), 24 * tt * wd * 4 + 8 * tt * tt * 4),
        name="hgrn_core",
    )(proj, proj, proj, proj, lb_logits, norm_gain.reshape(1, dh))
    return matmul_residual(o, w_out, x)


def _rwkv_shift_mix(x_ref, xp_ref, gn_ref, first):
    gn = gn_ref[...]
    h = _rms(x_ref[...], gn)
    last = _rms(xp_ref[...], gn)[V7X_SUBLANES - 1:V7X_SUBLANES, :]
    last = jnp.where(first, 0.0, last)
    row = lax.broadcasted_iota(jnp.int32, h.shape, 0)
    return h, jnp.where(row == 0, last, pltpu.roll(h, 1, 0)) - h


def _rwkv_rkv_body(x_ref, xp_ref, gn_ref, mu_ref, wrkv_ref, rkv_ref, mix_scr, *, tiles_per_seq, n_col_tiles):
    n = pl.program_id(1)
    first = (pl.program_id(0) % tiles_per_seq) == 0

    @pl.when(n == 0)
    def _():
        h, d = _rwkv_shift_mix(x_ref, xp_ref, gn_ref, first)
        mu = mu_ref[...]
        for m in range(3):
            mix_scr[m] = (h + d * mu[m:m + 1, :]).astype(BF16)

    rkv_ref[...] = _dot(mix_scr[n // n_col_tiles], wrkv_ref[...])


def _rwkv_lora_body(x_ref, xp_ref, gn_ref, mu_ref, w0_ref, w1_ref, w2_ref, a0_ref, a1_ref, a2_ref, g1_ref, g2_ref,
                    lw_ref, a_ref, g_ref, *, tiles_per_seq):
    h, d = _rwkv_shift_mix(x_ref, xp_ref, gn_ref, (pl.program_id(0) % tiles_per_seq) == 0)
    mu = mu_ref[...]
    xw = (h + d * mu[3:4, :]).astype(BF16)
    xa = (h + d * mu[4:5, :]).astype(BF16)
    xg = (h + d * mu[5:6, :]).astype(BF16)
    z = -(w0_ref[...] + _dot(jnp.tanh(_dot(xw, w1_ref[...])).astype(BF16), w2_ref[...]))
    softplus = jnp.maximum(z, 0.0) + jnp.log1p(jnp.exp(-jnp.abs(z)))
    lw_ref[...] = -jnp.exp(-softplus - 0.5)
    a_ref[...] = jax.nn.sigmoid(a0_ref[...] + _dot(_dot(xa, a1_ref[...]).astype(BF16), a2_ref[...]))
    g_ref[...] = _dot(jax.nn.sigmoid(_dot(xg, g1_ref[...])).astype(BF16), g2_ref[...])


def _rwkv_core_body(r_ref, k_ref, v_ref, lw_ref, a_ref, g_ref, kk_ref, ka_ref, rk_ref, lnw_ref, lnb_ref,
                    o_ref, h_scr):
    @pl.when(pl.program_id(2) == 0)
    def _():
        h_scr[...] = jnp.zeros_like(h_scr)

    tt = r_ref.shape[0]
    C, N = RWKV_CHUNK, RWKV_HEAD_DIM
    lw = lw_ref[...]
    g_cum, g_tot, _ = _chunk_sums(lw, C)
    r, k, v, a = r_ref[...], k_ref[...], v_ref[...], a_ref[...]
    left = lax.broadcasted_iota(jnp.int32, (tt, 2 * N), 1) < N

    def head_sum(t):
        return jnp.where(left, jnp.sum(jnp.where(left, t, 0.0), -1, keepdims=True),
                         jnp.sum(jnp.where(left, 0.0, t), -1, keepdims=True))

    kk = k * kk_ref[...]
    kk = kk * lax.rsqrt(jnp.maximum(head_sum(kk * kk), 1e-24))
    kmod = k * (1.0 + (a - 1.0) * ka_ref[...])
    bv = kk * a
    e_neg = jnp.exp(-g_cum)
    e_end = jnp.exp(g_tot - g_cum)
    a_t = (-kk) * jnp.exp(g_cum - lw)
    r_t = r * jnp.exp(g_cum)
    k_t, b_t = kmod * e_neg, bv * e_neg
    k_h, b_h = kmod * e_end, bv * e_end
    dec = jnp.exp(g_tot)
    def key_lanes(t):
        return [jnp.where(left, t, 0.0), jnp.where(left, pltpu.roll(t, N, 1), 0.0)]

    a_k, r_k, bt_k, kt_k, bh_k, kh_k, dec_k = map(key_lanes, (a_t, r_t, b_t, k_t, b_h, k_h, dec))
    v_v = [jnp.where(left, 0.0, pltpu.roll(v, N, 1)), jnp.where(left, 0.0, v)]
    i2 = lax.broadcasted_iota(jnp.int32, (C, 2 * C), 0)
    lane2 = lax.broadcasted_iota(jnp.int32, (C, 2 * C), 1)
    lo = lane2 < C
    t2 = jnp.bitwise_and(lane2, C - 1)
    strict, incl = t2 < i2, t2 <= i2
    eye_hi = jnp.where(lane2 == i2 + C, 1.0, 0.0)
    zeros16 = jnp.zeros((C, 2 * C), BF16)

    nc = tt // C
    units = [(hh, c) for hh in range(2) for c in range(nc)]

    def cut(per_head, u):
        return per_head[u[0]][u[1] * C:(u[1] + 1) * C, :]

    v_c = [cut(v_v, u).astype(BF16) for u in units]
    prod = [_dot_nt(jnp.concatenate([cut(a_k, u), cut(r_k, u)], axis=0).astype(BF16),
                    jnp.concatenate([cut(bt_k, u), cut(kt_k, u)], axis=0).astype(BF16))
            for u in units]
    n_abk = [jnp.where(strict, p[:C], 0.0) for p in prod]
    t_rbk = [jnp.where(incl, p[C:], 0.0).astype(BF16) for p in prod]
    n_lo = [jnp.where(lo, n, 0.0) for n in n_abk]
    s1 = [_dot(n.astype(BF16), jnp.concatenate([nl.astype(BF16), vv], axis=0))
          for n, nl, vv in zip(n_abk, n_lo, v_c)]
    z = [jnp.where(lo, s, 0.0) + pltpu.roll(nl, C, 1) + eye_hi for s, nl in zip(s1, n_lo)]
    for _ in range(5):
        z16 = [zz.astype(BF16) for zz in z]
        z = [_dot(zz16[:, :C], zz16) + jnp.where(lo, 0.0, zz) for zz, zz16 in zip(z, z16)]
    w0 = [(cut(a_k, u) + jnp.where(lo, 0.0, s)).astype(BF16) for u, s in zip(units, s1)]
    au = [_dot(zz.astype(BF16), jnp.concatenate([zeros16, w], axis=0)).astype(BF16)
          for zz, w in zip(z, w0)]
    auv = [jnp.concatenate([x, vv], axis=0) for x, vv in zip(au, v_c)]
    ry = [_dot(t, x) for t, x in zip(t_rbk, auv)]
    th = [_dot_tn(x, jnp.concatenate([cut(bh_k, u), cut(kh_k, u)], axis=0).astype(BF16))
          for x, u in zip(auv, units)]
    r_p = [(cut(r_k, u) + jnp.where(lo, y, 0.0)).astype(BF16) for u, y in zip(units, ry)]

    ht = [h_scr[hh] for hh in range(2)]
    ys = [[], []]
    for c in range(nc):
        for hh in range(2):
            i = hh * nc + c
            ht16 = ht[hh].astype(BF16)
            ys[hh].append(_dot_nt(r_p[i], ht16) + ry[i][:, C:])
            ht[hh] = (ht[hh] * dec_k[hh][c * C:c * C + 1, :] + _dot(ht16[:, :C], th[i][:C].astype(BF16))
                      + th[i][C:])
    y_heads = []
    for hh in range(2):
        h_scr[hh] = ht[hh]
        y = jnp.concatenate(ys[hh], axis=0)
        mean = jnp.mean(y, -1, keepdims=True)
        var = jnp.mean(jnp.square(y - mean), -1, keepdims=True)
        y_heads.append((y - mean) * lax.rsqrt(var + RWKV_GN_EPS))
    yn = jnp.concatenate(y_heads, axis=1) * lnw_ref[...] + lnb_ref[...]
    bonus = head_sum(r * kmod * rk_ref[...]) * v
    o_ref[...] = ((yn + bonus) * g_ref[...]).astype(o_ref.dtype)


def rwkv_mixer(x, B, S, gain, mu, w_rkv, w0, w1, w2, a0, a1, a2, g1, g2, k_k, k_a, r_k, ln_w, ln_b, w_out,
               *, tm=512, tn=1024, tl=256, tt=512):
    T, D = x.shape
    tm, tl, tt = min(tm, S), min(tl, S), min(tt, S)
    nct = D // tn
    row = lambda i, n: (0, 0)

    def prev_rows(t):
        return lambda i, *_: (jnp.maximum(i * (t // V7X_SUBLANES) - 1, 0), 0)

    rkv = pl.pallas_call(
        functools.partial(_rwkv_rkv_body, tiles_per_seq=S // tm, n_col_tiles=nct),
        grid=(T // tm, 3 * nct),
        in_specs=[
            pl.BlockSpec((tm, D), lambda i, n: (i, 0)),
            pl.BlockSpec((V7X_SUBLANES, D), prev_rows(tm)),
            pl.BlockSpec((1, D), row),
            pl.BlockSpec((6, D), row),
            _wspec(w_rkv, (None, D, tn), lambda i, n: (n // nct, 0, n % nct)),
        ],
        out_specs=pl.BlockSpec((tm, tn), lambda i, n: (i, n)),
        out_shape=jax.ShapeDtypeStruct((T, 3 * D), F32),
        scratch_shapes=[pltpu.VMEM((3, tm, D), BF16)],
        compiler_params=_cparams(("parallel", "arbitrary"),
                                 2 * tm * D * 4 + 3 * tm * D * 2 + 2 * D * tn * 2 + 2 * tm * tn * 4 + 3 * tm * D * 4),
        name="rwkv_rkv",
    )(x, x, gain.reshape(1, D), mu, w_rkv[0])

    lora = w1.shape[1]
    pad = (-lora) % V7X_LANES
    w1p, a1p = jnp.pad(w1, ((0, 0), (0, pad))), jnp.pad(a1, ((0, 0), (0, pad)))
    w2p, a2p = jnp.pad(w2, ((0, pad), (0, 0))), jnp.pad(a2, ((0, pad), (0, 0)))
    lp, gl = lora + pad, g1.shape[1]
    one = lambda i: (0, 0)
    tok_l = pl.BlockSpec((tl, D), lambda i: (i, 0))
    lw, a, g = pl.pallas_call(
        functools.partial(_rwkv_lora_body, tiles_per_seq=S // tl),
        grid=(T // tl,),
        in_specs=[
            tok_l,
            pl.BlockSpec((V7X_SUBLANES, D), prev_rows(tl)),
            pl.BlockSpec((1, D), one),
            pl.BlockSpec((6, D), one),
            pl.BlockSpec((1, D), one), pl.BlockSpec((D, lp), one), pl.BlockSpec((lp, D), one),
            pl.BlockSpec((1, D), one), pl.BlockSpec((D, lp), one), pl.BlockSpec((lp, D), one),
            pl.BlockSpec((D, gl), one), pl.BlockSpec((gl, D), one),
        ],
        out_specs=[tok_l, tok_l, tok_l],
        out_shape=[jax.ShapeDtypeStruct((T, D), F32)] * 3,
        compiler_params=_cparams(("parallel",), 14 * tl * D * 4 + 4 * (2 * D * lp + D * gl) * 2),
        name="rwkv_lora",
    )(x, x, gain.reshape(1, D), mu, w0.reshape(1, D), w1p, w2p, a0.reshape(1, D), a1p, a2p, g1, g2)

    nt = S // tt
    pw = 2 * RWKV_HEAD_DIM
    npair = D // pw

    def tok(col0):
        return pl.BlockSpec((tt, pw), lambda b, p, s: (b * nt + s, col0 + p))

    par = pl.BlockSpec((1, pw), lambda b, p, s: (0, p))
    o = pl.pallas_call(
        _rwkv_core_body,
        grid=(B, npair, nt),
        in_specs=[tok(0), tok(npair), tok(2 * npair), tok(0), tok(0), tok(0), par, par, par, par, par],
        out_specs=pl.BlockSpec((tt, pw), lambda b, p, s: (b * nt + s, p)),
        out_shape=jax.ShapeDtypeStruct((T, D), BF16),
        scratch_shapes=[pltpu.VMEM((2, RWKV_HEAD_DIM, pw), F32)],
        compiler_params=_cparams(("parallel", "parallel", "arbitrary"), 40 * tt * pw * 4 + 8 * tt * tt * 4),
        name="rwkv_core",
    )(rkv, rkv, rkv, lw, a, g, k_k.reshape(1, D), k_a.reshape(1, D), r_k.reshape(1, D),
      ln_w.reshape(1, D), ln_b.reshape(1, D))
    return matmul_residual(o, w_out, x)


def kernel(x, mem, positions, ffn_norm, ffn_w_gate, ffn_w_up, ffn_w_down, mix_norm, xattn_norm, mem_norm, xattn_wq, xattn_wkv, xattn_wo, xattn_q_gain, xattn_k_gain, conv_w_in, conv_w, conv_w_out, dil_w_qkv, dil_q_gain, dil_k_gain, dil_w_out, hgrn_w_in, hgrn_lb_logits, hgrn_norm, hgrn_w_out, rwkv_mu, rwkv_w_rkv, rwkv_w0, rwkv_w1, rwkv_w2, rwkv_a0, rwkv_a1, rwkv_a2, rwkv_g1, rwkv_g2, rwkv_k_k, rwkv_k_a, rwkv_r_k, rwkv_ln_w, rwkv_ln_b, rwkv_w_out):
    B, S, D = x.shape
    assert D == D_MODEL and S % (DIL_BLOCK * DIL_PATTERNS[-1][1]) == 0
    depth = ffn_norm.shape[0]
    xf = x.reshape(B * S, D)
    memf = mem.reshape(B * MEM_LEN, D)
    bf = lambda w: w.astype(BF16)
    wg_all, wu_all, wd_all = bf(ffn_w_gate), bf(ffn_w_up), bf(ffn_w_down)
    wq_all, wkv_all, wo_all = bf(xattn_wq), bf(xattn_wkv), bf(xattn_wo)
    conv_in_all, conv_out_all = bf(conv_w_in), bf(conv_w_out)
    dil_qkv_all, dil_out_all = bf(dil_w_qkv), bf(dil_w_out)
    hgrn_in_all, hgrn_out_all = bf(hgrn_w_in), bf(hgrn_w_out)
    rkv_all, rwkv_out_all = bf(rwkv_w_rkv), bf(rwkv_w_out)
    for i in range(depth):
        kind, j = i % N_MIXERS, i // N_MIXERS
        xf = ffn_half(xf, ffn_norm[i, 0], (wg_all, (i, 0)), (wu_all, (i, 0)), (wd_all, (i, 0)))
        if kind == 0:
            xf = conv_mixer(xf, S, mix_norm[i], (conv_in_all, (j,)), conv_w[j], (conv_out_all, (j,)))
        elif kind == 1:
            xf = dilated_mixer(xf, B, S, positions, mix_norm[i], (dil_qkv_all, (j,)), dil_q_gain[j], dil_k_gain[j],
                               (dil_out_all, (j,)))
        elif kind == 2:
            xf = hgrn_mixer(xf, B, S, i, mix_norm[i], (hgrn_in_all, (j,)), hgrn_lb_logits, hgrn_norm[j],
                            (hgrn_out_all, (j,)))
        else:
            xf = rwkv_mixer(xf, B, S, mix_norm[i], rwkv_mu[j], (rkv_all, (j,)), rwkv_w0[j], bf(rwkv_w1[j]),
                            bf(rwkv_w2[j]), rwkv_a0[j], bf(rwkv_a1[j]), bf(rwkv_a2[j]), bf(rwkv_g1[j]),
                            bf(rwkv_g2[j]), rwkv_k_k[j], rwkv_k_a[j], rwkv_r_k[j], rwkv_ln_w[j], rwkv_ln_b[j],
                            (rwkv_out_all, (j,)))
        xf = cross_attention(xf, S, memf, xattn_norm[i], mem_norm[i], (wq_all, (i,)), (wkv_all, (i,)),
                             (wo_all, (i,)), xattn_q_gain[i], xattn_k_gain[i])
        xf = ffn_half(xf, ffn_norm[i, 1], (wg_all, (i, 1)), (wu_all, (i, 1)), (wd_all, (i, 1)))
    return xf.reshape(B, S, D)
```

```python
import functools

import jax
import jax.numpy as jnp
from jax import lax
from jax.experimental import pallas as pl
from jax.experimental.pallas import tpu as pltpu

F32 = jnp.float32
BF16 = jnp.bfloat16

D_MODEL = 2048
DEPTH = 4
N_MIXERS = 4
MEM_LEN = 256
FFN_DIM = 5632
NORM_EPS = 1e-6
NEG_INF = -1e30
ROPE_THETA = 500000.0
DIL_PATTERNS = ((128, 1), (512, 4), (2048, 16))
DIL_GROUPS = 3
DIL_HEADS = 8
DIL_HEAD_DIM = 128
DIL_BLOCK = 128
DIL_UNITS_IN_FLIGHT = 8
HGRN_CHUNK = 16
HGRN_HEADS = 16
RWKV_HEAD_DIM = 64
RWKV_HEADS = 32
RWKV_CHUNK = 64
RWKV_GN_EPS = 64e-5
XATTN_HEADS = 4
XATTN_HEAD_DIM = 512

V7X_LANES = 128
V7X_SUBLANES = 8
V7X_VMEM_BYTES = 64 * 2**20
V7X_VMEM_CAP = 56 * 2**20


def _cparams(sem, vmem_bytes):
    limit = min(int(vmem_bytes * 1.25) + (4 << 20), V7X_VMEM_CAP)
    return pltpu.CompilerParams(dimension_semantics=sem, vmem_limit_bytes=limit)


def _rms(x, gain):
    return x * lax.rsqrt(jnp.mean(x * x, axis=-1, keepdims=True) + NORM_EPS) * gain


def _dot(a, b):
    return jnp.dot(a, b, preferred_element_type=F32)


def _dot_nt(a, b):
    return lax.dot_general(a, b, (((1,), (1,)), ((), ())), preferred_element_type=F32)


def _dot_tn(a, b):
    return lax.dot_general(a, b, (((0,), (0,)), ((), ())), preferred_element_type=F32)


def _wshape(w):
    arr, lead = w
    return arr.shape[len(lead):]


def _wspec(w, block, tail):
    lead = tuple(w[1])
    return pl.BlockSpec((None,) * len(lead) + tuple(block), lambda *g: lead + tuple(tail(*g)))


def _chunk_sums(x, chunk):
    t, w = x.shape
    rows = lax.broadcasted_iota(jnp.int32, (t, t), 0)
    cols = lax.broadcasted_iota(jnp.int32, (t, t), 1)
    same = _chunk_of(rows, chunk) == _chunk_of(cols, chunk)
    tri = same & (cols <= rows)
    sel = jnp.concatenate([tri.astype(BF16), same.astype(BF16)], axis=0)
    hi = x.astype(BF16)
    r1 = x - hi.astype(F32)
    mid = r1.astype(BF16)
    lo = (r1 - mid.astype(F32)).astype(BF16)
    s = _dot(sel, jnp.concatenate([hi, mid, lo], axis=1))
    s = s[:, :w] + s[:, w:2 * w] + s[:, 2 * w:]
    return s[:t], s[t:], tri


def _chunk_of(idx, chunk):
    return jnp.right_shift(idx, chunk.bit_length() - 1)


def _silu(x):
    return x * jax.nn.sigmoid(x)


def _norm_matmul_body(x_ref, g_ref, w_ref, o_ref, h_scr):
    @pl.when(pl.program_id(1) == 0)
    def _():
        h_scr[...] = _rms(x_ref[...], g_ref[...]).astype(BF16)

    o_ref[...] = _dot(h_scr[...], w_ref[...]).astype(o_ref.dtype)


def norm_matmul(x, gain, w, *, tm=1024, tn=1024, out_dtype=F32):
    M, K = x.shape
    N = _wshape(w)[1]
    tm, tn = min(tm, M), min(tn, N)
    ob = jnp.dtype(out_dtype).itemsize
    vmem = 2 * tm * K * 4 + tm * K * 2 + 2 * K * tn * 2 + 2 * tm * tn * ob
    return pl.pallas_call(
        _norm_matmul_body,
        grid=(M // tm, N // tn),
        in_specs=[
            pl.BlockSpec((tm, K), lambda i, j: (i, 0)),
            pl.BlockSpec((1, K), lambda i, j: (0, 0)),
            _wspec(w, (K, tn), lambda i, j: (0, j)),
        ],
        out_specs=pl.BlockSpec((tm, tn), lambda i, j: (i, j)),
        out_shape=jax.ShapeDtypeStruct((M, N), out_dtype),
        scratch_shapes=[pltpu.VMEM((tm, K), BF16)],
        compiler_params=_cparams(("parallel", "arbitrary"), vmem),
        name="norm_matmul",
    )(x, gain.reshape(1, K), w[0])


def _ffn_body(x_ref, g_ref, wg_ref, wu_ref, wd_ref, o_ref, h_scr):
    @pl.when(pl.program_id(1) == 0)
    def _():
        x = x_ref[...]
        h_scr[...] = _rms(x, g_ref[...]).astype(BF16)
        o_ref[...] = x

    h = h_scr[...]
    act = _silu(_dot(h, wg_ref[...])) * _dot(h, wu_ref[...])
    o_ref[...] += 0.5 * _dot(act.astype(BF16), wd_ref[...])


def ffn_half(x, gain, wg, wu, wd, *, tm=1024, tf=512):
    M, D = x.shape
    F = _wshape(wg)[1]
    tm = min(tm, M)
    vmem = 4 * tm * D * 4 + tm * D * 2 + 2 * 3 * D * tf * 2 + 3 * tm * tf * 4
    return pl.pallas_call(
        _ffn_body,
        grid=(M // tm, F // tf),
        in_specs=[
            pl.BlockSpec((tm, D), lambda i, f: (i, 0)),
            pl.BlockSpec((1, D), lambda i, f: (0, 0)),
            _wspec(wg, (D, tf), lambda i, f: (0, f)),
            _wspec(wu, (D, tf), lambda i, f: (0, f)),
            _wspec(wd, (tf, D), lambda i, f: (f, 0)),
        ],
        out_specs=pl.BlockSpec((tm, D), lambda i, f: (i, 0)),
        out_shape=jax.ShapeDtypeStruct((M, D), F32),
        scratch_shapes=[pltpu.VMEM((tm, D), BF16)],
        compiler_params=_cparams(("parallel", "arbitrary"), vmem),
        name="ffn_half",
    )(x, gain.reshape(1, D), wg[0], wu[0], wd[0])


def _matmul_res_body(a_ref, w_ref, r_ref, o_ref):
    o_ref[...] = r_ref[...] + _dot(a_ref[...], w_ref[...])


def matmul_residual(a, w, res, *, tm=1024, tn=1024):
    M, K = a.shape
    N = _wshape(w)[1]
    tm = min(tm, M)
    vmem = 2 * tm * K * 2 + 2 * K * tn * 2 + 4 * tm * tn * 4
    return pl.pallas_call(
        _matmul_res_body,
        grid=(M // tm, N // tn),
        in_specs=[
            pl.BlockSpec((tm, K), lambda i, j: (i, 0)),
            _wspec(w, (K, tn), lambda i, j: (0, j)),
            pl.BlockSpec((tm, tn), lambda i, j: (i, j)),
        ],
        out_specs=pl.BlockSpec((tm, tn), lambda i, j: (i, j)),
        out_shape=jax.ShapeDtypeStruct((M, N), F32),
        compiler_params=_cparams(("parallel", "arbitrary"), vmem),
        name="matmul_residual",
    )(a, w[0], res)


def _prologue_matmul_res_body(prologue, n_in, *refs):
    in_refs = refs[:n_in]
    w_ref, r_ref, o_ref, lhs_scr = refs[n_in:]
    row_tile = pl.program_id(0)

    @pl.when(pl.program_id(1) == 0)
    def _():
        prologue(row_tile, *in_refs, lhs_scr)

    o_ref[...] = r_ref[...] + _dot(lhs_scr[...], w_ref[...])


def prologue_matmul_residual(prologue, inputs, in_specs, w, res, *, tm, tn, in_vmem, name):
    M, N = res.shape
    K = _wshape(w)[0]
    vmem = in_vmem + tm * K * 2 + 2 * K * tn * 2 + 4 * tm * tn * 4
    return pl.pallas_call(
        functools.partial(_prologue_matmul_res_body, prologue, len(inputs)),
        grid=(M // tm, N // tn),
        in_specs=list(in_specs) + [
            _wspec(w, (K, tn), lambda i, j: (0, j)),
            pl.BlockSpec((tm, tn), lambda i, j: (i, j)),
        ],
        out_specs=pl.BlockSpec((tm, tn), lambda i, j: (i, j)),
        out_shape=jax.ShapeDtypeStruct((M, N), F32),
        scratch_shapes=[pltpu.VMEM((tm, K), BF16)],
        compiler_params=_cparams(("parallel", "arbitrary"), vmem),
        name=name,
    )(*inputs, w[0], res)


def _conv_prologue(tiles_per_seq, row_tile, b_ref, c_ref, u_ref, cp_ref, up_ref, cw_ref, lhs_scr):
    cu = c_ref[...] * u_ref[...]
    prev = cp_ref[...] * up_ref[...]
    first = (row_tile % tiles_per_seq) == 0
    prev = jnp.where(first, 0.0, prev)
    p1, p2 = prev[7:8, :], prev[6:7, :]
    row = lax.broadcasted_iota(jnp.int32, cu.shape, 0)
    s1 = jnp.where(row == 0, p1, pltpu.roll(cu, 1, 0))
    s2 = jnp.where(row == 0, p2, jnp.where(row == 1, p1, pltpu.roll(cu, 2, 0)))
    w = cw_ref[...]
    y = w[0:1, :] * s2 + w[1:2, :] * s1 + w[2:3, :] * cu
    lhs_scr[...] = (b_ref[...] * y).astype(BF16)


def conv_mixer(x, S, gain, w_in, conv_w, w_out, *, tm=512):
    T, D = x.shape
    tm = min(tm, S)
    bcu = norm_matmul(x, gain, w_in)
    r8 = tm // V7X_SUBLANES

    def prev_map(col):
        return lambda i, j: (jnp.maximum(i * r8 - 1, 0), col)

    in_specs = [
        pl.BlockSpec((tm, D), lambda i, j: (i, 0)),
        pl.BlockSpec((tm, D), lambda i, j: (i, 1)),
        pl.BlockSpec((tm, D), lambda i, j: (i, 2)),
        pl.BlockSpec((V7X_SUBLANES, D), prev_map(1)),
        pl.BlockSpec((V7X_SUBLANES, D), prev_map(2)),
        pl.BlockSpec((3, D), lambda i, j: (0, 0)),
    ]
    return prologue_matmul_residual(
        functools.partial(_conv_prologue, S // tm), (bcu, bcu, bcu, bcu, bcu, conv_w), in_specs, w_out, x,
        tm=tm, tn=1024, in_vmem=2 * 3 * tm * D * 4 + 4 * V7X_SUBLANES * D * 4, name="conv_mixer_out")


def _xattn_body(x_ref, kv_ref, xg_ref, qg_ref, kg_ref, wq_ref, wo_ref, o_ref, attn_scr):
    scale = XATTN_HEAD_DIM ** -0.5
    D = XATTN_HEADS * XATTN_HEAD_DIM
    x = x_ref[...]
    q = _dot(_rms(x, xg_ref[...]).astype(BF16), wq_ref[...])
    for h in range(XATTN_HEADS):
        sl = slice(h * XATTN_HEAD_DIM, (h + 1) * XATTN_HEAD_DIM)
        qn = _rms(q[:, sl], qg_ref[...]).astype(BF16)
        kn = _rms(kv_ref[:, sl], kg_ref[...]).astype(BF16)
        v = kv_ref[:, D + h * XATTN_HEAD_DIM:D + (h + 1) * XATTN_HEAD_DIM].astype(BF16)
        s = _dot_nt(qn, kn) * scale
        p = jnp.exp(s - jnp.max(s, axis=-1, keepdims=True))
        l = jnp.sum(p, axis=-1, keepdims=True)
        attn_scr[:, sl] = (_dot(p.astype(BF16), v) / l).astype(BF16)
    o_ref[...] = x + _dot(attn_scr[...], wo_ref[...])


def cross_attention(x, S, mem, xgain, mgain, wq, wkv, wo, q_gain, k_gain, *, tm=512):
    T, D = x.shape
    tm = min(tm, S)
    kv = norm_matmul(mem, mgain, wkv)
    tps = S // tm
    once = pl.Buffered(1)
    vmem = 4 * tm * D * 4 + 2 * MEM_LEN * 2 * D * 4 + 2 * D * D * 2 + tm * D * 2 + 3 * tm * D * 4
    return pl.pallas_call(
        _xattn_body,
        grid=(T // tm,),
        in_specs=[
            pl.BlockSpec((tm, D), lambda i: (i, 0)),
            pl.BlockSpec((MEM_LEN, 2 * D), lambda i: (i // tps, 0)),
            pl.BlockSpec((1, D), lambda i: (0, 0)),
            pl.BlockSpec((1, XATTN_HEAD_DIM), lambda i: (0, 0)),
            pl.BlockSpec((1, XATTN_HEAD_DIM), lambda i: (0, 0)),
            pl.BlockSpec((None,) * len(wq[1]) + (D, D), lambda i: tuple(wq[1]) + (0, 0), pipeline_mode=once),
            pl.BlockSpec((None,) * len(wo[1]) + (D, D), lambda i: tuple(wo[1]) + (0, 0), pipeline_mode=once),
        ],
        out_specs=pl.BlockSpec((tm, D), lambda i: (i, 0)),
        out_shape=jax.ShapeDtypeStruct((T, D), F32),
        scratch_shapes=[pltpu.VMEM((tm, D), BF16)],
        compiler_params=_cparams(("parallel",), vmem),
        name="xattn",
    )(x, kv, xgain.reshape(1, D), q_gain.reshape(1, -1), k_gain.reshape(1, -1), wq[0], wo[0])


def _dil_prep_body(x_ref, pos_ref, invf_ref, qg_ref, kg_ref, o_ref):
    ang = pos_ref[...] * invf_ref[...]
    lane = lax.broadcasted_iota(jnp.int32, ang.shape, 1)
    half = DIL_HEAD_DIM // 8
    cos, sin = jnp.cos(ang), jnp.sin(ang)
    sin_lo = jnp.where(lane < half, -sin, 0.0)
    sin_hi = jnp.where((lane >= half) & (lane < 2 * half), sin, 0.0)
    r = lax.broadcasted_iota(jnp.int32, (2 * DIL_HEAD_DIM, 2 * DIL_HEAD_DIM), 0)
    c = lax.broadcasted_iota(jnp.int32, (2 * DIL_HEAD_DIM, 2 * DIL_HEAD_DIM), 1)
    mean_mat = jnp.where((r < DIL_HEAD_DIM) == (c < DIL_HEAD_DIM), 1.0 / DIL_HEAD_DIM, 0.0).astype(BF16)
    for part, g_ref in ((0, qg_ref), (1, kg_ref)):
        for g in range(DIL_GROUPS):
            gain = g_ref[g:g + 1, :]
            for pair in range(DIL_HEADS // 2):
                col = ((part * DIL_GROUPS + g) * DIL_HEADS + 2 * pair) * DIL_HEAD_DIM
                x2 = x_ref[:, col:col + 2 * DIL_HEAD_DIM]
                inv = lax.rsqrt(_dot((x2 * x2).astype(BF16), mean_mat) + NORM_EPS)
                for hh in range(2):
                    sl = slice(hh * DIL_HEAD_DIM, (hh + 1) * DIL_HEAD_DIM)
                    xn = x2[:, sl] * inv[:, sl] * gain
                    o_ref[:, col + sl.start:col + sl.stop] = (
                        xn * cos + pltpu.roll(xn, DIL_HEAD_DIM - half, 1) * sin_lo
                        + pltpu.roll(xn, half, 1) * sin_hi)


def _dil_attn_body(*refs, n_chunks):
    ins, o_ref, scr = refs[:15], refs[15], refs[16:]
    c = pl.program_id(1)
    scale = DIL_HEAD_DIM ** -0.5
    ii = lax.broadcasted_iota(jnp.int32, (DIL_BLOCK, DIL_BLOCK), 0)
    jj = lax.broadcasted_iota(jnp.int32, (DIL_BLOCK, DIL_BLOCK), 1)
    cur_mask = jj <= ii
    prev_mask = jj >= ii
    ch = o_ref.shape[0]
    for g, (_, dil) in enumerate(DIL_PATTERNS):
        q_ref, k_ref, v_ref, kh_ref, vh_ref = ins[5 * g:5 * g + 5]
        kf, vf, og, lg = scr[4 * g:4 * g + 4]
        hist = DIL_BLOCK * dil
        kf[0:hist, :] = kh_ref[...]
        vf[0:hist, :] = vh_ref[...]
        kf[hist:hist + ch, :] = k_ref[...]
        vf[hist:hist + ch, :] = v_ref[...]
        def rows(start):
            return pl.ds(start, DIL_BLOCK, stride=dil) if dil > 1 else pl.ds(start, DIL_BLOCK)

        units = [(blk, blk * hist + r) for blk in range(ch // hist) for r in range(dil)]
        for b0 in range(0, len(units), DIL_UNITS_IN_FLIGHT):
            batch = units[b0:b0 + DIL_UNITS_IN_FLIGHT]
            qv = [q_ref[rows(q0), :].astype(BF16) for _, q0 in batch]
            sc = [_dot_nt(q, kf[rows(hist + q0), :].astype(BF16)) for q, (_, q0) in zip(qv, batch)]
            sp = [_dot_nt(q, kf[rows(q0), :].astype(BF16)) for q, (_, q0) in zip(qv, batch)]
            sc = [jnp.where(cur_mask, s * scale, NEG_INF) for s in sc]
            sp = [jnp.where(prev_mask if blk > 0 else prev_mask & (c > 0), s * scale, NEG_INF)
                  for s, (blk, _) in zip(sp, batch)]
            m = [jnp.maximum(jnp.max(a, -1, keepdims=True), jnp.max(b, -1, keepdims=True)) for a, b in zip(sc, sp)]
            pc = [jnp.exp(a - mm) for a, mm in zip(sc, m)]
            pp = [jnp.exp(b - mm) for b, mm in zip(sp, m)]
            l = [jnp.sum(a, -1, keepdims=True) + jnp.sum(b, -1, keepdims=True) for a, b in zip(pc, pp)]
            o = [_dot(a.astype(BF16), vf[rows(hist + q0), :].astype(BF16))
                 + _dot(b.astype(BF16), vf[rows(q0), :].astype(BF16)) for a, b, (_, q0) in zip(pc, pp, batch)]
            for oo, ll, mm, (_, q0) in zip(o, l, m, batch):
                og[rows(q0), :] = oo / ll
                lg[rows(q0), :] = jnp.broadcast_to(mm + jnp.log(ll), (DIL_BLOCK, DIL_HEAD_DIM))
    l0, l1, l2 = scr[3][...], scr[7][...], scr[11][...]
    mx = jnp.maximum(jnp.maximum(l0, l1), l2)
    e0, e1, e2 = jnp.exp(l0 - mx), jnp.exp(l1 - mx), jnp.exp(l2 - mx)
    o_ref[...] = ((e0 * scr[2][...] + e1 * scr[6][...] + e2 * scr[10][...]) / (e0 + e1 + e2)).astype(o_ref.dtype)


def dilated_mixer(x, B, S, positions, gain, w_qkv, q_gain, k_gain, w_out):
    T, D = x.shape
    nh = DIL_GROUPS * DIL_HEADS
    rot = DIL_HEAD_DIM // 4
    inv_freq = ROPE_THETA ** (-jnp.arange(0, rot, 2, dtype=F32) / rot)
    invf = jnp.concatenate([inv_freq, inv_freq, jnp.zeros((DIL_HEAD_DIM - rot,), F32)]).reshape(1, DIL_HEAD_DIM)
    pos = positions.astype(F32).reshape(T, 1)
    qkv = norm_matmul(x, gain, w_qkv)
    tp = 256
    qk_cols = 2 * nh * DIL_HEAD_DIM
    qk = pl.pallas_call(
        _dil_prep_body,
        grid=(T // tp,),
        in_specs=[
            pl.BlockSpec((tp, qk_cols), lambda i: (i, 0)),
            pl.BlockSpec((tp, 1), lambda i: (i, 0)),
            pl.BlockSpec((1, DIL_HEAD_DIM), lambda i: (0, 0)),
            pl.BlockSpec((DIL_GROUPS, DIL_HEAD_DIM), lambda i: (0, 0)),
            pl.BlockSpec((DIL_GROUPS, DIL_HEAD_DIM), lambda i: (0, 0)),
        ],
        out_specs=pl.BlockSpec((tp, qk_cols), lambda i: (i, 0)),
        out_shape=jax.ShapeDtypeStruct((T, qk_cols), F32),
        compiler_params=_cparams(("parallel",), 4 * tp * qk_cols * 4 + 2 * tp * V7X_LANES * 4),
        name="dil_qk_prep",
    )(qkv, pos, invf, q_gain, k_gain)

    ch = DIL_BLOCK * DIL_PATTERNS[-1][1]
    n_chunks = S // ch
    inputs, in_specs, scratch = [], [], []
    vmem = 2 * ch * DIL_HEAD_DIM * 2
    for g, (_, dil) in enumerate(DIL_PATTERNS):
        hist = DIL_BLOCK * dil
        per = ch // hist

        def cur_map(col):
            return lambda b, c, h: (b * n_chunks + c, col + h)

        def hist_map(col, per=per):
            return lambda b, c, h: (jnp.maximum((b * n_chunks + c) * per - 1, 0), col + h)

        inputs += [qk, qk, qkv, qk, qkv]
        in_specs += [
            pl.BlockSpec((ch, DIL_HEAD_DIM), cur_map(g * DIL_HEADS)),
            pl.BlockSpec((ch, DIL_HEAD_DIM), cur_map(nh + g * DIL_HEADS)),
            pl.BlockSpec((ch, DIL_HEAD_DIM), cur_map(2 * nh + g * DIL_HEADS)),
            pl.BlockSpec((hist, DIL_HEAD_DIM), hist_map(nh + g * DIL_HEADS)),
            pl.BlockSpec((hist, DIL_HEAD_DIM), hist_map(2 * nh + g * DIL_HEADS)),
        ]
        scratch += [pltpu.VMEM((hist + ch, DIL_HEAD_DIM), F32), pltpu.VMEM((hist + ch, DIL_HEAD_DIM), F32),
                    pltpu.VMEM((ch, DIL_HEAD_DIM), F32), pltpu.VMEM((ch, DIL_HEAD_DIM), F32)]
        vmem += (2 * (3 * ch + 2 * hist) + 2 * (hist + ch) + 2 * ch) * DIL_HEAD_DIM * 4
    o = pl.pallas_call(
        functools.partial(_dil_attn_body, n_chunks=n_chunks),
        grid=(B, n_chunks, DIL_HEADS),
        in_specs=in_specs,
        out_specs=pl.BlockSpec((ch, DIL_HEAD_DIM), lambda b, c, h: (b * n_chunks + c, h)),
        out_shape=jax.ShapeDtypeStruct((T, DIL_HEADS * DIL_HEAD_DIM), BF16),
        scratch_shapes=scratch,
        compiler_params=_cparams(("parallel", "arbitrary", "arbitrary"), vmem),
        name="dil_attention",
    )(*inputs)
    return matmul_residual(o, w_out, x)


def _hgrn_body(q_ref, f_ref, i_ref, gt_ref, lbl_ref, gain_ref, o_ref, st_scr, *, layer):
    @pl.when(pl.program_id(2) == 0)
    def _():
        st_scr[...] = jnp.zeros_like(st_scr)

    tt = q_ref.shape[0]
    dh = gain_ref.shape[1]
    heads = [slice(h * dh, (h + 1) * dh) for h in range(q_ref.shape[1] // dh)]
    lbl = lbl_ref[...]
    e = jnp.exp(lbl - jnp.max(lbl, axis=0, keepdims=True))
    p = e / jnp.sum(e, axis=0, keepdims=True)
    lb = jnp.sum(p[1:layer + 1, :], axis=0, keepdims=True)
    forget = lb + (1.0 - lb) * jax.nn.sigmoid(f_ref[...])
    k = 1.0 - forget
    gl = jnp.log(forget)
    a_cum, a_tot, tri = _chunk_sums(gl, HGRN_CHUNK)
    q_dec = (q_ref[...] * jnp.exp(a_cum)).astype(BF16)
    k_in = (k * jnp.exp(-a_cum)).astype(BF16)
    k_end = (k * jnp.exp(a_tot - a_cum)).astype(BF16)
    v = i_ref[...].astype(BF16)
    dec = jnp.exp(a_tot)
    att = [jnp.where(tri, _dot_nt(q_dec[:, hs], k_in[:, hs]), 0.0).astype(BF16) for hs in heads]
    o = [_dot(a, v[:, hs]) for a, hs in zip(att, heads)]
    chunks = [slice(c * HGRN_CHUNK, (c + 1) * HGRN_CHUNK) for c in range(tt // HGRN_CHUNK)]
    upd = [[_dot_tn(v[sl, hs], k_end[sl, hs]) for sl in chunks] for hs in heads]
    st = [st_scr[h] for h in range(len(heads))]
    inter = [[] for _ in heads]
    for ci, sl in enumerate(chunks):
        for h, hs in enumerate(heads):
            inter[h].append(_dot_nt(q_dec[sl, hs], st[h].astype(BF16)))
            st[h] = st[h] * dec[sl.start:sl.start + 1, hs] + upd[h][ci]
    gt = gt_ref[...]
    for h, hs in enumerate(heads):
        st_scr[h] = st[h]
        oh = o[h] + jnp.concatenate(inter[h], axis=0)
        o_ref[:, hs] = (_rms(oh, gain_ref[...]) * _silu(gt[:, hs])).astype(o_ref.dtype)


def hgrn_mixer(x, B, S, layer, gain, w_in, lb_logits, norm_gain, w_out, *, tt=256, heads_per_step=8):
    T, D = x.shape
    dh = D // HGRN_HEADS
    proj = norm_matmul(x, gain, w_in)
    tt = min(tt, S)
    nt = S // tt
    ng = HGRN_HEADS // heads_per_step
    wd = heads_per_step * dh

    def part(pidx):
        return pl.BlockSpec((tt, wd), lambda b, h, s: (b * nt + s, pidx * ng + h))

    o = pl.pallas_call(
        functools.partial(_hgrn_body, layer=layer),
        grid=(B, ng, nt),
        in_specs=[part(0), part(1), part(2), part(3),
                  pl.BlockSpec((DEPTH, wd), lambda b, h, s: (0, h)),
                  pl.BlockSpec((1, dh), lambda b, h, s: (0, 0))],
        out_specs=pl.BlockSpec((tt, wd), lambda b, h, s: (b * nt + s, h)),
        out_shape=jax.ShapeDtypeStruct((T, D), BF16),
        scratch_shapes=[pltpu.VMEM((heads_per_step, dh, dh), F32)],
        compiler_params=_cparams(("parallel", "parallel", "arbitrary"), 24 * tt * wd * 4 + 8 * tt * tt * 4),
        name="hgrn_core",
    )(proj, proj, proj, proj, lb_logits, norm_gain.reshape(1, dh))
    return matmul_residual(o, w_out, x)


def _rwkv_shift_mix(x_ref, xp_ref, gn_ref, first):
    gn = gn_ref[...]
    h = _rms(x_ref[...], gn)
    last = _rms(xp_ref[...], gn)[V7X_SUBLANES - 1:V7X_SUBLANES, :]
    last = jnp.where(first, 0.0, last)
    row = lax.broadcasted_iota(jnp.int32, h.shape, 0)
    return h, jnp.where(row == 0, last, pltpu.roll(h, 1, 0)) - h


def _rwkv_rkv_body(x_ref, xp_ref, gn_ref, mu_ref, wrkv_ref, rkv_ref, mix_scr, *, tiles_per_seq, n_col_tiles):
    n = pl.program_id(1)
    first = (pl.program_id(0) % tiles_per_seq) == 0

    @pl.when(n == 0)
    def _():
        gn, mu = gn_ref[...], mu_ref[...]
        last = jnp.where(first, 0.0, _rms(xp_ref[...], gn)[V7X_SUBLANES - 1:V7X_SUBLANES, :])
        sub = min(256, x_ref.shape[0])
        for r0 in range(0, x_ref.shape[0], sub):
            h = _rms(x_ref[r0:r0 + sub, :], gn)
            row = lax.broadcasted_iota(jnp.int32, h.shape, 0)
            d = jnp.where(row == 0, last, pltpu.roll(h, 1, 0)) - h
            for m in range(3):
                mix_scr[m, r0:r0 + sub, :] = (h + d * mu[m:m + 1, :]).astype(BF16)
            last = h[sub - 1:sub, :]

    rkv_ref[...] = _dot(mix_scr[n // n_col_tiles], wrkv_ref[...])


def _rwkv_lora_body(x_ref, xp_ref, gn_ref, mu_ref, w0_ref, w1_ref, w2_ref, a0_ref, a1_ref, a2_ref, g1_ref, g2_ref,
                    lw_ref, a_ref, g_ref, *, tiles_per_seq):
    h, d = _rwkv_shift_mix(x_ref, xp_ref, gn_ref, (pl.program_id(0) % tiles_per_seq) == 0)
    mu = mu_ref[...]
    xw = (h + d * mu[3:4, :]).astype(BF16)
    xa = (h + d * mu[4:5, :]).astype(BF16)
    xg = (h + d * mu[5:6, :]).astype(BF16)
    z = -(w0_ref[...] + _dot(jnp.tanh(_dot(xw, w1_ref[...])).astype(BF16), w2_ref[...]))
    softplus = jnp.maximum(z, 0.0) + jnp.log1p(jnp.exp(-jnp.abs(z)))
    lw_ref[...] = -jnp.exp(-softplus - 0.5)
    a_ref[...] = jax.nn.sigmoid(a0_ref[...] + _dot(_dot(xa, a1_ref[...]).astype(BF16), a2_ref[...]))
    g_ref[...] = _dot(jax.nn.sigmoid(_dot(xg, g1_ref[...])).astype(BF16), g2_ref[...])


def _rwkv_core_body(r_ref, k_ref, v_ref, lw_ref, a_ref, g_ref, kk_ref, ka_ref, rk_ref, lnw_ref, lnb_ref,
                    o_ref, h_scr):
    @pl.when(pl.program_id(2) == 0)
    def _():
        h_scr[...] = jnp.zeros_like(h_scr)

    tt = r_ref.shape[0]
    C, N = RWKV_CHUNK, RWKV_HEAD_DIM
    lw = lw_ref[...]
    g_cum, g_tot, _ = _chunk_sums(lw, C)
    r, k, v, a = r_ref[...], k_ref[...], v_ref[...], a_ref[...]
    pairs = [slice(p * 2 * N, (p + 1) * 2 * N) for p in range(r.shape[1] // (2 * N))]
    left = lax.broadcasted_iota(jnp.int32, (tt, 2 * N), 1) < N

    def head_sum(t):
        return jnp.concatenate(
            [jnp.where(left, jnp.sum(jnp.where(left, t[:, ps], 0.0), -1, keepdims=True),
                       jnp.sum(jnp.where(left, 0.0, t[:, ps]), -1, keepdims=True)) for ps in pairs], axis=1)

    kk = k * kk_ref[...]
    kk = kk * lax.rsqrt(jnp.maximum(head_sum(kk * kk), 1e-24))
    kmod = k * (1.0 + (a - 1.0) * ka_ref[...])
    bv = kk * a
    e_neg = jnp.exp(-g_cum)
    e_end = jnp.exp(g_tot - g_cum)
    a_t = (-kk) * jnp.exp(g_cum - lw)
    r_t = r * jnp.exp(g_cum)
    k_t, b_t = kmod * e_neg, bv * e_neg
    k_h, b_h = kmod * e_end, bv * e_end
    dec = jnp.exp(g_tot)
    def key_lanes(t):
        out = []
        for ps in pairs:
            out += [jnp.where(left, t[:, ps], 0.0), jnp.where(left, pltpu.roll(t[:, ps], N, 1), 0.0)]
        return out

    a_k, r_k, bt_k, kt_k, bh_k, kh_k, dec_k = map(key_lanes, (a_t, r_t, b_t, k_t, b_h, k_h, dec))
    v_v = []
    for ps in pairs:
        v_v += [jnp.where(left, 0.0, pltpu.roll(v[:, ps], N, 1)), jnp.where(left, 0.0, v[:, ps])]
    n_heads = 2 * len(pairs)
    i2 = lax.broadcasted_iota(jnp.int32, (C, 2 * C), 0)
    lane2 = lax.broadcasted_iota(jnp.int32, (C, 2 * C), 1)
    lo = lane2 < C
    t2 = jnp.bitwise_and(lane2, C - 1)
    strict, incl = t2 < i2, t2 <= i2
    eye_hi = jnp.where(lane2 == i2 + C, 1.0, 0.0)
    zeros16 = jnp.zeros((C, 2 * C), BF16)

    nc = tt // C
    units = [(hh, c) for hh in range(n_heads) for c in range(nc)]

    def cut(per_head, u):
        return per_head[u[0]][u[1] * C:(u[1] + 1) * C, :]

    v_c = [cut(v_v, u).astype(BF16) for u in units]
    prod = [_dot_nt(jnp.concatenate([cut(a_k, u), cut(r_k, u)], axis=0).astype(BF16),
                    jnp.concatenate([cut(bt_k, u), cut(kt_k, u)], axis=0).astype(BF16))
            for u in units]
    n_abk = [jnp.where(strict, p[:C], 0.0) for p in prod]
    t_rbk = [jnp.where(incl, p[C:], 0.0).astype(BF16) for p in prod]
    n_lo = [jnp.where(lo, n, 0.0) for n in n_abk]
    s1 = [_dot(n.astype(BF16), jnp.concatenate([nl.astype(BF16), vv], axis=0))
          for n, nl, vv in zip(n_abk, n_lo, v_c)]
    z = [jnp.where(lo, s, 0.0) + pltpu.roll(nl, C, 1) + eye_hi for s, nl in zip(s1, n_lo)]
    for _ in range(5):
        z16 = [zz.astype(BF16) for zz in z]
        z = [_dot(zz16[:, :C], zz16) + jnp.where(lo, 0.0, zz) for zz, zz16 in zip(z, z16)]
    w0 = [(cut(a_k, u) + jnp.where(lo, 0.0, s)).astype(BF16) for u, s in zip(units, s1)]
    au = [_dot(zz.astype(BF16), jnp.concatenate([zeros16, w], axis=0)).astype(BF16)
          for zz, w in zip(z, w0)]
    auv = [jnp.concatenate([x, vv], axis=0) for x, vv in zip(au, v_c)]
    ry = [_dot(t, x) for t, x in zip(t_rbk, auv)]
    th = [_dot_tn(x, jnp.concatenate([cut(bh_k, u), cut(kh_k, u)], axis=0).astype(BF16))
          for x, u in zip(auv, units)]
    r_p = [(cut(r_k, u) + jnp.where(lo, y, 0.0)).astype(BF16) for u, y in zip(units, ry)]

    ht = [h_scr[hh] for hh in range(n_heads)]
    ys = [[] for _ in range(n_heads)]
    for c in range(nc):
        for hh in range(n_heads):
            i = hh * nc + c
            ht16 = ht[hh].astype(BF16)
            ys[hh].append(_dot_nt(r_p[i], ht16) + ry[i][:, C:])
            ht[hh] = (ht[hh] * dec_k[hh][c * C:c * C + 1, :] + _dot(ht16[:, :C], th[i][:C].astype(BF16))
                      + th[i][C:])
    y_heads = []
    for hh in range(n_heads):
        h_scr[hh] = ht[hh]
        y = jnp.concatenate(ys[hh], axis=0)
        mean = jnp.mean(y, -1, keepdims=True)
        var = jnp.mean(jnp.square(y - mean), -1, keepdims=True)
        y_heads.append((y - mean) * lax.rsqrt(var + RWKV_GN_EPS))
    yn = jnp.concatenate(y_heads, axis=1) * lnw_ref[...] + lnb_ref[...]
    bonus = head_sum(r * kmod * rk_ref[...]) * v
    o_ref[...] = ((yn + bonus) * g_ref[...]).astype(o_ref.dtype)


def rwkv_mixer(x, B, S, gain, mu, w_rkv, w0, w1, w2, a0, a1, a2, g1, g2, k_k, k_a, r_k, ln_w, ln_b, w_out,
               *, tm=1024, tn=1024, tl=256, tt=256, heads_per_step=4):
    T, D = x.shape
    tm, tl, tt = min(tm, S), min(tl, S), min(tt, S)
    nct = D // tn
    row = lambda i, n: (0, 0)

    def prev_rows(t):
        return lambda i, *_: (jnp.maximum(i * (t // V7X_SUBLANES) - 1, 0), 0)

    rkv = pl.pallas_call(
        functools.partial(_rwkv_rkv_body, tiles_per_seq=S // tm, n_col_tiles=nct),
        grid=(T // tm, 3 * nct),
        in_specs=[
            pl.BlockSpec((tm, D), lambda i, n: (i, 0)),
            pl.BlockSpec((V7X_SUBLANES, D), prev_rows(tm)),
            pl.BlockSpec((1, D), row),
            pl.BlockSpec((6, D), row),
            _wspec(w_rkv, (None, D, tn), lambda i, n: (n // nct, 0, n % nct)),
        ],
        out_specs=pl.BlockSpec((tm, tn), lambda i, n: (i, n)),
        out_shape=jax.ShapeDtypeStruct((T, 3 * D), F32),
        scratch_shapes=[pltpu.VMEM((3, tm, D), BF16)],
        compiler_params=_cparams(("parallel", "arbitrary"),
                                 2 * tm * D * 4 + 3 * tm * D * 2 + 2 * D * tn * 2 + 2 * tm * tn * 4 + 3 * tm * D * 4),
        name="rwkv_rkv",
    )(x, x, gain.reshape(1, D), mu, w_rkv[0])

    lora = w1.shape[1]
    pad = (-lora) % V7X_LANES
    w1p, a1p = jnp.pad(w1, ((0, 0), (0, pad))), jnp.pad(a1, ((0, 0), (0, pad)))
    w2p, a2p = jnp.pad(w2, ((0, pad), (0, 0))), jnp.pad(a2, ((0, pad), (0, 0)))
    lp, gl = lora + pad, g1.shape[1]
    one = lambda i: (0, 0)
    tok_l = pl.BlockSpec((tl, D), lambda i: (i, 0))
    lw, a, g = pl.pallas_call(
        functools.partial(_rwkv_lora_body, tiles_per_seq=S // tl),
        grid=(T // tl,),
        in_specs=[
            tok_l,
            pl.BlockSpec((V7X_SUBLANES, D), prev_rows(tl)),
            pl.BlockSpec((1, D), one),
            pl.BlockSpec((6, D), one),
            pl.BlockSpec((1, D), one), pl.BlockSpec((D, lp), one), pl.BlockSpec((lp, D), one),
            pl.BlockSpec((1, D), one), pl.BlockSpec((D, lp), one), pl.BlockSpec((lp, D), one),
            pl.BlockSpec((D, gl), one), pl.BlockSpec((gl, D), one),
        ],
        out_specs=[tok_l, tok_l, tok_l],
        out_shape=[jax.ShapeDtypeStruct((T, D), F32)] * 3,
        compiler_params=_cparams(("parallel",), 14 * tl * D * 4 + 4 * (2 * D * lp + D * gl) * 2),
        name="rwkv_lora",
    )(x, x, gain.reshape(1, D), mu, w0.reshape(1, D), w1p, w2p, a0.reshape(1, D), a1p, a2p, g1, g2)

    nt = S // tt
    pw = heads_per_step * RWKV_HEAD_DIM
    npair = D // pw

    def tok(col0):
        return pl.BlockSpec((tt, pw), lambda b, p, s: (b * nt + s, col0 + p))

    par = pl.BlockSpec((1, pw), lambda b, p, s: (0, p))
    o = pl.pallas_call(
        _rwkv_core_body,
        grid=(B, npair, nt),
        in_specs=[tok(0), tok(npair), tok(2 * npair), tok(0), tok(0), tok(0), par, par, par, par, par],
        out_specs=pl.BlockSpec((tt, pw), lambda b, p, s: (b * nt + s, p)),
        out_shape=jax.ShapeDtypeStruct((T, D), BF16),
        scratch_shapes=[pltpu.VMEM((heads_per_step, RWKV_HEAD_DIM, 2 * RWKV_HEAD_DIM), F32)],
        compiler_params=_cparams(("parallel", "parallel", "arbitrary"), 40 * tt * pw * 4 + 8 * tt * tt * 4),
        name="rwkv_core",
    )(rkv, rkv, rkv, lw, a, g, k_k.reshape(1, D), k_a.reshape(1, D), r_k.reshape(1, D),
      ln_w.reshape(1, D), ln_b.reshape(1, D))
    return matmul_residual(o, w_out, x)


def kernel(x, mem, positions, ffn_norm, ffn_w_gate, ffn_w_up, ffn_w_down, mix_norm, xattn_norm, mem_norm, xattn_wq, xattn_wkv, xattn_wo, xattn_q_gain, xattn_k_gain, conv_w_in, conv_w, conv_w_out, dil_w_qkv, dil_q_gain, dil_k_gain, dil_w_out, hgrn_w_in, hgrn_lb_logits, hgrn_norm, hgrn_w_out, rwkv_mu, rwkv_w_rkv, rwkv_w0, rwkv_w1, rwkv_w2, rwkv_a0, rwkv_a1, rwkv_a2, rwkv_g1, rwkv_g2, rwkv_k_k, rwkv_k_a, rwkv_r_k, rwkv_ln_w, rwkv_ln_b, rwkv_w_out):
    B, S, D = x.shape
    assert D == D_MODEL and S % (DIL_BLOCK * DIL_PATTERNS[-1][1]) == 0
    depth = ffn_norm.shape[0]
    xf = x.reshape(B * S, D)
    memf = mem.reshape(B * MEM_LEN, D)
    bf = lambda w: w.astype(BF16)
    wg_all, wu_all, wd_all = bf(ffn_w_gate), bf(ffn_w_up), bf(ffn_w_down)
    wq_all, wkv_all, wo_all = bf(xattn_wq), bf(xattn_wkv), bf(xattn_wo)
    conv_in_all, conv_out_all = bf(conv_w_in), bf(conv_w_out)
    dil_qkv_all, dil_out_all = bf(dil_w_qkv), bf(dil_w_out)
    hgrn_in_all, hgrn_out_all = bf(hgrn_w_in), bf(hgrn_w_out)
    rkv_all, rwkv_out_all = bf(rwkv_w_rkv), bf(rwkv_w_out)
    for i in range(depth):
        kind, j = i % N_MIXERS, i // N_MIXERS
        xf = ffn_half(xf, ffn_norm[i, 0], (wg_all, (i, 0)), (wu_all, (i, 0)), (wd_all, (i, 0)))
        if kind == 0:
            xf = conv_mixer(xf, S, mix_norm[i], (conv_in_all, (j,)), conv_w[j], (conv_out_all, (j,)))
        elif kind == 1:
            xf = dilated_mixer(xf, B, S, positions, mix_norm[i], (dil_qkv_all, (j,)), dil_q_gain[j], dil_k_gain[j],
                               (dil_out_all, (j,)))
        elif kind == 2:
            xf = hgrn_mixer(xf, B, S, i, mix_norm[i], (hgrn_in_all, (j,)), hgrn_lb_logits, hgrn_norm[j],
                            (hgrn_out_all, (j,)))
        else:
            xf = rwkv_mixer(xf, B, S, mix_norm[i], rwkv_mu[j], (rkv_all, (j,)), rwkv_w0[j], bf(rwkv_w1[j]),
                            bf(rwkv_w2[j]), rwkv_a0[j], bf(rwkv_a1[j]), bf(rwkv_a2[j]), bf(rwkv_g1[j]),
                            bf(rwkv_g2[j]), rwkv_k_k[j], rwkv_k_a[j], rwkv_r_k[j], rwkv_ln_w[j], rwkv_ln_b[j],
                            (rwkv_out_all, (j,)))
        xf = cross_attention(xf, S, memf, xattn_norm[i], mem_norm[i], (wq_all, (i,)), (wkv_all, (i,)),
                             (wo_all, (i,)), xattn_q_gain[i], xattn_k_gain[i])
        xf = ffn_half(xf, ffn_norm[i, 1], (wg_all, (i, 1)), (wu_all, (i, 1)), (wd_all, (i, 1)))
    return xf.reshape(B, S, D)
```

```python
import functools

import jax
import jax.numpy as jnp
from jax import lax
from jax.experimental import pallas as pl
from jax.experimental.pallas import tpu as pltpu

F32 = jnp.float32
BF16 = jnp.bfloat16

D_MODEL = 2048
DEPTH = 4
N_MIXERS = 4
MEM_LEN = 256
FFN_DIM = 5632
NORM_EPS = 1e-6
NEG_INF = -1e30
ROPE_THETA = 500000.0
DIL_PATTERNS = ((128, 1), (512, 4), (2048, 16))
DIL_GROUPS = 3
DIL_HEADS = 8
DIL_HEAD_DIM = 128
DIL_BLOCK = 128
DIL_UNITS_IN_FLIGHT = 8
HGRN_CHUNK = 16
HGRN_HEADS = 16
RWKV_HEAD_DIM = 64
RWKV_HEADS = 32
RWKV_CHUNK = 64
RWKV_GN_EPS = 64e-5
XATTN_HEADS = 4
XATTN_HEAD_DIM = 512

V7X_LANES = 128
V7X_SUBLANES = 8
V7X_VMEM_BYTES = 64 * 2**20
V7X_VMEM_CAP = 56 * 2**20


def _cparams(sem, vmem_bytes):
    limit = min(int(vmem_bytes * 1.25) + (4 << 20), V7X_VMEM_CAP)
    return pltpu.CompilerParams(dimension_semantics=sem, vmem_limit_bytes=limit)


def _rms(x, gain):
    return x * lax.rsqrt(jnp.mean(x * x, axis=-1, keepdims=True) + NORM_EPS) * gain


def _dot(a, b):
    return jnp.dot(a, b, preferred_element_type=F32)


def _dot_nt(a, b):
    return lax.dot_general(a, b, (((1,), (1,)), ((), ())), preferred_element_type=F32)


def _dot_tn(a, b):
    return lax.dot_general(a, b, (((0,), (0,)), ((), ())), preferred_element_type=F32)


def _wshape(w):
    arr, lead = w
    return arr.shape[len(lead):]


def _wspec(w, block, tail):
    lead = tuple(w[1])
    return pl.BlockSpec((None,) * len(lead) + tuple(block), lambda *g: lead + tuple(tail(*g)))


def _chunk_sums(x, chunk):
    t, w = x.shape
    rows = lax.broadcasted_iota(jnp.int32, (t, t), 0)
    cols = lax.broadcasted_iota(jnp.int32, (t, t), 1)
    same = _chunk_of(rows, chunk) == _chunk_of(cols, chunk)
    tri = same & (cols <= rows)
    sel = jnp.concatenate([tri.astype(BF16), same.astype(BF16)], axis=0)
    hi = x.astype(BF16)
    r1 = x - hi.astype(F32)
    mid = r1.astype(BF16)
    lo = (r1 - mid.astype(F32)).astype(BF16)
    s = _dot(sel, jnp.concatenate([hi, mid, lo], axis=1))
    s = s[:, :w] + s[:, w:2 * w] + s[:, 2 * w:]
    return s[:t], s[t:], tri


def _chunk_of(idx, chunk):
    return jnp.right_shift(idx, chunk.bit_length() - 1)


def _silu(x):
    return x * jax.nn.sigmoid(x)


def _norm_matmul_body(x_ref, g_ref, w_ref, o_ref, h_scr):
    @pl.when(pl.program_id(1) == 0)
    def _():
        h_scr[...] = _rms(x_ref[...], g_ref[...]).astype(BF16)

    o_ref[...] = _dot(h_scr[...], w_ref[...]).astype(o_ref.dtype)


def norm_matmul(x, gain, w, *, tm=1024, tn=1024, out_dtype=F32):
    M, K = x.shape
    N = _wshape(w)[1]
    tm, tn = min(tm, M), min(tn, N)
    ob = jnp.dtype(out_dtype).itemsize
    vmem = 2 * tm * K * 4 + tm * K * 2 + 2 * K * tn * 2 + 2 * tm * tn * ob
    return pl.pallas_call(
        _norm_matmul_body,
        grid=(M // tm, N // tn),
        in_specs=[
            pl.BlockSpec((tm, K), lambda i, j: (i, 0)),
            pl.BlockSpec((1, K), lambda i, j: (0, 0)),
            _wspec(w, (K, tn), lambda i, j: (0, j)),
        ],
        out_specs=pl.BlockSpec((tm, tn), lambda i, j: (i, j)),
        out_shape=jax.ShapeDtypeStruct((M, N), out_dtype),
        scratch_shapes=[pltpu.VMEM((tm, K), BF16)],
        compiler_params=_cparams(("parallel", "arbitrary"), vmem),
        name="norm_matmul",
    )(x, gain.reshape(1, K), w[0])


def _ffn_body(x_ref, g_ref, wg_ref, wu_ref, wd_ref, o_ref, h_scr):
    @pl.when(pl.program_id(1) == 0)
    def _():
        x = x_ref[...]
        h_scr[...] = _rms(x, g_ref[...]).astype(BF16)
        o_ref[...] = x

    h = h_scr[...]
    act = _silu(_dot(h, wg_ref[...])) * _dot(h, wu_ref[...])
    o_ref[...] += 0.5 * _dot(act.astype(BF16), wd_ref[...])


def ffn_half(x, gain, wg, wu, wd, *, tm=1024, tf=512):
    M, D = x.shape
    F = _wshape(wg)[1]
    tm = min(tm, M)
    vmem = 4 * tm * D * 4 + tm * D * 2 + 2 * 3 * D * tf * 2 + 3 * tm * tf * 4
    return pl.pallas_call(
        _ffn_body,
        grid=(M // tm, F // tf),
        in_specs=[
            pl.BlockSpec((tm, D), lambda i, f: (i, 0)),
            pl.BlockSpec((1, D), lambda i, f: (0, 0)),
            _wspec(wg, (D, tf), lambda i, f: (0, f)),
            _wspec(wu, (D, tf), lambda i, f: (0, f)),
            _wspec(wd, (tf, D), lambda i, f: (f, 0)),
        ],
        out_specs=pl.BlockSpec((tm, D), lambda i, f: (i, 0)),
        out_shape=jax.ShapeDtypeStruct((M, D), F32),
        scratch_shapes=[pltpu.VMEM((tm, D), BF16)],
        compiler_params=_cparams(("parallel", "arbitrary"), vmem),
        name="ffn_half",
    )(x, gain.reshape(1, D), wg[0], wu[0], wd[0])


def _matmul_res_body(a_ref, w_ref, r_ref, o_ref):
    o_ref[...] = r_ref[...] + _dot(a_ref[...], w_ref[...])


def matmul_residual(a, w, res, *, tm=1024, tn=1024):
    M, K = a.shape
    N = _wshape(w)[1]
    tm = min(tm, M)
    vmem = 2 * tm * K * 2 + 2 * K * tn * 2 + 4 * tm * tn * 4
    return pl.pallas_call(
        _matmul_res_body,
        grid=(M // tm, N // tn),
        in_specs=[
            pl.BlockSpec((tm, K), lambda i, j: (i, 0)),
            _wspec(w, (K, tn), lambda i, j: (0, j)),
            pl.BlockSpec((tm, tn), lambda i, j: (i, j)),
        ],
        out_specs=pl.BlockSpec((tm, tn), lambda i, j: (i, j)),
        out_shape=jax.ShapeDtypeStruct((M, N), F32),
        compiler_params=_cparams(("parallel", "arbitrary"), vmem),
        name="matmul_residual",
    )(a, w[0], res)


def _prologue_matmul_res_body(prologue, n_in, *refs):
    in_refs = refs[:n_in]
    w_ref, r_ref, o_ref, lhs_scr = refs[n_in:]
    row_tile = pl.program_id(0)

    @pl.when(pl.program_id(1) == 0)
    def _():
        prologue(row_tile, *in_refs, lhs_scr)

    o_ref[...] = r_ref[...] + _dot(lhs_scr[...], w_ref[...])


def prologue_matmul_residual(prologue, inputs, in_specs, w, res, *, tm, tn, in_vmem, name):
    M, N = res.shape
    K = _wshape(w)[0]
    vmem = in_vmem + tm * K * 2 + 2 * K * tn * 2 + 4 * tm * tn * 4
    return pl.pallas_call(
        functools.partial(_prologue_matmul_res_body, prologue, len(inputs)),
        grid=(M // tm, N // tn),
        in_specs=list(in_specs) + [
            _wspec(w, (K, tn), lambda i, j: (0, j)),
            pl.BlockSpec((tm, tn), lambda i, j: (i, j)),
        ],
        out_specs=pl.BlockSpec((tm, tn), lambda i, j: (i, j)),
        out_shape=jax.ShapeDtypeStruct((M, N), F32),
        scratch_shapes=[pltpu.VMEM((tm, K), BF16)],
        compiler_params=_cparams(("parallel", "arbitrary"), vmem),
        name=name,
    )(*inputs, w[0], res)


def _conv_prologue(tiles_per_seq, row_tile, b_ref, c_ref, u_ref, cp_ref, up_ref, cw_ref, lhs_scr):
    cu = c_ref[...].astype(F32) * u_ref[...].astype(F32)
    prev = cp_ref[...].astype(F32) * up_ref[...].astype(F32)
    first = (row_tile % tiles_per_seq) == 0
    prev = jnp.where(first, 0.0, prev)
    n_prev = prev.shape[0]
    p1, p2 = prev[n_prev - 1:n_prev, :], prev[n_prev - 2:n_prev - 1, :]
    row = lax.broadcasted_iota(jnp.int32, cu.shape, 0)
    s1 = jnp.where(row == 0, p1, pltpu.roll(cu, 1, 0))
    s2 = jnp.where(row == 0, p2, jnp.where(row == 1, p1, pltpu.roll(cu, 2, 0)))
    w = cw_ref[...]
    y = w[0:1, :] * s2 + w[1:2, :] * s1 + w[2:3, :] * cu
    lhs_scr[...] = (b_ref[...].astype(F32) * y).astype(BF16)


def conv_mixer(x, S, gain, w_in, conv_w, w_out, *, tm=512):
    T, D = x.shape
    tm = min(tm, S)
    bcu = norm_matmul(x, gain, w_in, out_dtype=BF16)
    halo = 2 * V7X_SUBLANES
    rh = tm // halo

    def prev_map(col):
        return lambda i, j: (jnp.maximum(i * rh - 1, 0), col)

    in_specs = [
        pl.BlockSpec((tm, D), lambda i, j: (i, 0)),
        pl.BlockSpec((tm, D), lambda i, j: (i, 1)),
        pl.BlockSpec((tm, D), lambda i, j: (i, 2)),
        pl.BlockSpec((halo, D), prev_map(1)),
        pl.BlockSpec((halo, D), prev_map(2)),
        pl.BlockSpec((3, D), lambda i, j: (0, 0)),
    ]
    return prologue_matmul_residual(
        functools.partial(_conv_prologue, S // tm), (bcu, bcu, bcu, bcu, bcu, conv_w), in_specs, w_out, x,
        tm=tm, tn=1024, in_vmem=2 * 3 * tm * D * 2 + 4 * halo * D * 2 + 3 * tm * D * 4, name="conv_mixer_out")


def _xattn_body(x_ref, kv_ref, xg_ref, qg_ref, kg_ref, wq_ref, wo_ref, o_ref, attn_scr):
    scale = XATTN_HEAD_DIM ** -0.5
    D = XATTN_HEADS * XATTN_HEAD_DIM
    x = x_ref[...]
    q = _dot(_rms(x, xg_ref[...]).astype(BF16), wq_ref[...])
    for h in range(XATTN_HEADS):
        sl = slice(h * XATTN_HEAD_DIM, (h + 1) * XATTN_HEAD_DIM)
        qn = _rms(q[:, sl], qg_ref[...]).astype(BF16)
        kn = _rms(kv_ref[:, sl], kg_ref[...]).astype(BF16)
        v = kv_ref[:, D + h * XATTN_HEAD_DIM:D + (h + 1) * XATTN_HEAD_DIM].astype(BF16)
        s = _dot_nt(qn, kn) * scale
        p = jnp.exp(s - jnp.max(s, axis=-1, keepdims=True))
        l = jnp.sum(p, axis=-1, keepdims=True)
        attn_scr[:, sl] = (_dot(p.astype(BF16), v) / l).astype(BF16)
    o_ref[...] = x + _dot(attn_scr[...], wo_ref[...])


def cross_attention(x, S, mem, xgain, mgain, wq, wkv, wo, q_gain, k_gain, *, tm=512):
    T, D = x.shape
    tm = min(tm, S)
    kv = norm_matmul(mem, mgain, wkv)
    tps = S // tm
    once = pl.Buffered(1)
    vmem = 4 * tm * D * 4 + 2 * MEM_LEN * 2 * D * 4 + 2 * D * D * 2 + tm * D * 2 + 3 * tm * D * 4
    return pl.pallas_call(
        _xattn_body,
        grid=(T // tm,),
        in_specs=[
            pl.BlockSpec((tm, D), lambda i: (i, 0)),
            pl.BlockSpec((MEM_LEN, 2 * D), lambda i: (i // tps, 0)),
            pl.BlockSpec((1, D), lambda i: (0, 0)),
            pl.BlockSpec((1, XATTN_HEAD_DIM), lambda i: (0, 0)),
            pl.BlockSpec((1, XATTN_HEAD_DIM), lambda i: (0, 0)),
            pl.BlockSpec((None,) * len(wq[1]) + (D, D), lambda i: tuple(wq[1]) + (0, 0), pipeline_mode=once),
            pl.BlockSpec((None,) * len(wo[1]) + (D, D), lambda i: tuple(wo[1]) + (0, 0), pipeline_mode=once),
        ],
        out_specs=pl.BlockSpec((tm, D), lambda i: (i, 0)),
        out_shape=jax.ShapeDtypeStruct((T, D), F32),
        scratch_shapes=[pltpu.VMEM((tm, D), BF16)],
        compiler_params=_cparams(("parallel",), vmem),
        name="xattn",
    )(x, kv, xgain.reshape(1, D), q_gain.reshape(1, -1), k_gain.reshape(1, -1), wq[0], wo[0])


def _dil_prep_body(x_ref, pos_ref, invf_ref, qg_ref, kg_ref, o_ref):
    ang = pos_ref[...] * invf_ref[...]
    lane = lax.broadcasted_iota(jnp.int32, ang.shape, 1)
    half = DIL_HEAD_DIM // 8
    cos, sin = jnp.cos(ang), jnp.sin(ang)
    sin_lo = jnp.where(lane < half, -sin, 0.0)
    sin_hi = jnp.where((lane >= half) & (lane < 2 * half), sin, 0.0)
    r = lax.broadcasted_iota(jnp.int32, (2 * DIL_HEAD_DIM, 2 * DIL_HEAD_DIM), 0)
    c = lax.broadcasted_iota(jnp.int32, (2 * DIL_HEAD_DIM, 2 * DIL_HEAD_DIM), 1)
    mean_mat = jnp.where((r < DIL_HEAD_DIM) == (c < DIL_HEAD_DIM), 1.0 / DIL_HEAD_DIM, 0.0).astype(BF16)
    for part, g_ref in ((0, qg_ref), (1, kg_ref)):
        for g in range(DIL_GROUPS):
            gain = g_ref[g:g + 1, :]
            for pair in range(DIL_HEADS // 2):
                col = ((part * DIL_GROUPS + g) * DIL_HEADS + 2 * pair) * DIL_HEAD_DIM
                x2 = x_ref[:, col:col + 2 * DIL_HEAD_DIM]
                inv = lax.rsqrt(_dot((x2 * x2).astype(BF16), mean_mat) + NORM_EPS)
                for hh in range(2):
                    sl = slice(hh * DIL_HEAD_DIM, (hh + 1) * DIL_HEAD_DIM)
                    xn = x2[:, sl] * inv[:, sl] * gain
                    o_ref[:, col + sl.start:col + sl.stop] = (
                        xn * cos + pltpu.roll(xn, DIL_HEAD_DIM - half, 1) * sin_lo
                        + pltpu.roll(xn, half, 1) * sin_hi)


def _dil_attn_body(*refs, n_chunks):
    ins, o_ref, scr = refs[:15], refs[15], refs[16:]
    c = pl.program_id(1)
    scale = DIL_HEAD_DIM ** -0.5
    ii = lax.broadcasted_iota(jnp.int32, (DIL_BLOCK, DIL_BLOCK), 0)
    jj = lax.broadcasted_iota(jnp.int32, (DIL_BLOCK, DIL_BLOCK), 1)
    cur_mask = jj <= ii
    prev_mask = jj >= ii
    ch = o_ref.shape[0]
    for g, (_, dil) in enumerate(DIL_PATTERNS):
        q_ref, k_ref, v_ref, kh_ref, vh_ref = ins[5 * g:5 * g + 5]
        kf, vf, og, lg = scr[4 * g:4 * g + 4]
        hist = DIL_BLOCK * dil
        kf[0:hist, :] = kh_ref[...]
        vf[0:hist, :] = vh_ref[...]
        kf[hist:hist + ch, :] = k_ref[...]
        vf[hist:hist + ch, :] = v_ref[...]
        def rows(start):
            return pl.ds(start, DIL_BLOCK, stride=dil) if dil > 1 else pl.ds(start, DIL_BLOCK)

        units = [(blk, blk * hist + r) for blk in range(ch // hist) for r in range(dil)]
        for b0 in range(0, len(units), DIL_UNITS_IN_FLIGHT):
            batch = units[b0:b0 + DIL_UNITS_IN_FLIGHT]
            qv = [q_ref[rows(q0), :].astype(BF16) for _, q0 in batch]
            sc = [_dot_nt(q, kf[rows(hist + q0), :].astype(BF16)) for q, (_, q0) in zip(qv, batch)]
            sp = [_dot_nt(q, kf[rows(q0), :].astype(BF16)) for q, (_, q0) in zip(qv, batch)]
            sc = [jnp.where(cur_mask, s * scale, NEG_INF) for s in sc]
            sp = [jnp.where(prev_mask if blk > 0 else prev_mask & (c > 0), s * scale, NEG_INF)
                  for s, (blk, _) in zip(sp, batch)]
            m = [jnp.maximum(jnp.max(a, -1, keepdims=True), jnp.max(b, -1, keepdims=True)) for a, b in zip(sc, sp)]
            pc = [jnp.exp(a - mm) for a, mm in zip(sc, m)]
            pp = [jnp.exp(b - mm) for b, mm in zip(sp, m)]
            l = [jnp.sum(a, -1, keepdims=True) + jnp.sum(b, -1, keepdims=True) for a, b in zip(pc, pp)]
            o = [_dot(a.astype(BF16), vf[rows(hist + q0), :].astype(BF16))
                 + _dot(b.astype(BF16), vf[rows(q0), :].astype(BF16)) for a, b, (_, q0) in zip(pc, pp, batch)]
            for oo, ll, mm, (_, q0) in zip(o, l, m, batch):
                og[rows(q0), :] = oo / ll
                lg[rows(q0), :] = jnp.broadcast_to(mm + jnp.log(ll), (DIL_BLOCK, DIL_HEAD_DIM))
    l0, l1, l2 = scr[3][...], scr[7][...], scr[11][...]
    mx = jnp.maximum(jnp.maximum(l0, l1), l2)
    e0, e1, e2 = jnp.exp(l0 - mx), jnp.exp(l1 - mx), jnp.exp(l2 - mx)
    o_ref[...] = ((e0 * scr[2][...] + e1 * scr[6][...] + e2 * scr[10][...]) / (e0 + e1 + e2)).astype(o_ref.dtype)


def dilated_mixer(x, B, S, positions, gain, w_qkv, q_gain, k_gain, w_out):
    T, D = x.shape
    nh = DIL_GROUPS * DIL_HEADS
    rot = DIL_HEAD_DIM // 4
    inv_freq = ROPE_THETA ** (-jnp.arange(0, rot, 2, dtype=F32) / rot)
    invf = jnp.concatenate([inv_freq, inv_freq, jnp.zeros((DIL_HEAD_DIM - rot,), F32)]).reshape(1, DIL_HEAD_DIM)
    pos = positions.astype(F32).reshape(T, 1)
    qkv = norm_matmul(x, gain, w_qkv)
    tp = 256
    qk_cols = 2 * nh * DIL_HEAD_DIM
    qk = pl.pallas_call(
        _dil_prep_body,
        grid=(T // tp,),
        in_specs=[
            pl.BlockSpec((tp, qk_cols), lambda i: (i, 0)),
            pl.BlockSpec((tp, 1), lambda i: (i, 0)),
            pl.BlockSpec((1, DIL_HEAD_DIM), lambda i: (0, 0)),
            pl.BlockSpec((DIL_GROUPS, DIL_HEAD_DIM), lambda i: (0, 0)),
            pl.BlockSpec((DIL_GROUPS, DIL_HEAD_DIM), lambda i: (0, 0)),
        ],
        out_specs=pl.BlockSpec((tp, qk_cols), lambda i: (i, 0)),
        out_shape=jax.ShapeDtypeStruct((T, qk_cols), F32),
        compiler_params=_cparams(("parallel",), 4 * tp * qk_cols * 4 + 2 * tp * V7X_LANES * 4),
        name="dil_qk_prep",
    )(qkv, pos, invf, q_gain, k_gain)

    ch = DIL_BLOCK * DIL_PATTERNS[-1][1]
    n_chunks = S // ch
    inputs, in_specs, scratch = [], [], []
    vmem = 2 * ch * DIL_HEAD_DIM * 2
    for g, (_, dil) in enumerate(DIL_PATTERNS):
        hist = DIL_BLOCK * dil
        per = ch // hist

        def cur_map(col):
            return lambda b, c, h: (b * n_chunks + c, col + h)

        def hist_map(col, per=per):
            return lambda b, c, h: (jnp.maximum((b * n_chunks + c) * per - 1, 0), col + h)

        inputs += [qk, qk, qkv, qk, qkv]
        in_specs += [
            pl.BlockSpec((ch, DIL_HEAD_DIM), cur_map(g * DIL_HEADS)),
            pl.BlockSpec((ch, DIL_HEAD_DIM), cur_map(nh + g * DIL_HEADS)),
            pl.BlockSpec((ch, DIL_HEAD_DIM), cur_map(2 * nh + g * DIL_HEADS)),
            pl.BlockSpec((hist, DIL_HEAD_DIM), hist_map(nh + g * DIL_HEADS)),
            pl.BlockSpec((hist, DIL_HEAD_DIM), hist_map(2 * nh + g * DIL_HEADS)),
        ]
        scratch += [pltpu.VMEM((hist + ch, DIL_HEAD_DIM), F32), pltpu.VMEM((hist + ch, DIL_HEAD_DIM), F32),
                    pltpu.VMEM((ch, DIL_HEAD_DIM), F32), pltpu.VMEM((ch, DIL_HEAD_DIM), F32)]
        vmem += (2 * (3 * ch + 2 * hist) + 2 * (hist + ch) + 2 * ch) * DIL_HEAD_DIM * 4
    o = pl.pallas_call(
        functools.partial(_dil_attn_body, n_chunks=n_chunks),
        grid=(B, n_chunks, DIL_HEADS),
        in_specs=in_specs,
        out_specs=pl.BlockSpec((ch, DIL_HEAD_DIM), lambda b, c, h: (b * n_chunks + c, h)),
        out_shape=jax.ShapeDtypeStruct((T, DIL_HEADS * DIL_HEAD_DIM), BF16),
        scratch_shapes=scratch,
        compiler_params=_cparams(("parallel", "arbitrary", "arbitrary"), vmem),
        name="dil_attention",
    )(*inputs)
    return matmul_residual(o, w_out, x)


def _hgrn_body(q_ref, f_ref, i_ref, gt_ref, lbl_ref, gain_ref, o_ref, st_scr, *, layer):
    @pl.when(pl.program_id(2) == 0)
    def _():
        st_scr[...] = jnp.zeros_like(st_scr)

    tt = q_ref.shape[0]
    dh = gain_ref.shape[1]
    heads = [slice(h * dh, (h + 1) * dh) for h in range(q_ref.shape[1] // dh)]
    lbl = lbl_ref[...]
    e = jnp.exp(lbl - jnp.max(lbl, axis=0, keepdims=True))
    p = e / jnp.sum(e, axis=0, keepdims=True)
    lb = jnp.sum(p[1:layer + 1, :], axis=0, keepdims=True)
    forget = lb + (1.0 - lb) * jax.nn.sigmoid(f_ref[...].astype(F32))
    k = 1.0 - forget
    gl = jnp.log(forget)
    a_cum, a_tot, tri = _chunk_sums(gl, HGRN_CHUNK)
    q_dec = (q_ref[...].astype(F32) * jnp.exp(a_cum)).astype(BF16)
    k_in = (k * jnp.exp(-a_cum)).astype(BF16)
    k_end = (k * jnp.exp(a_tot - a_cum)).astype(BF16)
    v = i_ref[...].astype(BF16)
    dec = jnp.exp(a_tot)
    att = [jnp.where(tri, _dot_nt(q_dec[:, hs], k_in[:, hs]), 0.0).astype(BF16) for hs in heads]
    o = [_dot(a, v[:, hs]) for a, hs in zip(att, heads)]
    chunks = [slice(c * HGRN_CHUNK, (c + 1) * HGRN_CHUNK) for c in range(tt // HGRN_CHUNK)]
    upd = [[_dot_tn(v[sl, hs], k_end[sl, hs]) for sl in chunks] for hs in heads]
    st = [st_scr[h] for h in range(len(heads))]
    inter = [[] for _ in heads]
    for ci, sl in enumerate(chunks):
        for h, hs in enumerate(heads):
            inter[h].append(_dot_nt(q_dec[sl, hs], st[h].astype(BF16)))
            st[h] = st[h] * dec[sl.start:sl.start + 1, hs] + upd[h][ci]
    gt = gt_ref[...].astype(F32)
    for h, hs in enumerate(heads):
        st_scr[h] = st[h]
        oh = o[h] + jnp.concatenate(inter[h], axis=0)
        o_ref[:, hs] = (_rms(oh, gain_ref[...]) * _silu(gt[:, hs])).astype(o_ref.dtype)


def hgrn_mixer(x, B, S, layer, gain, w_in, lb_logits, norm_gain, w_out, *, tt=256, heads_per_step=8):
    T, D = x.shape
    dh = D // HGRN_HEADS
    proj = norm_matmul(x, gain, w_in, out_dtype=BF16)
    tt = min(tt, S)
    nt = S // tt
    ng = HGRN_HEADS // heads_per_step
    wd = heads_per_step * dh

    def part(pidx):
        return pl.BlockSpec((tt, wd), lambda b, h, s: (b * nt + s, pidx * ng + h))

    o = pl.pallas_call(
        functools.partial(_hgrn_body, layer=layer),
        grid=(B, ng, nt),
        in_specs=[part(0), part(1), part(2), part(3),
                  pl.BlockSpec((DEPTH, wd), lambda b, h, s: (0, h)),
                  pl.BlockSpec((1, dh), lambda b, h, s: (0, 0))],
        out_specs=pl.BlockSpec((tt, wd), lambda b, h, s: (b * nt + s, h)),
        out_shape=jax.ShapeDtypeStruct((T, D), BF16),
        scratch_shapes=[pltpu.VMEM((heads_per_step, dh, dh), F32)],
        compiler_params=_cparams(("parallel", "parallel", "arbitrary"), 24 * tt * wd * 4 + 8 * tt * tt * 4),
        name="hgrn_core",
    )(proj, proj, proj, proj, lb_logits, norm_gain.reshape(1, dh))
    return matmul_residual(o, w_out, x)


def _rwkv_shift_mix(x_ref, xp_ref, gn_ref, first):
    gn = gn_ref[...]
    h = _rms(x_ref[...], gn)
    last = _rms(xp_ref[...], gn)[V7X_SUBLANES - 1:V7X_SUBLANES, :]
    last = jnp.where(first, 0.0, last)
    row = lax.broadcasted_iota(jnp.int32, h.shape, 0)
    return h, jnp.where(row == 0, last, pltpu.roll(h, 1, 0)) - h


def _rwkv_rkv_body(x_ref, xp_ref, gn_ref, mu_ref, wrkv_ref, rkv_ref, mix_scr, *, tiles_per_seq, n_col_tiles):
    n = pl.program_id(1)
    first = (pl.program_id(0) % tiles_per_seq) == 0

    @pl.when(n == 0)
    def _():
        gn, mu = gn_ref[...], mu_ref[...]
        last = jnp.where(first, 0.0, _rms(xp_ref[...], gn)[V7X_SUBLANES - 1:V7X_SUBLANES, :])
        sub = min(256, x_ref.shape[0])
        for r0 in range(0, x_ref.shape[0], sub):
            h = _rms(x_ref[r0:r0 + sub, :], gn)
            row = lax.broadcasted_iota(jnp.int32, h.shape, 0)
            d = jnp.where(row == 0, last, pltpu.roll(h, 1, 0)) - h
            for m in range(3):
                mix_scr[m, r0:r0 + sub, :] = (h + d * mu[m:m + 1, :]).astype(BF16)
            last = h[sub - 1:sub, :]

    rkv_ref[...] = _dot(mix_scr[n // n_col_tiles], wrkv_ref[...])


def _rwkv_lora_body(x_ref, xp_ref, gn_ref, mu_ref, w0_ref, w1_ref, w2_ref, a0_ref, a1_ref, a2_ref, g1_ref, g2_ref,
                    lw_ref, a_ref, g_ref, *, tiles_per_seq):
    h, d = _rwkv_shift_mix(x_ref, xp_ref, gn_ref, (pl.program_id(0) % tiles_per_seq) == 0)
    mu = mu_ref[...]
    xw = (h + d * mu[3:4, :]).astype(BF16)
    xa = (h + d * mu[4:5, :]).astype(BF16)
    xg = (h + d * mu[5:6, :]).astype(BF16)
    z = -(w0_ref[...] + _dot(jnp.tanh(_dot(xw, w1_ref[...])).astype(BF16), w2_ref[...]))
    softplus = jnp.maximum(z, 0.0) + jnp.log1p(jnp.exp(-jnp.abs(z)))
    lw_ref[...] = -jnp.exp(-softplus - 0.5)
    a_ref[...] = jax.nn.sigmoid(a0_ref[...] + _dot(_dot(xa, a1_ref[...]).astype(BF16), a2_ref[...])).astype(a_ref.dtype)
    g_ref[...] = _dot(jax.nn.sigmoid(_dot(xg, g1_ref[...])).astype(BF16), g2_ref[...]).astype(g_ref.dtype)


def _rwkv_core_body(r_ref, k_ref, v_ref, lw_ref, a_ref, g_ref, kk_ref, ka_ref, rk_ref, lnw_ref, lnb_ref,
                    o_ref, h_scr):
    @pl.when(pl.program_id(2) == 0)
    def _():
        h_scr[...] = jnp.zeros_like(h_scr)

    tt = r_ref.shape[0]
    C, N = RWKV_CHUNK, RWKV_HEAD_DIM
    lw = lw_ref[...]
    g_cum, g_tot, _ = _chunk_sums(lw, C)
    r, k, v, a = r_ref[...], k_ref[...], v_ref[...], a_ref[...].astype(F32)
    pairs = [slice(p * 2 * N, (p + 1) * 2 * N) for p in range(r.shape[1] // (2 * N))]
    left = lax.broadcasted_iota(jnp.int32, (tt, 2 * N), 1) < N

    def head_sum(t):
        return jnp.concatenate(
            [jnp.where(left, jnp.sum(jnp.where(left, t[:, ps], 0.0), -1, keepdims=True),
                       jnp.sum(jnp.where(left, 0.0, t[:, ps]), -1, keepdims=True)) for ps in pairs], axis=1)

    kk = k * kk_ref[...]
    kk = kk * lax.rsqrt(jnp.maximum(head_sum(kk * kk), 1e-24))
    kmod = k * (1.0 + (a - 1.0) * ka_ref[...])
    bv = kk * a
    e_neg = jnp.exp(-g_cum)
    e_end = jnp.exp(g_tot - g_cum)
    a_t = (-kk) * jnp.exp(g_cum - lw)
    r_t = r * jnp.exp(g_cum)
    k_t, b_t = kmod * e_neg, bv * e_neg
    k_h, b_h = kmod * e_end, bv * e_end
    dec = jnp.exp(g_tot)
    def key_lanes(t):
        out = []
        for ps in pairs:
            out += [jnp.where(left, t[:, ps], 0.0), jnp.where(left, pltpu.roll(t[:, ps], N, 1), 0.0)]
        return out

    a_k, r_k, bt_k, kt_k, bh_k, kh_k, dec_k = map(key_lanes, (a_t, r_t, b_t, k_t, b_h, k_h, dec))
    v_v = []
    for ps in pairs:
        v_v += [jnp.where(left, 0.0, pltpu.roll(v[:, ps], N, 1)), jnp.where(left, 0.0, v[:, ps])]
    n_heads = 2 * len(pairs)
    i2 = lax.broadcasted_iota(jnp.int32, (C, 2 * C), 0)
    lane2 = lax.broadcasted_iota(jnp.int32, (C, 2 * C), 1)
    lo = lane2 < C
    t2 = jnp.bitwise_and(lane2, C - 1)
    strict, incl = t2 < i2, t2 <= i2
    eye_hi = jnp.where(lane2 == i2 + C, 1.0, 0.0)
    zeros16 = jnp.zeros((C, 2 * C), BF16)

    nc = tt // C
    units = [(hh, c) for hh in range(n_heads) for c in range(nc)]

    def cut(per_head, u):
        return per_head[u[0]][u[1] * C:(u[1] + 1) * C, :]

    v_c = [cut(v_v, u).astype(BF16) for u in units]
    prod = [_dot_nt(jnp.concatenate([cut(a_k, u), cut(r_k, u)], axis=0).astype(BF16),
                    jnp.concatenate([cut(bt_k, u), cut(kt_k, u)], axis=0).astype(BF16))
            for u in units]
    n_abk = [jnp.where(strict, p[:C], 0.0) for p in prod]
    t_rbk = [jnp.where(incl, p[C:], 0.0).astype(BF16) for p in prod]
    n_lo = [jnp.where(lo, n, 0.0) for n in n_abk]
    s1 = [_dot(n.astype(BF16), jnp.concatenate([nl.astype(BF16), vv], axis=0))
          for n, nl, vv in zip(n_abk, n_lo, v_c)]
    z = [jnp.where(lo, s, 0.0) + pltpu.roll(nl, C, 1) + eye_hi for s, nl in zip(s1, n_lo)]
    for _ in range(5):
        z16 = [zz.astype(BF16) for zz in z]
        z = [_dot(zz16[:, :C], zz16) + jnp.where(lo, 0.0, zz) for zz, zz16 in zip(z, z16)]
    w0 = [(cut(a_k, u) + jnp.where(lo, 0.0, s)).astype(BF16) for u, s in zip(units, s1)]
    au = [_dot(zz.astype(BF16), jnp.concatenate([zeros16, w], axis=0)).astype(BF16)
          for zz, w in zip(z, w0)]
    auv = [jnp.concatenate([x, vv], axis=0) for x, vv in zip(au, v_c)]
    ry = [_dot(t, x) for t, x in zip(t_rbk, auv)]
    th = [_dot_tn(x, jnp.concatenate([cut(bh_k, u), cut(kh_k, u)], axis=0).astype(BF16))
          for x, u in zip(auv, units)]
    r_p = [(cut(r_k, u) + jnp.where(lo, y, 0.0)).astype(BF16) for u, y in zip(units, ry)]

    ht = [h_scr[hh] for hh in range(n_heads)]
    ys = [[] for _ in range(n_heads)]
    for c in range(nc):
        for hh in range(n_heads):
            i = hh * nc + c
            ht16 = ht[hh].astype(BF16)
            ys[hh].append(_dot_nt(r_p[i], ht16) + ry[i][:, C:])
            ht[hh] = (ht[hh] * dec_k[hh][c * C:c * C + 1, :] + _dot(ht16[:, :C], th[i][:C].astype(BF16))
                      + th[i][C:])
    y_heads = []
    for hh in range(n_heads):
        h_scr[hh] = ht[hh]
        y = jnp.concatenate(ys[hh], axis=0)
        mean = jnp.mean(y, -1, keepdims=True)
        var = jnp.mean(jnp.square(y - mean), -1, keepdims=True)
        y_heads.append((y - mean) * lax.rsqrt(var + RWKV_GN_EPS))
    yn = jnp.concatenate(y_heads, axis=1) * lnw_ref[...] + lnb_ref[...]
    bonus = head_sum(r * kmod * rk_ref[...]) * v
    o_ref[...] = ((yn + bonus) * g_ref[...].astype(F32)).astype(o_ref.dtype)


def rwkv_mixer(x, B, S, gain, mu, w_rkv, w0, w1, w2, a0, a1, a2, g1, g2, k_k, k_a, r_k, ln_w, ln_b, w_out,
               *, tm=1024, tn=1024, tl=256, tt=256, heads_per_step=4):
    T, D = x.shape
    tm, tl, tt = min(tm, S), min(tl, S), min(tt, S)
    nct = D // tn
    row = lambda i, n: (0, 0)

    def prev_rows(t):
        return lambda i, *_: (jnp.maximum(i * (t // V7X_SUBLANES) - 1, 0), 0)

    rkv = pl.pallas_call(
        functools.partial(_rwkv_rkv_body, tiles_per_seq=S // tm, n_col_tiles=nct),
        grid=(T // tm, 3 * nct),
        in_specs=[
            pl.BlockSpec((tm, D), lambda i, n: (i, 0)),
            pl.BlockSpec((V7X_SUBLANES, D), prev_rows(tm)),
            pl.BlockSpec((1, D), row),
            pl.BlockSpec((6, D), row),
            _wspec(w_rkv, (None, D, tn), lambda i, n: (n // nct, 0, n % nct)),
        ],
        out_specs=pl.BlockSpec((tm, tn), lambda i, n: (i, n)),
        out_shape=jax.ShapeDtypeStruct((T, 3 * D), F32),
        scratch_shapes=[pltpu.VMEM((3, tm, D), BF16)],
        compiler_params=_cparams(("parallel", "arbitrary"),
                                 2 * tm * D * 4 + 3 * tm * D * 2 + 2 * D * tn * 2 + 2 * tm * tn * 4 + 3 * tm * D * 4),
        name="rwkv_rkv",
    )(x, x, gain.reshape(1, D), mu, w_rkv[0])

    lora = w1.shape[1]
    pad = (-lora) % V7X_LANES
    w1p, a1p = jnp.pad(w1, ((0, 0), (0, pad))), jnp.pad(a1, ((0, 0), (0, pad)))
    w2p, a2p = jnp.pad(w2, ((0, pad), (0, 0))), jnp.pad(a2, ((0, pad), (0, 0)))
    lp, gl = lora + pad, g1.shape[1]
    one = lambda i: (0, 0)
    tok_l = pl.BlockSpec((tl, D), lambda i: (i, 0))
    lw, a, g = pl.pallas_call(
        functools.partial(_rwkv_lora_body, tiles_per_seq=S // tl),
        grid=(T // tl,),
        in_specs=[
            tok_l,
            pl.BlockSpec((V7X_SUBLANES, D), prev_rows(tl)),
            pl.BlockSpec((1, D), one),
            pl.BlockSpec((6, D), one),
            pl.BlockSpec((1, D), one), pl.BlockSpec((D, lp), one), pl.BlockSpec((lp, D), one),
            pl.BlockSpec((1, D), one), pl.BlockSpec((D, lp), one), pl.BlockSpec((lp, D), one),
            pl.BlockSpec((D, gl), one), pl.BlockSpec((gl, D), one),
        ],
        out_specs=[tok_l, tok_l, tok_l],
        out_shape=[jax.ShapeDtypeStruct((T, D), F32), jax.ShapeDtypeStruct((T, D), BF16),
                   jax.ShapeDtypeStruct((T, D), BF16)],
        compiler_params=_cparams(("parallel",), 14 * tl * D * 4 + 4 * (2 * D * lp + D * gl) * 2),
        name="rwkv_lora",
    )(x, x, gain.reshape(1, D), mu, w0.reshape(1, D), w1p, w2p, a0.reshape(1, D), a1p, a2p, g1, g2)

    nt = S // tt
    pw = heads_per_step * RWKV_HEAD_DIM
    npair = D // pw

    def tok(col0):
        return pl.BlockSpec((tt, pw), lambda b, p, s: (b * nt + s, col0 + p))

    par = pl.BlockSpec((1, pw), lambda b, p, s: (0, p))
    o = pl.pallas_call(
        _rwkv_core_body,
        grid=(B, npair, nt),
        in_specs=[tok(0), tok(npair), tok(2 * npair), tok(0), tok(0), tok(0), par, par, par, par, par],
        out_specs=pl.BlockSpec((tt, pw), lambda b, p, s: (b * nt + s, p)),
        out_shape=jax.ShapeDtypeStruct((T, D), BF16),
        scratch_shapes=[pltpu.VMEM((heads_per_step, RWKV_HEAD_DIM, 2 * RWKV_HEAD_DIM), F32)],
        compiler_params=_cparams(("parallel", "parallel", "arbitrary"), 40 * tt * pw * 4 + 8 * tt * tt * 4),
        name="rwkv_core",
    )(rkv, rkv, rkv, lw, a, g, k_k.reshape(1, D), k_a.reshape(1, D), r_k.reshape(1, D),
      ln_w.reshape(1, D), ln_b.reshape(1, D))
    return matmul_residual(o, w_out, x)


def kernel(x, mem, positions, ffn_norm, ffn_w_gate, ffn_w_up, ffn_w_down, mix_norm, xattn_norm, mem_norm, xattn_wq, xattn_wkv, xattn_wo, xattn_q_gain, xattn_k_gain, conv_w_in, conv_w, conv_w_out, dil_w_qkv, dil_q_gain, dil_k_gain, dil_w_out, hgrn_w_in, hgrn_lb_logits, hgrn_norm, hgrn_w_out, rwkv_mu, rwkv_w_rkv, rwkv_w0, rwkv_w1, rwkv_w2, rwkv_a0, rwkv_a1, rwkv_a2, rwkv_g1, rwkv_g2, rwkv_k_k, rwkv_k_a, rwkv_r_k, rwkv_ln_w, rwkv_ln_b, rwkv_w_out):
    B, S, D = x.shape
    assert D == D_MODEL and S % (DIL_BLOCK * DIL_PATTERNS[-1][1]) == 0
    depth = ffn_norm.shape[0]
    xf = x.reshape(B * S, D)
    memf = mem.reshape(B * MEM_LEN, D)
    bf = lambda w: w.astype(BF16)
    wg_all, wu_all, wd_all = bf(ffn_w_gate), bf(ffn_w_up), bf(ffn_w_down)
    wq_all, wkv_all, wo_all = bf(xattn_wq), bf(xattn_wkv), bf(xattn_wo)
    conv_in_all, conv_out_all = bf(conv_w_in), bf(conv_w_out)
    dil_qkv_all, dil_out_all = bf(dil_w_qkv), bf(dil_w_out)
    hgrn_in_all, hgrn_out_all = bf(hgrn_w_in), bf(hgrn_w_out)
    rkv_all, rwkv_out_all = bf(rwkv_w_rkv), bf(rwkv_w_out)
    for i in range(depth):
        kind, j = i % N_MIXERS, i // N_MIXERS
        xf = ffn_half(xf, ffn_norm[i, 0], (wg_all, (i, 0)), (wu_all, (i, 0)), (wd_all, (i, 0)))
        if kind == 0:
            xf = conv_mixer(xf, S, mix_norm[i], (conv_in_all, (j,)), conv_w[j], (conv_out_all, (j,)))
        elif kind == 1:
            xf = dilated_mixer(xf, B, S, positions, mix_norm[i], (dil_qkv_all, (j,)), dil_q_gain[j], dil_k_gain[j],
                               (dil_out_all, (j,)))
        elif kind == 2:
            xf = hgrn_mixer(xf, B, S, i, mix_norm[i], (hgrn_in_all, (j,)), hgrn_lb_logits, hgrn_norm[j],
                            (hgrn_out_all, (j,)))
        else:
            xf = rwkv_mixer(xf, B, S, mix_norm[i], rwkv_mu[j], (rkv_all, (j,)), rwkv_w0[j], bf(rwkv_w1[j]),
                            bf(rwkv_w2[j]), rwkv_a0[j], bf(rwkv_a1[j]), bf(rwkv_a2[j]), bf(rwkv_g1[j]),
                            bf(rwkv_g2[j]), rwkv_k_k[j], rwkv_k_a[j], rwkv_r_k[j], rwkv_ln_w[j], rwkv_ln_b[j],
                            (rwkv_out_all, (j,)))
        xf = cross_attention(xf, S, memf, xattn_norm[i], mem_norm[i], (wq_all, (i,)), (wkv_all, (i,)),
                             (wo_all, (i,)), xattn_q_gain[i], xattn_k_gain[i])
        xf = ffn_half(xf, ffn_norm[i, 1], (wg_all, (i, 1)), (wu_all, (i, 1)), (wd_all, (i, 1)))
    return xf.reshape(B, S, D)
```

```python
import functools

import jax
import jax.numpy as jnp
from jax import lax
from jax.experimental import pallas as pl
from jax.experimental.pallas import tpu as pltpu

F32 = jnp.float32
BF16 = jnp.bfloat16

D_MODEL = 2048
DEPTH = 4
N_MIXERS = 4
MEM_LEN = 256
FFN_DIM = 5632
NORM_EPS = 1e-6
NEG_INF = -1e30
ROPE_THETA = 500000.0
DIL_PATTERNS = ((128, 1), (512, 4), (2048, 16))
DIL_GROUPS = 3
DIL_HEADS = 8
DIL_HEAD_DIM = 128
DIL_BLOCK = 128
DIL_UNITS_IN_FLIGHT = 8
HGRN_CHUNK = 16
HGRN_HEADS = 16
RWKV_HEAD_DIM = 64
RWKV_HEADS = 32
RWKV_CHUNK = 64
RWKV_GN_EPS = 64e-5
XATTN_HEADS = 4
XATTN_HEAD_DIM = 512

V7X_LANES = 128
V7X_SUBLANES = 8
V7X_VMEM_BYTES = 64 * 2**20
V7X_VMEM_CAP = 60 * 2**20


def _cparams(sem, vmem_bytes):
    limit = min(int(vmem_bytes * 1.25) + (4 << 20), V7X_VMEM_CAP)
    return pltpu.CompilerParams(dimension_semantics=sem, vmem_limit_bytes=limit)


def _rms(x, gain):
    return x * lax.rsqrt(jnp.mean(x * x, axis=-1, keepdims=True) + NORM_EPS) * gain


def _dot(a, b):
    return jnp.dot(a, b, preferred_element_type=F32)


def _dot_nt(a, b):
    return lax.dot_general(a, b, (((1,), (1,)), ((), ())), preferred_element_type=F32)


def _dot_tn(a, b):
    return lax.dot_general(a, b, (((0,), (0,)), ((), ())), preferred_element_type=F32)


def _wshape(w):
    arr, lead = w
    return arr.shape[len(lead):]


def _wspec(w, block, tail):
    lead = tuple(w[1])
    return pl.BlockSpec((None,) * len(lead) + tuple(block), lambda *g: lead + tuple(tail(*g)))


def _chunk_sums(x, chunk):
    t, w = x.shape
    rows = lax.broadcasted_iota(jnp.int32, (t, t), 0)
    cols = lax.broadcasted_iota(jnp.int32, (t, t), 1)
    same = _chunk_of(rows, chunk) == _chunk_of(cols, chunk)
    tri = same & (cols <= rows)
    sel = jnp.concatenate([tri.astype(BF16), same.astype(BF16)], axis=0)
    hi = x.astype(BF16)
    r1 = x - hi.astype(F32)
    mid = r1.astype(BF16)
    lo = (r1 - mid.astype(F32)).astype(BF16)
    s = _dot(sel, jnp.concatenate([hi, mid, lo], axis=1))
    s = s[:, :w] + s[:, w:2 * w] + s[:, 2 * w:]
    return s[:t], s[t:], tri


def _chunk_of(idx, chunk):
    return jnp.right_shift(idx, chunk.bit_length() - 1)


def _silu(x):
    return x * jax.nn.sigmoid(x)


def _norm_matmul_body(x_ref, g_ref, w_ref, o_ref, h_scr):
    @pl.when(pl.program_id(1) == 0)
    def _():
        h_scr[...] = _rms(x_ref[...], g_ref[...]).astype(BF16)

    o_ref[...] = _dot(h_scr[...], w_ref[...]).astype(o_ref.dtype)


def norm_matmul(x, gain, w, *, tm=1024, tn=1024, out_dtype=F32):
    M, K = x.shape
    N = _wshape(w)[1]
    tm, tn = min(tm, M), min(tn, N)
    ob = jnp.dtype(out_dtype).itemsize
    vmem = 2 * tm * K * 4 + tm * K * 2 + 2 * K * tn * 2 + 2 * tm * tn * ob
    return pl.pallas_call(
        _norm_matmul_body,
        grid=(M // tm, N // tn),
        in_specs=[
            pl.BlockSpec((tm, K), lambda i, j: (i, 0)),
            pl.BlockSpec((1, K), lambda i, j: (0, 0)),
            _wspec(w, (K, tn), lambda i, j: (0, j)),
        ],
        out_specs=pl.BlockSpec((tm, tn), lambda i, j: (i, j)),
        out_shape=jax.ShapeDtypeStruct((M, N), out_dtype),
        scratch_shapes=[pltpu.VMEM((tm, K), BF16)],
        compiler_params=_cparams(("parallel", "arbitrary"), vmem),
        name="norm_matmul",
    )(x, gain.reshape(1, K), w[0])


def _ffn_body(*refs, n_cast):
    x_ref, g_ref, wg_ref, wu_ref, wd_ref = refs[:5]
    src = refs[5:5 + n_cast]
    o_ref = refs[5 + n_cast]
    dst = refs[6 + n_cast:6 + 2 * n_cast]
    h_scr = refs[-1]

    @pl.when(pl.program_id(1) == 0)
    def _():
        x = x_ref[...]
        h_scr[...] = _rms(x, g_ref[...]).astype(BF16)
        o_ref[...] = x

    h = h_scr[...]
    act = _silu(_dot(h, wg_ref[...])) * _dot(h, wu_ref[...])
    o_ref[...] += 0.5 * _dot(act.astype(BF16), wd_ref[...])
    for s_ref, d_ref in zip(src, dst):
        d_ref[...] = s_ref[...].astype(BF16)


def ffn_half(x, gain, wg, wu, wd, cast_next=None, *, tm=1024, tf=512):
    M, D = x.shape
    F = _wshape(wg)[1]
    tm = min(tm, M)
    gm, gf = M // tm, F // tf
    vmem = 4 * tm * D * 4 + tm * D * 2 + 2 * 3 * D * tf * 2 + 3 * tm * tf * 4
    in_specs = [
        pl.BlockSpec((tm, D), lambda i, f: (i, 0)),
        pl.BlockSpec((1, D), lambda i, f: (0, 0)),
        _wspec(wg, (D, tf), lambda i, f: (0, f)),
        _wspec(wu, (D, tf), lambda i, f: (0, f)),
        _wspec(wd, (tf, D), lambda i, f: (f, 0)),
    ]
    out_specs = [pl.BlockSpec((tm, D), lambda i, f: (i, 0))]
    out_shape = [jax.ShapeDtypeStruct((M, D), F32)]
    cast_in = []
    if cast_next is not None:
        assert D % gm == 0 and F % gf == 0
        dm = D // gm
        for w, shape, block, tail in (
                (cast_next[0], (D, F), (dm, tf), lambda i, f: (i, f)),
                (cast_next[1], (D, F), (dm, tf), lambda i, f: (i, f)),
                (cast_next[2], (F, D), (tf, dm), lambda i, f: (f, i))):
            in_specs.append(_wspec(w, block, tail))
            out_specs.append(pl.BlockSpec(block, tail))
            out_shape.append(jax.ShapeDtypeStruct(shape, BF16))
            cast_in.append(w[0])
            vmem += 2 * block[0] * block[1] * 6
    outs = pl.pallas_call(
        functools.partial(_ffn_body, n_cast=len(cast_in)),
        grid=(gm, gf),
        in_specs=in_specs,
        out_specs=out_specs,
        out_shape=out_shape,
        scratch_shapes=[pltpu.VMEM((tm, D), BF16)],
        compiler_params=_cparams(("parallel", "arbitrary"), vmem),
        name="ffn_half",
    )(x, gain.reshape(1, D), wg[0], wu[0], wd[0], *cast_in)
    return outs[0], [(w, ()) for w in outs[1:]]


def _matmul_res_body(a_ref, w_ref, r_ref, o_ref):
    o_ref[...] = r_ref[...] + _dot(a_ref[...], w_ref[...])


def matmul_residual(a, w, res, *, tm=1024, tn=1024):
    M, K = a.shape
    N = _wshape(w)[1]
    tm = min(tm, M)
    vmem = 2 * tm * K * 2 + 2 * K * tn * 2 + 4 * tm * tn * 4
    return pl.pallas_call(
        _matmul_res_body,
        grid=(M // tm, N // tn),
        in_specs=[
            pl.BlockSpec((tm, K), lambda i, j: (i, 0)),
            _wspec(w, (K, tn), lambda i, j: (0, j)),
            pl.BlockSpec((tm, tn), lambda i, j: (i, j)),
        ],
        out_specs=pl.BlockSpec((tm, tn), lambda i, j: (i, j)),
        out_shape=jax.ShapeDtypeStruct((M, N), F32),
        compiler_params=_cparams(("parallel", "arbitrary"), vmem),
        name="matmul_residual",
    )(a, w[0], res)


def _prologue_matmul_res_body(prologue, n_in, *refs):
    in_refs = refs[:n_in]
    w_ref, r_ref, o_ref, lhs_scr = refs[n_in:]
    row_tile = pl.program_id(0)

    @pl.when(pl.program_id(1) == 0)
    def _():
        prologue(row_tile, *in_refs, lhs_scr)

    o_ref[...] = r_ref[...] + _dot(lhs_scr[...], w_ref[...])


def prologue_matmul_residual(prologue, inputs, in_specs, w, res, *, tm, tn, in_vmem, name):
    M, N = res.shape
    K = _wshape(w)[0]
    vmem = in_vmem + tm * K * 2 + 2 * K * tn * 2 + 4 * tm * tn * 4
    return pl.pallas_call(
        functools.partial(_prologue_matmul_res_body, prologue, len(inputs)),
        grid=(M // tm, N // tn),
        in_specs=list(in_specs) + [
            _wspec(w, (K, tn), lambda i, j: (0, j)),
            pl.BlockSpec((tm, tn), lambda i, j: (i, j)),
        ],
        out_specs=pl.BlockSpec((tm, tn), lambda i, j: (i, j)),
        out_shape=jax.ShapeDtypeStruct((M, N), F32),
        scratch_shapes=[pltpu.VMEM((tm, K), BF16)],
        compiler_params=_cparams(("parallel", "arbitrary"), vmem),
        name=name,
    )(*inputs, w[0], res)


def _conv_prologue(tiles_per_seq, row_tile, b_ref, c_ref, u_ref, cp_ref, up_ref, cw_ref, lhs_scr):
    cu = c_ref[...].astype(F32) * u_ref[...].astype(F32)
    prev = cp_ref[...].astype(F32) * up_ref[...].astype(F32)
    first = (row_tile % tiles_per_seq) == 0
    prev = jnp.where(first, 0.0, prev)
    n_prev = prev.shape[0]
    p1, p2 = prev[n_prev - 1:n_prev, :], prev[n_prev - 2:n_prev - 1, :]
    row = lax.broadcasted_iota(jnp.int32, cu.shape, 0)
    s1 = jnp.where(row == 0, p1, pltpu.roll(cu, 1, 0))
    s2 = jnp.where(row == 0, p2, jnp.where(row == 1, p1, pltpu.roll(cu, 2, 0)))
    w = cw_ref[...]
    y = w[0:1, :] * s2 + w[1:2, :] * s1 + w[2:3, :] * cu
    lhs_scr[...] = (b_ref[...].astype(F32) * y).astype(BF16)


def conv_mixer(x, S, gain, w_in, conv_w, w_out, *, tm=512):
    T, D = x.shape
    tm = min(tm, S)
    bcu = norm_matmul(x, gain, w_in, out_dtype=BF16)
    halo = 2 * V7X_SUBLANES
    rh = tm // halo

    def prev_map(col):
        return lambda i, j: (jnp.maximum(i * rh - 1, 0), col)

    in_specs = [
        pl.BlockSpec((tm, D), lambda i, j: (i, 0)),
        pl.BlockSpec((tm, D), lambda i, j: (i, 1)),
        pl.BlockSpec((tm, D), lambda i, j: (i, 2)),
        pl.BlockSpec((halo, D), prev_map(1)),
        pl.BlockSpec((halo, D), prev_map(2)),
        pl.BlockSpec((3, D), lambda i, j: (0, 0)),
    ]
    return prologue_matmul_residual(
        functools.partial(_conv_prologue, S // tm), (bcu, bcu, bcu, bcu, bcu, conv_w), in_specs, w_out, x,
        tm=tm, tn=1024, in_vmem=2 * 3 * tm * D * 2 + 4 * halo * D * 2 + 3 * tm * D * 4, name="conv_mixer_out")


def _xattn_body(x_ref, kv_ref, xg_ref, qg_ref, kg_ref, wq_ref, wo_ref, o_ref, attn_scr):
    scale = XATTN_HEAD_DIM ** -0.5
    D = XATTN_HEADS * XATTN_HEAD_DIM
    x = x_ref[...]
    q = _dot(_rms(x, xg_ref[...]).astype(BF16), wq_ref[...])
    for h in range(XATTN_HEADS):
        sl = slice(h * XATTN_HEAD_DIM, (h + 1) * XATTN_HEAD_DIM)
        qn = _rms(q[:, sl], qg_ref[...]).astype(BF16)
        kn = _rms(kv_ref[:, sl], kg_ref[...]).astype(BF16)
        v = kv_ref[:, D + h * XATTN_HEAD_DIM:D + (h + 1) * XATTN_HEAD_DIM].astype(BF16)
        s = _dot_nt(qn, kn) * scale
        p = jnp.exp(s - jnp.max(s, axis=-1, keepdims=True))
        l = jnp.sum(p, axis=-1, keepdims=True)
        attn_scr[:, sl] = (_dot(p.astype(BF16), v) / l).astype(BF16)
    o_ref[...] = x + _dot(attn_scr[...], wo_ref[...])


def cross_attention(x, S, mem, xgain, mgain, wq, wkv, wo, q_gain, k_gain, *, tm=512):
    T, D = x.shape
    tm = min(tm, S)
    kv = norm_matmul(mem, mgain, wkv)
    tps = S // tm
    once = pl.Buffered(1)
    vmem = 4 * tm * D * 4 + 2 * MEM_LEN * 2 * D * 4 + 2 * D * D * 2 + tm * D * 2 + 3 * tm * D * 4
    return pl.pallas_call(
        _xattn_body,
        grid=(T // tm,),
        in_specs=[
            pl.BlockSpec((tm, D), lambda i: (i, 0)),
            pl.BlockSpec((MEM_LEN, 2 * D), lambda i: (i // tps, 0)),
            pl.BlockSpec((1, D), lambda i: (0, 0)),
            pl.BlockSpec((1, XATTN_HEAD_DIM), lambda i: (0, 0)),
            pl.BlockSpec((1, XATTN_HEAD_DIM), lambda i: (0, 0)),
            pl.BlockSpec((None,) * len(wq[1]) + (D, D), lambda i: tuple(wq[1]) + (0, 0), pipeline_mode=once),
            pl.BlockSpec((None,) * len(wo[1]) + (D, D), lambda i: tuple(wo[1]) + (0, 0), pipeline_mode=once),
        ],
        out_specs=pl.BlockSpec((tm, D), lambda i: (i, 0)),
        out_shape=jax.ShapeDtypeStruct((T, D), F32),
        scratch_shapes=[pltpu.VMEM((tm, D), BF16)],
        compiler_params=_cparams(("parallel",), vmem),
        name="xattn",
    )(x, kv, xgain.reshape(1, D), q_gain.reshape(1, -1), k_gain.reshape(1, -1), wq[0], wo[0])


def _dil_prep_body(x_ref, pos_ref, invf_ref, qg_ref, kg_ref, o_ref):
    ang = pos_ref[...] * invf_ref[...]
    lane = lax.broadcasted_iota(jnp.int32, ang.shape, 1)
    half = DIL_HEAD_DIM // 8
    cos, sin = jnp.cos(ang), jnp.sin(ang)
    sin_lo = jnp.where(lane < half, -sin, 0.0)
    sin_hi = jnp.where((lane >= half) & (lane < 2 * half), sin, 0.0)
    r = lax.broadcasted_iota(jnp.int32, (2 * DIL_HEAD_DIM, 2 * DIL_HEAD_DIM), 0)
    c = lax.broadcasted_iota(jnp.int32, (2 * DIL_HEAD_DIM, 2 * DIL_HEAD_DIM), 1)
    mean_mat = jnp.where((r < DIL_HEAD_DIM) == (c < DIL_HEAD_DIM), 1.0 / DIL_HEAD_DIM, 0.0).astype(BF16)
    for part, g_ref in ((0, qg_ref), (1, kg_ref)):
        for g in range(DIL_GROUPS):
            gain = g_ref[g:g + 1, :]
            for pair in range(DIL_HEADS // 2):
                col = ((part * DIL_GROUPS + g) * DIL_HEADS + 2 * pair) * DIL_HEAD_DIM
                x2 = x_ref[:, col:col + 2 * DIL_HEAD_DIM]
                inv = lax.rsqrt(_dot((x2 * x2).astype(BF16), mean_mat) + NORM_EPS)
                for hh in range(2):
                    sl = slice(hh * DIL_HEAD_DIM, (hh + 1) * DIL_HEAD_DIM)
                    xn = x2[:, sl] * inv[:, sl] * gain
                    o_ref[:, col + sl.start:col + sl.stop] = (
                        xn * cos + pltpu.roll(xn, DIL_HEAD_DIM - half, 1) * sin_lo
                        + pltpu.roll(xn, half, 1) * sin_hi)


def _dil_attn_body(*refs, n_chunks):
    ins, o_ref, scr = refs[:15], refs[15], refs[16:]
    c = pl.program_id(1)
    scale = DIL_HEAD_DIM ** -0.5
    ii = lax.broadcasted_iota(jnp.int32, (DIL_BLOCK, DIL_BLOCK), 0)
    jj = lax.broadcasted_iota(jnp.int32, (DIL_BLOCK, DIL_BLOCK), 1)
    cur_mask = jj <= ii
    prev_mask = jj >= ii
    ch = o_ref.shape[0]
    for g, (_, dil) in enumerate(DIL_PATTERNS):
        q_ref, k_ref, v_ref, kh_ref, vh_ref = ins[5 * g:5 * g + 5]
        kf, vf, og, lg = scr[4 * g:4 * g + 4]
        hist = DIL_BLOCK * dil
        kf[0:hist, :] = kh_ref[...]
        vf[0:hist, :] = vh_ref[...]
        kf[hist:hist + ch, :] = k_ref[...]
        vf[hist:hist + ch, :] = v_ref[...]
        def rows(start):
            return pl.ds(start, DIL_BLOCK, stride=dil) if dil > 1 else pl.ds(start, DIL_BLOCK)

        units = [(blk, blk * hist + r) for blk in range(ch // hist) for r in range(dil)]
        for b0 in range(0, len(units), DIL_UNITS_IN_FLIGHT):
            batch = units[b0:b0 + DIL_UNITS_IN_FLIGHT]
            qv = [q_ref[rows(q0), :].astype(BF16) for _, q0 in batch]
            sc = [_dot_nt(q, kf[rows(hist + q0), :].astype(BF16)) for q, (_, q0) in zip(qv, batch)]
            sp = [_dot_nt(q, kf[rows(q0), :].astype(BF16)) for q, (_, q0) in zip(qv, batch)]
            sc = [jnp.where(cur_mask, s * scale, NEG_INF) for s in sc]
            sp = [jnp.where(prev_mask if blk > 0 else prev_mask & (c > 0), s * scale, NEG_INF)
                  for s, (blk, _) in zip(sp, batch)]
            m = [jnp.maximum(jnp.max(a, -1, keepdims=True), jnp.max(b, -1, keepdims=True)) for a, b in zip(sc, sp)]
            pc = [jnp.exp(a - mm) for a, mm in zip(sc, m)]
            pp = [jnp.exp(b - mm) for b, mm in zip(sp, m)]
            l = [jnp.sum(a, -1, keepdims=True) + jnp.sum(b, -1, keepdims=True) for a, b in zip(pc, pp)]
            o = [_dot(a.astype(BF16), vf[rows(hist + q0), :].astype(BF16))
                 + _dot(b.astype(BF16), vf[rows(q0), :].astype(BF16)) for a, b, (_, q0) in zip(pc, pp, batch)]
            for oo, ll, mm, (_, q0) in zip(o, l, m, batch):
                og[rows(q0), :] = oo / ll
                lg[rows(q0), :] = jnp.broadcast_to(mm + jnp.log(ll), (DIL_BLOCK, DIL_HEAD_DIM))
    l0, l1, l2 = scr[3][...], scr[7][...], scr[11][...]
    mx = jnp.maximum(jnp.maximum(l0, l1), l2)
    e0, e1, e2 = jnp.exp(l0 - mx), jnp.exp(l1 - mx), jnp.exp(l2 - mx)
    o_ref[...] = ((e0 * scr[2][...] + e1 * scr[6][...] + e2 * scr[10][...]) / (e0 + e1 + e2)).astype(o_ref.dtype)


def dilated_mixer(x, B, S, positions, gain, w_qkv, q_gain, k_gain, w_out):
    T, D = x.shape
    nh = DIL_GROUPS * DIL_HEADS
    rot = DIL_HEAD_DIM // 4
    inv_freq = ROPE_THETA ** (-jnp.arange(0, rot, 2, dtype=F32) / rot)
    invf = jnp.concatenate([inv_freq, inv_freq, jnp.zeros((DIL_HEAD_DIM - rot,), F32)]).reshape(1, DIL_HEAD_DIM)
    pos = positions.astype(F32).reshape(T, 1)
    qkv = norm_matmul(x, gain, w_qkv)
    tp = 256
    qk_cols = 2 * nh * DIL_HEAD_DIM
    qk = pl.pallas_call(
        _dil_prep_body,
        grid=(T // tp,),
        in_specs=[
            pl.BlockSpec((tp, qk_cols), lambda i: (i, 0)),
            pl.BlockSpec((tp, 1), lambda i: (i, 0)),
            pl.BlockSpec((1, DIL_HEAD_DIM), lambda i: (0, 0)),
            pl.BlockSpec((DIL_GROUPS, DIL_HEAD_DIM), lambda i: (0, 0)),
            pl.BlockSpec((DIL_GROUPS, DIL_HEAD_DIM), lambda i: (0, 0)),
        ],
        out_specs=pl.BlockSpec((tp, qk_cols), lambda i: (i, 0)),
        out_shape=jax.ShapeDtypeStruct((T, qk_cols), F32),
        compiler_params=_cparams(("parallel",), 4 * tp * qk_cols * 4 + 2 * tp * V7X_LANES * 4),
        name="dil_qk_prep",
    )(qkv, pos, invf, q_gain, k_gain)

    ch = DIL_BLOCK * DIL_PATTERNS[-1][1]
    n_chunks = S // ch
    inputs, in_specs, scratch = [], [], []
    vmem = 2 * ch * DIL_HEAD_DIM * 2
    for g, (_, dil) in enumerate(DIL_PATTERNS):
        hist = DIL_BLOCK * dil
        per = ch // hist

        def cur_map(col):
            return lambda b, c, h: (b * n_chunks + c, col + h)

        def hist_map(col, per=per):
            return lambda b, c, h: (jnp.maximum((b * n_chunks + c) * per - 1, 0), col + h)

        inputs += [qk, qk, qkv, qk, qkv]
        in_specs += [
            pl.BlockSpec((ch, DIL_HEAD_DIM), cur_map(g * DIL_HEADS)),
            pl.BlockSpec((ch, DIL_HEAD_DIM), cur_map(nh + g * DIL_HEADS)),
            pl.BlockSpec((ch, DIL_HEAD_DIM), cur_map(2 * nh + g * DIL_HEADS)),
            pl.BlockSpec((hist, DIL_HEAD_DIM), hist_map(nh + g * DIL_HEADS)),
            pl.BlockSpec((hist, DIL_HEAD_DIM), hist_map(2 * nh + g * DIL_HEADS)),
        ]
        scratch += [pltpu.VMEM((hist + ch, DIL_HEAD_DIM), F32), pltpu.VMEM((hist + ch, DIL_HEAD_DIM), F32),
                    pltpu.VMEM((ch, DIL_HEAD_DIM), F32), pltpu.VMEM((ch, DIL_HEAD_DIM), F32)]
        vmem += (2 * (3 * ch + 2 * hist) + 2 * (hist + ch) + 2 * ch) * DIL_HEAD_DIM * 4
    o = pl.pallas_call(
        functools.partial(_dil_attn_body, n_chunks=n_chunks),
        grid=(B, n_chunks, DIL_HEADS),
        in_specs=in_specs,
        out_specs=pl.BlockSpec((ch, DIL_HEAD_DIM), lambda b, c, h: (b * n_chunks + c, h)),
        out_shape=jax.ShapeDtypeStruct((T, DIL_HEADS * DIL_HEAD_DIM), BF16),
        scratch_shapes=scratch,
        compiler_params=_cparams(("parallel", "arbitrary", "arbitrary"), vmem),
        name="dil_attention",
    )(*inputs)
    return matmul_residual(o, w_out, x)


def _hgrn_body(q_ref, f_ref, i_ref, gt_ref, lbl_ref, gain_ref, o_ref, st_scr, *, layer):
    @pl.when(pl.program_id(2) == 0)
    def _():
        st_scr[...] = jnp.zeros_like(st_scr)

    tt = q_ref.shape[0]
    dh = gain_ref.shape[1]
    heads = [slice(h * dh, (h + 1) * dh) for h in range(q_ref.shape[1] // dh)]
    lbl = lbl_ref[...]
    e = jnp.exp(lbl - jnp.max(lbl, axis=0, keepdims=True))
    p = e / jnp.sum(e, axis=0, keepdims=True)
    lb = jnp.sum(p[1:layer + 1, :], axis=0, keepdims=True)
    forget = lb + (1.0 - lb) * jax.nn.sigmoid(f_ref[...].astype(F32))
    k = 1.0 - forget
    gl = jnp.log(forget)
    a_cum, a_tot, tri = _chunk_sums(gl, HGRN_CHUNK)
    q_dec = (q_ref[...].astype(F32) * jnp.exp(a_cum)).astype(BF16)
    k_in = (k * jnp.exp(-a_cum)).astype(BF16)
    k_end = (k * jnp.exp(a_tot - a_cum)).astype(BF16)
    v = i_ref[...].astype(BF16)
    dec = jnp.exp(a_tot)
    att = [jnp.where(tri, _dot_nt(q_dec[:, hs], k_in[:, hs]), 0.0).astype(BF16) for hs in heads]
    o = [_dot(a, v[:, hs]) for a, hs in zip(att, heads)]
    chunks = [slice(c * HGRN_CHUNK, (c + 1) * HGRN_CHUNK) for c in range(tt // HGRN_CHUNK)]
    upd = [[_dot_tn(v[sl, hs], k_end[sl, hs]) for sl in chunks] for hs in heads]
    st = [st_scr[h] for h in range(len(heads))]
    inter = [[] for _ in heads]
    for ci, sl in enumerate(chunks):
        for h, hs in enumerate(heads):
            inter[h].append(_dot_nt(q_dec[sl, hs], st[h].astype(BF16)))
            st[h] = st[h] * dec[sl.start:sl.start + 1, hs] + upd[h][ci]
    gt = gt_ref[...].astype(F32)
    for h, hs in enumerate(heads):
        st_scr[h] = st[h]
        oh = o[h] + jnp.concatenate(inter[h], axis=0)
        o_ref[:, hs] = (_rms(oh, gain_ref[...]) * _silu(gt[:, hs])).astype(o_ref.dtype)


def hgrn_mixer(x, B, S, layer, gain, w_in, lb_logits, norm_gain, w_out, *, tt=256, heads_per_step=8):
    T, D = x.shape
    dh = D // HGRN_HEADS
    proj = norm_matmul(x, gain, w_in, out_dtype=BF16)
    tt = min(tt, S)
    nt = S // tt
    ng = HGRN_HEADS // heads_per_step
    wd = heads_per_step * dh

    def part(pidx):
        return pl.BlockSpec((tt, wd), lambda b, h, s: (b * nt + s, pidx * ng + h))

    o = pl.pallas_call(
        functools.partial(_hgrn_body, layer=layer),
        grid=(B, ng, nt),
        in_specs=[part(0), part(1), part(2), part(3),
                  pl.BlockSpec((DEPTH, wd), lambda b, h, s: (0, h)),
                  pl.BlockSpec((1, dh), lambda b, h, s: (0, 0))],
        out_specs=pl.BlockSpec((tt, wd), lambda b, h, s: (b * nt + s, h)),
        out_shape=jax.ShapeDtypeStruct((T, D), BF16),
        scratch_shapes=[pltpu.VMEM((heads_per_step, dh, dh), F32)],
        compiler_params=_cparams(("parallel", "parallel", "arbitrary"), 24 * tt * wd * 4 + 8 * tt * tt * 4),
        name="hgrn_core",
    )(proj, proj, proj, proj, lb_logits, norm_gain.reshape(1, dh))
    return matmul_residual(o, w_out, x)


def _rwkv_shift_mix(x_ref, xp_ref, gn_ref, first):
    gn = gn_ref[...]
    h = _rms(x_ref[...], gn)
    last = _rms(xp_ref[...], gn)[V7X_SUBLANES - 1:V7X_SUBLANES, :]
    last = jnp.where(first, 0.0, last)
    row = lax.broadcasted_iota(jnp.int32, h.shape, 0)
    return h, jnp.where(row == 0, last, pltpu.roll(h, 1, 0)) - h


def _rwkv_rkv_body(x_ref, xp_ref, gn_ref, mu_ref, wrkv_ref, rkv_ref, mix_scr, *, tiles_per_seq, n_col_tiles):
    n = pl.program_id(1)
    first = (pl.program_id(0) % tiles_per_seq) == 0

    @pl.when(n == 0)
    def _():
        gn, mu = gn_ref[...], mu_ref[...]
        last = jnp.where(first, 0.0, _rms(xp_ref[...], gn)[V7X_SUBLANES - 1:V7X_SUBLANES, :])
        sub = min(256, x_ref.shape[0])
        for r0 in range(0, x_ref.shape[0], sub):
            h = _rms(x_ref[r0:r0 + sub, :], gn)
            row = lax.broadcasted_iota(jnp.int32, h.shape, 0)
            d = jnp.where(row == 0, last, pltpu.roll(h, 1, 0)) - h
            for m in range(3):
                mix_scr[m, r0:r0 + sub, :] = (h + d * mu[m:m + 1, :]).astype(BF16)
            last = h[sub - 1:sub, :]

    rkv_ref[...] = _dot(mix_scr[n // n_col_tiles], wrkv_ref[...])


def _rwkv_lora_body(x_ref, xp_ref, gn_ref, mu_ref, w0_ref, w1_ref, w2_ref, a0_ref, a1_ref, a2_ref, g1_ref, g2_ref,
                    lw_ref, a_ref, g_ref, *, tiles_per_seq):
    h, d = _rwkv_shift_mix(x_ref, xp_ref, gn_ref, (pl.program_id(0) % tiles_per_seq) == 0)
    mu = mu_ref[...]
    xw = (h + d * mu[3:4, :]).astype(BF16)
    xa = (h + d * mu[4:5, :]).astype(BF16)
    xg = (h + d * mu[5:6, :]).astype(BF16)
    z = -(w0_ref[...] + _dot(jnp.tanh(_dot(xw, w1_ref[...])).astype(BF16), w2_ref[...]))
    softplus = jnp.maximum(z, 0.0) + jnp.log1p(jnp.exp(-jnp.abs(z)))
    lw_ref[...] = -jnp.exp(-softplus - 0.5)
    a_ref[...] = jax.nn.sigmoid(a0_ref[...] + _dot(_dot(xa, a1_ref[...]).astype(BF16), a2_ref[...])).astype(a_ref.dtype)
    g_ref[...] = _dot(jax.nn.sigmoid(_dot(xg, g1_ref[...])).astype(BF16), g2_ref[...]).astype(g_ref.dtype)


def _rwkv_core_body(r_ref, k_ref, v_ref, lw_ref, a_ref, g_ref, kk_ref, ka_ref, rk_ref, lnw_ref, lnb_ref,
                    o_ref, h_scr):
    @pl.when(pl.program_id(2) == 0)
    def _():
        h_scr[...] = jnp.zeros_like(h_scr)

    tt = r_ref.shape[0]
    C, N = RWKV_CHUNK, RWKV_HEAD_DIM
    lw = lw_ref[...]
    g_cum, g_tot, _ = _chunk_sums(lw, C)
    r, k, v, a = r_ref[...], k_ref[...], v_ref[...], a_ref[...].astype(F32)
    pairs = [slice(p * 2 * N, (p + 1) * 2 * N) for p in range(r.shape[1] // (2 * N))]
    left = lax.broadcasted_iota(jnp.int32, (tt, 2 * N), 1) < N

    def head_sum(t):
        return jnp.concatenate(
            [jnp.where(left, jnp.sum(jnp.where(left, t[:, ps], 0.0), -1, keepdims=True),
                       jnp.sum(jnp.where(left, 0.0, t[:, ps]), -1, keepdims=True)) for ps in pairs], axis=1)

    kk = k * kk_ref[...]
    kk = kk * lax.rsqrt(jnp.maximum(head_sum(kk * kk), 1e-24))
    kmod = k * (1.0 + (a - 1.0) * ka_ref[...])
    bv = kk * a
    e_neg = jnp.exp(-g_cum)
    e_end = jnp.exp(g_tot - g_cum)
    a_t = (-kk) * jnp.exp(g_cum - lw)
    r_t = r * jnp.exp(g_cum)
    k_t, b_t = kmod * e_neg, bv * e_neg
    k_h, b_h = kmod * e_end, bv * e_end
    dec = jnp.exp(g_tot)
    def key_lanes(t):
        out = []
        for ps in pairs:
            out += [jnp.where(left, t[:, ps], 0.0), jnp.where(left, pltpu.roll(t[:, ps], N, 1), 0.0)]
        return out

    a_k, r_k, bt_k, kt_k, bh_k, kh_k, dec_k = map(key_lanes, (a_t, r_t, b_t, k_t, b_h, k_h, dec))
    v_v = []
    for ps in pairs:
        v_v += [jnp.where(left, 0.0, pltpu.roll(v[:, ps], N, 1)), jnp.where(left, 0.0, v[:, ps])]
    n_heads = 2 * len(pairs)
    i2 = lax.broadcasted_iota(jnp.int32, (C, 2 * C), 0)
    lane2 = lax.broadcasted_iota(jnp.int32, (C, 2 * C), 1)
    lo = lane2 < C
    t2 = jnp.bitwise_and(lane2, C - 1)
    strict, incl = t2 < i2, t2 <= i2
    eye_hi = jnp.where(lane2 == i2 + C, 1.0, 0.0)
    zeros16 = jnp.zeros((C, 2 * C), BF16)

    nc = tt // C
    units = [(hh, c) for hh in range(n_heads) for c in range(nc)]

    def cut(per_head, u):
        return per_head[u[0]][u[1] * C:(u[1] + 1) * C, :]

    v_c = [cut(v_v, u).astype(BF16) for u in units]
    prod = [_dot_nt(jnp.concatenate([cut(a_k, u), cut(r_k, u)], axis=0).astype(BF16),
                    jnp.concatenate([cut(bt_k, u), cut(kt_k, u)], axis=0).astype(BF16))
            for u in units]
    n_abk = [jnp.where(strict, p[:C], 0.0) for p in prod]
    t_rbk = [jnp.where(incl, p[C:], 0.0).astype(BF16) for p in prod]
    n_lo = [jnp.where(lo, n, 0.0) for n in n_abk]
    s1 = [_dot(n.astype(BF16), jnp.concatenate([nl.astype(BF16), vv], axis=0))
          for n, nl, vv in zip(n_abk, n_lo, v_c)]
    z = [jnp.where(lo, s, 0.0) + pltpu.roll(nl, C, 1) + eye_hi for s, nl in zip(s1, n_lo)]
    for _ in range(5):
        z16 = [zz.astype(BF16) for zz in z]
        z = [_dot(zz16[:, :C], zz16) + jnp.where(lo, 0.0, zz) for zz, zz16 in zip(z, z16)]
    w0 = [(cut(a_k, u) + jnp.where(lo, 0.0, s)).astype(BF16) for u, s in zip(units, s1)]
    au = [_dot(zz.astype(BF16), jnp.concatenate([zeros16, w], axis=0)).astype(BF16)
          for zz, w in zip(z, w0)]
    auv = [jnp.concatenate([x, vv], axis=0) for x, vv in zip(au, v_c)]
    ry = [_dot(t, x) for t, x in zip(t_rbk, auv)]
    th = [_dot_tn(x, jnp.concatenate([cut(bh_k, u), cut(kh_k, u)], axis=0).astype(BF16))
          for x, u in zip(auv, units)]
    r_p = [(cut(r_k, u) + jnp.where(lo, y, 0.0)).astype(BF16) for u, y in zip(units, ry)]

    ht = [h_scr[hh] for hh in range(n_heads)]
    ys = [[] for _ in range(n_heads)]
    for c in range(nc):
        for hh in range(n_heads):
            i = hh * nc + c
            ht16 = ht[hh].astype(BF16)
            ys[hh].append(_dot_nt(r_p[i], ht16) + ry[i][:, C:])
            ht[hh] = (ht[hh] * dec_k[hh][c * C:c * C + 1, :] + _dot(ht16[:, :C], th[i][:C].astype(BF16))
                      + th[i][C:])
    y_heads = []
    for hh in range(n_heads):
        h_scr[hh] = ht[hh]
        y = jnp.concatenate(ys[hh], axis=0)
        mean = jnp.mean(y, -1, keepdims=True)
        var = jnp.mean(jnp.square(y - mean), -1, keepdims=True)
        y_heads.append((y - mean) * lax.rsqrt(var + RWKV_GN_EPS))
    yn = jnp.concatenate(y_heads, axis=1) * lnw_ref[...] + lnb_ref[...]
    bonus = head_sum(r * kmod * rk_ref[...]) * v
    o_ref[...] = ((yn + bonus) * g_ref[...].astype(F32)).astype(o_ref.dtype)


def rwkv_mixer(x, B, S, gain, mu, w_rkv, w0, w1, w2, a0, a1, a2, g1, g2, k_k, k_a, r_k, ln_w, ln_b, w_out,
               *, tm=1024, tn=1024, tl=256, tt=256, heads_per_step=4):
    T, D = x.shape
    tm, tl, tt = min(tm, S), min(tl, S), min(tt, S)
    nct = D // tn
    row = lambda i, n: (0, 0)

    def prev_rows(t):
        return lambda i, *_: (jnp.maximum(i * (t // V7X_SUBLANES) - 1, 0), 0)

    rkv = pl.pallas_call(
        functools.partial(_rwkv_rkv_body, tiles_per_seq=S // tm, n_col_tiles=nct),
        grid=(T // tm, 3 * nct),
        in_specs=[
            pl.BlockSpec((tm, D), lambda i, n: (i, 0)),
            pl.BlockSpec((V7X_SUBLANES, D), prev_rows(tm)),
            pl.BlockSpec((1, D), row),
            pl.BlockSpec((6, D), row),
            _wspec(w_rkv, (None, D, tn), lambda i, n: (n // nct, 0, n % nct)),
        ],
        out_specs=pl.BlockSpec((tm, tn), lambda i, n: (i, n)),
        out_shape=jax.ShapeDtypeStruct((T, 3 * D), F32),
        scratch_shapes=[pltpu.VMEM((3, tm, D), BF16)],
        compiler_params=_cparams(("parallel", "arbitrary"),
                                 2 * tm * D * 4 + 3 * tm * D * 2 + 2 * D * tn * 2 + 2 * tm * tn * 4 + 3 * tm * D * 4),
        name="rwkv_rkv",
    )(x, x, gain.reshape(1, D), mu, w_rkv[0])

    lora = w1.shape[1]
    pad = (-lora) % V7X_LANES
    w1p, a1p = jnp.pad(w1, ((0, 0), (0, pad))), jnp.pad(a1, ((0, 0), (0, pad)))
    w2p, a2p = jnp.pad(w2, ((0, pad), (0, 0))), jnp.pad(a2, ((0, pad), (0, 0)))
    lp, gl = lora + pad, g1.shape[1]
    one = lambda i: (0, 0)
    tok_l = pl.BlockSpec((tl, D), lambda i: (i, 0))
    lw, a, g = pl.pallas_call(
        functools.partial(_rwkv_lora_body, tiles_per_seq=S // tl),
        grid=(T // tl,),
        in_specs=[
            tok_l,
            pl.BlockSpec((V7X_SUBLANES, D), prev_rows(tl)),
            pl.BlockSpec((1, D), one),
            pl.BlockSpec((6, D), one),
            pl.BlockSpec((1, D), one), pl.BlockSpec((D, lp), one), pl.BlockSpec((lp, D), one),
            pl.BlockSpec((1, D), one), pl.BlockSpec((D, lp), one), pl.BlockSpec((lp, D), one),
            pl.BlockSpec((D, gl), one), pl.BlockSpec((gl, D), one),
        ],
        out_specs=[tok_l, tok_l, tok_l],
        out_shape=[jax.ShapeDtypeStruct((T, D), F32), jax.ShapeDtypeStruct((T, D), BF16),
                   jax.ShapeDtypeStruct((T, D), BF16)],
        compiler_params=_cparams(("parallel",), 14 * tl * D * 4 + 4 * (2 * D * lp + D * gl) * 2),
        name="rwkv_lora",
    )(x, x, gain.reshape(1, D), mu, w0.reshape(1, D), w1p, w2p, a0.reshape(1, D), a1p, a2p, g1, g2)

    nt = S // tt
    pw = heads_per_step * RWKV_HEAD_DIM
    npair = D // pw

    def tok(col0):
        return pl.BlockSpec((tt, pw), lambda b, p, s: (b * nt + s, col0 + p))

    par = pl.BlockSpec((1, pw), lambda b, p, s: (0, p))
    o = pl.pallas_call(
        _rwkv_core_body,
        grid=(B, npair, nt),
        in_specs=[tok(0), tok(npair), tok(2 * npair), tok(0), tok(0), tok(0), par, par, par, par, par],
        out_specs=pl.BlockSpec((tt, pw), lambda b, p, s: (b * nt + s, p)),
        out_shape=jax.ShapeDtypeStruct((T, D), BF16),
        scratch_shapes=[pltpu.VMEM((heads_per_step, RWKV_HEAD_DIM, 2 * RWKV_HEAD_DIM), F32)],
        compiler_params=_cparams(("parallel", "parallel", "arbitrary"), 40 * tt * pw * 4 + 8 * tt * tt * 4),
        name="rwkv_core",
    )(rkv, rkv, rkv, lw, a, g, k_k.reshape(1, D), k_a.reshape(1, D), r_k.reshape(1, D),
      ln_w.reshape(1, D), ln_b.reshape(1, D))
    return matmul_residual(o, w_out, x)


def kernel(x, mem, positions, ffn_norm, ffn_w_gate, ffn_w_up, ffn_w_down, mix_norm, xattn_norm, mem_norm, xattn_wq, xattn_wkv, xattn_wo, xattn_q_gain, xattn_k_gain, conv_w_in, conv_w, conv_w_out, dil_w_qkv, dil_q_gain, dil_k_gain, dil_w_out, hgrn_w_in, hgrn_lb_logits, hgrn_norm, hgrn_w_out, rwkv_mu, rwkv_w_rkv, rwkv_w0, rwkv_w1, rwkv_w2, rwkv_a0, rwkv_a1, rwkv_a2, rwkv_g1, rwkv_g2, rwkv_k_k, rwkv_k_a, rwkv_r_k, rwkv_ln_w, rwkv_ln_b, rwkv_w_out):
    B, S, D = x.shape
    assert D == D_MODEL and S % (DIL_BLOCK * DIL_PATTERNS[-1][1]) == 0
    depth = ffn_norm.shape[0]
    xf = x.reshape(B * S, D)
    memf = mem.reshape(B * MEM_LEN, D)
    bf = lambda w: w.astype(BF16)
    wq_all, wkv_all, wo_all = bf(xattn_wq), bf(xattn_wkv), bf(xattn_wo)
    conv_in_all, conv_out_all = bf(conv_w_in), bf(conv_w_out)
    dil_qkv_all, dil_out_all = bf(dil_w_qkv), bf(dil_w_out)
    hgrn_in_all, hgrn_out_all = bf(hgrn_w_in), bf(hgrn_w_out)
    rkv_all, rwkv_out_all = bf(rwkv_w_rkv), bf(rwkv_w_out)
    ffn_f32 = lambda idx: ((ffn_w_gate, idx), (ffn_w_up, idx), (ffn_w_down, idx))
    ffn_w = [(bf(w[0][0, 0]), ()) for w in ffn_f32((0, 0))]

    def ffn(xf, i, half):
        last = i == depth - 1 and half == 1
        nxt = None if last else ffn_f32((i, 1) if half == 0 else (i + 1, 0))
        return ffn_half(xf, ffn_norm[i, half], *ffn_w, cast_next=nxt)

    for i in range(depth):
        kind, j = i % N_MIXERS, i // N_MIXERS
        xf, ffn_w = ffn(xf, i, 0)
        if kind == 0:
            xf = conv_mixer(xf, S, mix_norm[i], (conv_in_all, (j,)), conv_w[j], (conv_out_all, (j,)))
        elif kind == 1:
            xf = dilated_mixer(xf, B, S, positions, mix_norm[i], (dil_qkv_all, (j,)), dil_q_gain[j], dil_k_gain[j],
                               (dil_out_all, (j,)))
        elif kind == 2:
            xf = hgrn_mixer(xf, B, S, i, mix_norm[i], (hgrn_in_all, (j,)), hgrn_lb_logits, hgrn_norm[j],
                            (hgrn_out_all, (j,)))
        else:
            xf = rwkv_mixer(xf, B, S, mix_norm[i], rwkv_mu[j], (rkv_all, (j,)), rwkv_w0[j], bf(rwkv_w1[j]),
                            bf(rwkv_w2[j]), rwkv_a0[j], bf(rwkv_a1[j]), bf(rwkv_a2[j]), bf(rwkv_g1[j]),
                            bf(rwkv_g2[j]), rwkv_k_k[j], rwkv_k_a[j], rwkv_r_k[j], rwkv_ln_w[j], rwkv_ln_b[j],
                            (rwkv_out_all, (j,)))
        xf = cross_attention(xf, S, memf, xattn_norm[i], mem_norm[i], (wq_all, (i,)), (wkv_all, (i,)),
                             (wo_all, (i,)), xattn_q_gain[i], xattn_k_gain[i])
        xf, ffn_w = ffn(xf, i, 1)
    return xf.reshape(B, S, D)
```

```python
import functools

import jax
import jax.numpy as jnp
from jax import lax
from jax.experimental import pallas as pl
from jax.experimental.pallas import tpu as pltpu

F32 = jnp.float32
BF16 = jnp.bfloat16

D_MODEL = 2048
DEPTH = 4
N_MIXERS = 4
MEM_LEN = 256
NORM_EPS = 1e-6
NEG_INF = -1e30
ROPE_THETA = 500000.0
DIL_PATTERNS = ((128, 1), (512, 4), (2048, 16))
DIL_GROUPS = 3
DIL_HEADS = 8
DIL_HEAD_DIM = 128
DIL_BLOCK = 128
DIL_PREP_ROWS = 256
DIL_UNITS_IN_FLIGHT = 16
HGRN_CHUNK = 16
HGRN_HEADS = 16
RWKV_HEAD_DIM = 64
RWKV_CHUNK = 64
RWKV_MIX_ROWS = 256
RWKV_GN_EPS = 64e-5
XATTN_HEADS = 4
XATTN_HEAD_DIM = 512

V7X_LANES = 128
V7X_SUBLANES = 8
V7X_VMEM_BYTES = 64 * 2**20
V7X_VMEM_CAP = V7X_VMEM_BYTES - 4 * 2**20


def _cparams(sem, vmem_bytes):
    limit = min(int(vmem_bytes * 1.25) + (4 << 20), V7X_VMEM_CAP)
    return pltpu.CompilerParams(dimension_semantics=sem, vmem_limit_bytes=limit)


def _rms(x, gain):
    return x * lax.rsqrt(jnp.mean(x * x, axis=-1, keepdims=True) + NORM_EPS) * gain


def _dot(a, b):
    return jnp.dot(a, b, preferred_element_type=F32)


def _dot_nt(a, b):
    return lax.dot_general(a, b, (((1,), (1,)), ((), ())), preferred_element_type=F32)


def _dot_tn(a, b):
    return lax.dot_general(a, b, (((0,), (0,)), ((), ())), preferred_element_type=F32)


def _wshape(w):
    arr, lead = w
    return arr.shape[len(lead):]


def _wspec(w, block, tail):
    lead = tuple(w[1])
    return pl.BlockSpec((None,) * len(lead) + tuple(block), lambda *g: lead + tuple(tail(*g)))


def _chunk_sums(x, chunk):
    t, w = x.shape
    rows = lax.broadcasted_iota(jnp.int32, (t, t), 0)
    cols = lax.broadcasted_iota(jnp.int32, (t, t), 1)
    same = _chunk_of(rows, chunk) == _chunk_of(cols, chunk)
    tri = same & (cols <= rows)
    sel = jnp.concatenate([tri.astype(BF16), same.astype(BF16)], axis=0)
    hi = x.astype(BF16)
    r1 = x - hi.astype(F32)
    mid = r1.astype(BF16)
    lo = (r1 - mid.astype(F32)).astype(BF16)
    s = _dot(sel, jnp.concatenate([hi, mid, lo], axis=1))
    s = s[:, :w] + s[:, w:2 * w] + s[:, 2 * w:]
    return s[:t], s[t:], tri


def _chunk_of(idx, chunk):
    return jnp.right_shift(idx, chunk.bit_length() - 1)


def _silu(x):
    return x * jax.nn.sigmoid(x)


def _norm_matmul_body(x_ref, g_ref, w_ref, o_ref, h_scr):
    @pl.when(pl.program_id(1) == 0)
    def _():
        h_scr[...] = _rms(x_ref[...], g_ref[...]).astype(BF16)

    o_ref[...] = _dot(h_scr[...], w_ref[...]).astype(o_ref.dtype)


def norm_matmul(x, gain, w, *, tm=1024, tn=1024, out_dtype=F32):
    M, K = x.shape
    N = _wshape(w)[1]
    tm, tn = min(tm, M), min(tn, N)
    ob = jnp.dtype(out_dtype).itemsize
    vmem = 2 * tm * K * 4 + tm * K * 2 + 2 * K * tn * 2 + 2 * tm * tn * ob
    return pl.pallas_call(
        _norm_matmul_body,
        grid=(M // tm, N // tn),
        in_specs=[
            pl.BlockSpec((tm, K), lambda i, j: (i, 0)),
            pl.BlockSpec((1, K), lambda i, j: (0, 0)),
            _wspec(w, (K, tn), lambda i, j: (0, j)),
        ],
        out_specs=pl.BlockSpec((tm, tn), lambda i, j: (i, j)),
        out_shape=jax.ShapeDtypeStruct((M, N), out_dtype),
        scratch_shapes=[pltpu.VMEM((tm, K), BF16)],
        compiler_params=_cparams(("parallel", "arbitrary"), vmem),
        name="norm_matmul",
    )(x, gain.reshape(1, K), w[0])


def _ffn_body(*refs, n_cast):
    x_ref, g_ref, wg_ref, wu_ref, wd_ref = refs[:5]
    src = refs[5:5 + n_cast]
    o_ref = refs[5 + n_cast]
    dst = refs[6 + n_cast:6 + 2 * n_cast]
    h_scr = refs[-1]

    @pl.when(pl.program_id(1) == 0)
    def _():
        x = x_ref[...]
        h_scr[...] = _rms(x, g_ref[...]).astype(BF16)
        o_ref[...] = x

    h = h_scr[...]
    act = _silu(_dot(h, wg_ref[...])) * _dot(h, wu_ref[...])
    o_ref[...] += 0.5 * _dot(act.astype(BF16), wd_ref[...])
    for s_ref, d_ref in zip(src, dst):
        d_ref[...] = s_ref[...].astype(BF16)


def ffn_half(x, gain, wg, wu, wd, cast_next=None, *, tm=1024, tf=512):
    M, D = x.shape
    F = _wshape(wg)[1]
    tm = min(tm, M)
    gm, gf = M // tm, F // tf
    vmem = 4 * tm * D * 4 + tm * D * 2 + 2 * 3 * D * tf * 2 + 3 * tm * tf * 4
    in_specs = [
        pl.BlockSpec((tm, D), lambda i, f: (i, 0)),
        pl.BlockSpec((1, D), lambda i, f: (0, 0)),
        _wspec(wg, (D, tf), lambda i, f: (0, f)),
        _wspec(wu, (D, tf), lambda i, f: (0, f)),
        _wspec(wd, (tf, D), lambda i, f: (f, 0)),
    ]
    out_specs = [pl.BlockSpec((tm, D), lambda i, f: (i, 0))]
    out_shape = [jax.ShapeDtypeStruct((M, D), F32)]
    cast_in = []
    if cast_next is not None:
        assert D % gm == 0 and F % gf == 0
        dm = D // gm
        for w, shape, block, tail in (
                (cast_next[0], (D, F), (dm, tf), lambda i, f: (i, f)),
                (cast_next[1], (D, F), (dm, tf), lambda i, f: (i, f)),
                (cast_next[2], (F, D), (tf, dm), lambda i, f: (f, i))):
            in_specs.append(_wspec(w, block, tail))
            out_specs.append(pl.BlockSpec(block, tail))
            out_shape.append(jax.ShapeDtypeStruct(shape, BF16))
            cast_in.append(w[0])
            vmem += 2 * block[0] * block[1] * 6
    outs = pl.pallas_call(
        functools.partial(_ffn_body, n_cast=len(cast_in)),
        grid=(gm, gf),
        in_specs=in_specs,
        out_specs=out_specs,
        out_shape=out_shape,
        scratch_shapes=[pltpu.VMEM((tm, D), BF16)],
        compiler_params=_cparams(("parallel", "arbitrary"), vmem),
        name="ffn_half",
    )(x, gain.reshape(1, D), wg[0], wu[0], wd[0], *cast_in)
    return outs[0], [(w, ()) for w in outs[1:]]


def _matmul_res_body(a_ref, w_ref, r_ref, o_ref):
    o_ref[...] = r_ref[...] + _dot(a_ref[...], w_ref[...])


def matmul_residual(a, w, res, *, tm=1024, tn=1024):
    M, K = a.shape
    N = _wshape(w)[1]
    tm = min(tm, M)
    vmem = 2 * tm * K * 2 + 2 * K * tn * 2 + 4 * tm * tn * 4
    return pl.pallas_call(
        _matmul_res_body,
        grid=(M // tm, N // tn),
        in_specs=[
            pl.BlockSpec((tm, K), lambda i, j: (i, 0)),
            _wspec(w, (K, tn), lambda i, j: (0, j)),
            pl.BlockSpec((tm, tn), lambda i, j: (i, j)),
        ],
        out_specs=pl.BlockSpec((tm, tn), lambda i, j: (i, j)),
        out_shape=jax.ShapeDtypeStruct((M, N), F32),
        compiler_params=_cparams(("parallel", "arbitrary"), vmem),
        name="matmul_residual",
    )(a, w[0], res)


def _prologue_matmul_res_body(prologue, n_in, *refs):
    in_refs = refs[:n_in]
    w_ref, r_ref, o_ref, lhs_scr = refs[n_in:]
    row_tile = pl.program_id(0)

    @pl.when(pl.program_id(1) == 0)
    def _():
        prologue(row_tile, *in_refs, lhs_scr)

    o_ref[...] = r_ref[...] + _dot(lhs_scr[...], w_ref[...])


def prologue_matmul_residual(prologue, inputs, in_specs, w, res, *, tm, tn, in_vmem, name):
    M, N = res.shape
    K = _wshape(w)[0]
    vmem = in_vmem + tm * K * 2 + 2 * K * tn * 2 + 4 * tm * tn * 4
    return pl.pallas_call(
        functools.partial(_prologue_matmul_res_body, prologue, len(inputs)),
        grid=(M // tm, N // tn),
        in_specs=list(in_specs) + [
            _wspec(w, (K, tn), lambda i, j: (0, j)),
            pl.BlockSpec((tm, tn), lambda i, j: (i, j)),
        ],
        out_specs=pl.BlockSpec((tm, tn), lambda i, j: (i, j)),
        out_shape=jax.ShapeDtypeStruct((M, N), F32),
        scratch_shapes=[pltpu.VMEM((tm, K), BF16)],
        compiler_params=_cparams(("parallel", "arbitrary"), vmem),
        name=name,
    )(*inputs, w[0], res)


def _conv_prologue(tiles_per_seq, row_tile, b_ref, c_ref, u_ref, cp_ref, up_ref, cw_ref, lhs_scr):
    cu = c_ref[...].astype(F32) * u_ref[...].astype(F32)
    prev = cp_ref[...].astype(F32) * up_ref[...].astype(F32)
    first = (row_tile % tiles_per_seq) == 0
    prev = jnp.where(first, 0.0, prev)
    n_prev = prev.shape[0]
    p1, p2 = prev[n_prev - 1:n_prev, :], prev[n_prev - 2:n_prev - 1, :]
    row = lax.broadcasted_iota(jnp.int32, cu.shape, 0)
    s1 = jnp.where(row == 0, p1, pltpu.roll(cu, 1, 0))
    s2 = jnp.where(row == 0, p2, jnp.where(row == 1, p1, pltpu.roll(cu, 2, 0)))
    w = cw_ref[...]
    y = w[0:1, :] * s2 + w[1:2, :] * s1 + w[2:3, :] * cu
    lhs_scr[...] = (b_ref[...].astype(F32) * y).astype(BF16)


def conv_mixer(x, S, gain, w_in, conv_w, w_out, *, tm=512):
    T, D = x.shape
    tm = min(tm, S)
    bcu = norm_matmul(x, gain, w_in, out_dtype=BF16)
    halo = 2 * V7X_SUBLANES
    rh = tm // halo

    def prev_map(col):
        return lambda i, j: (jnp.maximum(i * rh - 1, 0), col)

    in_specs = [
        pl.BlockSpec((tm, D), lambda i, j: (i, 0)),
        pl.BlockSpec((tm, D), lambda i, j: (i, 1)),
        pl.BlockSpec((tm, D), lambda i, j: (i, 2)),
        pl.BlockSpec((halo, D), prev_map(1)),
        pl.BlockSpec((halo, D), prev_map(2)),
        pl.BlockSpec((3, D), lambda i, j: (0, 0)),
    ]
    return prologue_matmul_residual(
        functools.partial(_conv_prologue, S // tm), (bcu, bcu, bcu, bcu, bcu, conv_w), in_specs, w_out, x,
        tm=tm, tn=1024, in_vmem=2 * 3 * tm * D * 2 + 4 * halo * D * 2 + 3 * tm * D * 4, name="conv_mixer_out")


def _xattn_body(x_ref, kv_ref, xg_ref, qg_ref, kg_ref, wq_ref, wo_ref, o_ref, attn_scr):
    scale = XATTN_HEAD_DIM ** -0.5
    D = XATTN_HEADS * XATTN_HEAD_DIM
    x = x_ref[...]
    q = _dot(_rms(x, xg_ref[...]).astype(BF16), wq_ref[...])
    for h in range(XATTN_HEADS):
        sl = slice(h * XATTN_HEAD_DIM, (h + 1) * XATTN_HEAD_DIM)
        qn = _rms(q[:, sl], qg_ref[...]).astype(BF16)
        kn = _rms(kv_ref[:, sl], kg_ref[...]).astype(BF16)
        v = kv_ref[:, D + h * XATTN_HEAD_DIM:D + (h + 1) * XATTN_HEAD_DIM].astype(BF16)
        s = _dot_nt(qn, kn) * scale
        p = jnp.exp(s - jnp.max(s, axis=-1, keepdims=True))
        l = jnp.sum(p, axis=-1, keepdims=True)
        attn_scr[:, sl] = (_dot(p.astype(BF16), v) / l).astype(BF16)
    o_ref[...] = x + _dot(attn_scr[...], wo_ref[...])


def cross_attention(x, S, mem, xgain, mgain, wq, wkv, wo, q_gain, k_gain, *, tm=512):
    T, D = x.shape
    tm = min(tm, S)
    kv = norm_matmul(mem, mgain, wkv)
    tps = S // tm
    once = pl.Buffered(1)
    vmem = 4 * tm * D * 4 + 2 * MEM_LEN * 2 * D * 4 + 2 * D * D * 2 + tm * D * 2 + 3 * tm * D * 4
    return pl.pallas_call(
        _xattn_body,
        grid=(T // tm,),
        in_specs=[
            pl.BlockSpec((tm, D), lambda i: (i, 0)),
            pl.BlockSpec((MEM_LEN, 2 * D), lambda i: (i // tps, 0)),
            pl.BlockSpec((1, D), lambda i: (0, 0)),
            pl.BlockSpec((1, XATTN_HEAD_DIM), lambda i: (0, 0)),
            pl.BlockSpec((1, XATTN_HEAD_DIM), lambda i: (0, 0)),
            pl.BlockSpec((None,) * len(wq[1]) + (D, D), lambda i: tuple(wq[1]) + (0, 0), pipeline_mode=once),
            pl.BlockSpec((None,) * len(wo[1]) + (D, D), lambda i: tuple(wo[1]) + (0, 0), pipeline_mode=once),
        ],
        out_specs=pl.BlockSpec((tm, D), lambda i: (i, 0)),
        out_shape=jax.ShapeDtypeStruct((T, D), F32),
        scratch_shapes=[pltpu.VMEM((tm, D), BF16)],
        compiler_params=_cparams(("parallel",), vmem),
        name="xattn",
    )(x, kv, xgain.reshape(1, D), q_gain.reshape(1, -1), k_gain.reshape(1, -1), wq[0], wo[0])


def _dil_prep_body(x_ref, pos_ref, invf_ref, qg_ref, kg_ref, o_ref):
    ang = pos_ref[...] * invf_ref[...]
    lane = lax.broadcasted_iota(jnp.int32, ang.shape, 1)
    half = DIL_HEAD_DIM // 8
    cos, sin = jnp.cos(ang), jnp.sin(ang)
    sin_lo = jnp.where(lane < half, -sin, 0.0)
    sin_hi = jnp.where((lane >= half) & (lane < 2 * half), sin, 0.0)
    r = lax.broadcasted_iota(jnp.int32, (2 * DIL_HEAD_DIM, 2 * DIL_HEAD_DIM), 0)
    c = lax.broadcasted_iota(jnp.int32, (2 * DIL_HEAD_DIM, 2 * DIL_HEAD_DIM), 1)
    mean_mat = jnp.where((r < DIL_HEAD_DIM) == (c < DIL_HEAD_DIM), 1.0 / DIL_HEAD_DIM, 0.0).astype(BF16)
    for part, g_ref in ((0, qg_ref), (1, kg_ref)):
        for g in range(DIL_GROUPS):
            gain = g_ref[g:g + 1, :]
            for pair in range(DIL_HEADS // 2):
                col = ((part * DIL_GROUPS + g) * DIL_HEADS + 2 * pair) * DIL_HEAD_DIM
                x2 = x_ref[:, col:col + 2 * DIL_HEAD_DIM]
                inv = lax.rsqrt(_dot((x2 * x2).astype(BF16), mean_mat) + NORM_EPS)
                for hh in range(2):
                    sl = slice(hh * DIL_HEAD_DIM, (hh + 1) * DIL_HEAD_DIM)
                    xn = x2[:, sl] * inv[:, sl] * gain
                    o_ref[:, col + sl.start:col + sl.stop] = (
                        xn * cos + pltpu.roll(xn, DIL_HEAD_DIM - half, 1) * sin_lo
                        + pltpu.roll(xn, half, 1) * sin_hi)


def _dil_attn_body(*refs):
    ins, o_ref, scr = refs[:15], refs[15], refs[16:]
    c = pl.program_id(1)
    scale = DIL_HEAD_DIM ** -0.5
    ii = lax.broadcasted_iota(jnp.int32, (DIL_BLOCK, DIL_BLOCK), 0)
    jj = lax.broadcasted_iota(jnp.int32, (DIL_BLOCK, DIL_BLOCK), 1)
    cur_mask = jj <= ii
    prev_mask = jj >= ii
    ch = o_ref.shape[0]
    for g, (_, dil) in enumerate(DIL_PATTERNS):
        q_ref, k_ref, v_ref, kh_ref, vh_ref = ins[5 * g:5 * g + 5]
        kf, vf, og, lg = scr[4 * g:4 * g + 4]
        hist = DIL_BLOCK * dil
        kf[0:hist, :] = kh_ref[...]
        vf[0:hist, :] = vh_ref[...]
        kf[hist:hist + ch, :] = k_ref[...]
        vf[hist:hist + ch, :] = v_ref[...]
        def rows(start):
            return pl.ds(start, DIL_BLOCK, stride=dil) if dil > 1 else pl.ds(start, DIL_BLOCK)

        units = [(blk, blk * hist + r) for blk in range(ch // hist) for r in range(dil)]
        for b0 in range(0, len(units), DIL_UNITS_IN_FLIGHT):
            batch = units[b0:b0 + DIL_UNITS_IN_FLIGHT]
            qv = [q_ref[rows(q0), :].astype(BF16) for _, q0 in batch]
            sc = [_dot_nt(q, kf[rows(hist + q0), :].astype(BF16)) for q, (_, q0) in zip(qv, batch)]
            sp = [_dot_nt(q, kf[rows(q0), :].astype(BF16)) for q, (_, q0) in zip(qv, batch)]
            sc = [jnp.where(cur_mask, s * scale, NEG_INF) for s in sc]
            sp = [jnp.where(prev_mask if blk > 0 else prev_mask & (c > 0), s * scale, NEG_INF)
                  for s, (blk, _) in zip(sp, batch)]
            m = [jnp.maximum(jnp.max(a, -1, keepdims=True), jnp.max(b, -1, keepdims=True)) for a, b in zip(sc, sp)]
            pc = [jnp.exp(a - mm) for a, mm in zip(sc, m)]
            pp = [jnp.exp(b - mm) for b, mm in zip(sp, m)]
            l = [jnp.sum(a, -1, keepdims=True) + jnp.sum(b, -1, keepdims=True) for a, b in zip(pc, pp)]
            o = [_dot(a.astype(BF16), vf[rows(hist + q0), :].astype(BF16))
                 + _dot(b.astype(BF16), vf[rows(q0), :].astype(BF16)) for a, b, (_, q0) in zip(pc, pp, batch)]
            for oo, ll, mm, (_, q0) in zip(o, l, m, batch):
                og[rows(q0), :] = oo / ll
                lg[rows(q0), :] = jnp.broadcast_to(mm + jnp.log(ll), (DIL_BLOCK, DIL_HEAD_DIM))
    l0, l1, l2 = scr[3][...], scr[7][...], scr[11][...]
    mx = jnp.maximum(jnp.maximum(l0, l1), l2)
    e0, e1, e2 = jnp.exp(l0 - mx), jnp.exp(l1 - mx), jnp.exp(l2 - mx)
    o_ref[...] = ((e0 * scr[2][...] + e1 * scr[6][...] + e2 * scr[10][...]) / (e0 + e1 + e2)).astype(o_ref.dtype)


def dilated_mixer(x, B, S, positions, gain, w_qkv, q_gain, k_gain, w_out):
    T, D = x.shape
    nh = DIL_GROUPS * DIL_HEADS
    rot = DIL_HEAD_DIM // 4
    inv_freq = ROPE_THETA ** (-jnp.arange(0, rot, 2, dtype=F32) / rot)
    invf = jnp.concatenate([inv_freq, inv_freq, jnp.zeros((DIL_HEAD_DIM - rot,), F32)]).reshape(1, DIL_HEAD_DIM)
    pos = positions.astype(F32).reshape(T, 1)
    qkv = norm_matmul(x, gain, w_qkv)
    tp = DIL_PREP_ROWS
    qk_cols = 2 * nh * DIL_HEAD_DIM
    qk = pl.pallas_call(
        _dil_prep_body,
        grid=(T // tp,),
        in_specs=[
            pl.BlockSpec((tp, qk_cols), lambda i: (i, 0)),
            pl.BlockSpec((tp, 1), lambda i: (i, 0)),
            pl.BlockSpec((1, DIL_HEAD_DIM), lambda i: (0, 0)),
            pl.BlockSpec((DIL_GROUPS, DIL_HEAD_DIM), lambda i: (0, 0)),
            pl.BlockSpec((DIL_GROUPS, DIL_HEAD_DIM), lambda i: (0, 0)),
        ],
        out_specs=pl.BlockSpec((tp, qk_cols), lambda i: (i, 0)),
        out_shape=jax.ShapeDtypeStruct((T, qk_cols), F32),
        compiler_params=_cparams(("parallel",), 4 * tp * qk_cols * 4 + 2 * tp * V7X_LANES * 4),
        name="dil_qk_prep",
    )(qkv, pos, invf, q_gain, k_gain)

    ch = DIL_BLOCK * DIL_PATTERNS[-1][1]
    n_chunks = S // ch
    inputs, in_specs, scratch = [], [], []
    vmem = 2 * ch * DIL_HEAD_DIM * 2
    for g, (_, dil) in enumerate(DIL_PATTERNS):
        hist = DIL_BLOCK * dil
        per = ch // hist

        def cur_map(col):
            return lambda b, c, h: (b * n_chunks + c, col + h)

        def hist_map(col, per=per):
            return lambda b, c, h: (jnp.maximum((b * n_chunks + c) * per - 1, 0), col + h)

        inputs += [qk, qk, qkv, qk, qkv]
        in_specs += [
            pl.BlockSpec((ch, DIL_HEAD_DIM), cur_map(g * DIL_HEADS)),
            pl.BlockSpec((ch, DIL_HEAD_DIM), cur_map(nh + g * DIL_HEADS)),
            pl.BlockSpec((ch, DIL_HEAD_DIM), cur_map(2 * nh + g * DIL_HEADS)),
            pl.BlockSpec((hist, DIL_HEAD_DIM), hist_map(nh + g * DIL_HEADS)),
            pl.BlockSpec((hist, DIL_HEAD_DIM), hist_map(2 * nh + g * DIL_HEADS)),
        ]
        scratch += [pltpu.VMEM((hist + ch, DIL_HEAD_DIM), F32), pltpu.VMEM((hist + ch, DIL_HEAD_DIM), F32),
                    pltpu.VMEM((ch, DIL_HEAD_DIM), F32), pltpu.VMEM((ch, DIL_HEAD_DIM), F32)]
        vmem += (2 * (3 * ch + 2 * hist) + 2 * (hist + ch) + 2 * ch) * DIL_HEAD_DIM * 4
    o = pl.pallas_call(
        _dil_attn_body,
        grid=(B, n_chunks, DIL_HEADS),
        in_specs=in_specs,
        out_specs=pl.BlockSpec((ch, DIL_HEAD_DIM), lambda b, c, h: (b * n_chunks + c, h)),
        out_shape=jax.ShapeDtypeStruct((T, DIL_HEADS * DIL_HEAD_DIM), BF16),
        scratch_shapes=scratch,
        compiler_params=_cparams(("parallel", "arbitrary", "arbitrary"), vmem),
        name="dil_attention",
    )(*inputs)
    return matmul_residual(o, w_out, x)


def _hgrn_body(q_ref, f_ref, i_ref, gt_ref, lbl_ref, gain_ref, o_ref, st_scr, *, layer):
    @pl.when(pl.program_id(2) == 0)
    def _():
        st_scr[...] = jnp.zeros_like(st_scr)

    tt = q_ref.shape[0]
    dh = gain_ref.shape[1]
    heads = [slice(h * dh, (h + 1) * dh) for h in range(q_ref.shape[1] // dh)]
    lbl = lbl_ref[...]
    e = jnp.exp(lbl - jnp.max(lbl, axis=0, keepdims=True))
    p = e / jnp.sum(e, axis=0, keepdims=True)
    lb = jnp.sum(p[1:layer + 1, :], axis=0, keepdims=True)
    forget = lb + (1.0 - lb) * jax.nn.sigmoid(f_ref[...].astype(F32))
    k = 1.0 - forget
    gl = jnp.log(forget)
    a_cum, a_tot, tri = _chunk_sums(gl, HGRN_CHUNK)
    q_dec = (q_ref[...].astype(F32) * jnp.exp(a_cum)).astype(BF16)
    k_in = (k * jnp.exp(-a_cum)).astype(BF16)
    k_end = (k * jnp.exp(a_tot - a_cum)).astype(BF16)
    v = i_ref[...]
    dec = jnp.exp(a_tot)
    att = [jnp.where(tri, _dot_nt(q_dec[:, hs], k_in[:, hs]), 0.0).astype(BF16) for hs in heads]
    o = [_dot(a, v[:, hs]) for a, hs in zip(att, heads)]
    chunks = [slice(c * HGRN_CHUNK, (c + 1) * HGRN_CHUNK) for c in range(tt // HGRN_CHUNK)]
    upd = [[_dot_tn(v[sl, hs], k_end[sl, hs]) for sl in chunks] for hs in heads]
    st = [st_scr[h] for h in range(len(heads))]
    inter = [[] for _ in heads]
    for ci, sl in enumerate(chunks):
        for h, hs in enumerate(heads):
            inter[h].append(_dot_nt(q_dec[sl, hs], st[h].astype(BF16)))
            st[h] = st[h] * dec[sl.start:sl.start + 1, hs] + upd[h][ci]
    gt = gt_ref[...].astype(F32)
    for h, hs in enumerate(heads):
        st_scr[h] = st[h]
        oh = o[h] + jnp.concatenate(inter[h], axis=0)
        o_ref[:, hs] = (_rms(oh, gain_ref[...]) * _silu(gt[:, hs])).astype(o_ref.dtype)


def hgrn_mixer(x, B, S, layer, gain, w_in, lb_logits, norm_gain, w_out, *, tt=256, heads_per_step=8):
    T, D = x.shape
    dh = D // HGRN_HEADS
    proj = norm_matmul(x, gain, w_in, out_dtype=BF16)
    tt = min(tt, S)
    nt = S // tt
    ng = HGRN_HEADS // heads_per_step
    wd = heads_per_step * dh

    def part(pidx):
        return pl.BlockSpec((tt, wd), lambda b, h, s: (b * nt + s, pidx * ng + h))

    o = pl.pallas_call(
        functools.partial(_hgrn_body, layer=layer),
        grid=(B, ng, nt),
        in_specs=[part(0), part(1), part(2), part(3),
                  pl.BlockSpec((DEPTH, wd), lambda b, h, s: (0, h)),
                  pl.BlockSpec((1, dh), lambda b, h, s: (0, 0))],
        out_specs=pl.BlockSpec((tt, wd), lambda b, h, s: (b * nt + s, h)),
        out_shape=jax.ShapeDtypeStruct((T, D), BF16),
        scratch_shapes=[pltpu.VMEM((heads_per_step, dh, dh), F32)],
        compiler_params=_cparams(("parallel", "parallel", "arbitrary"), 24 * tt * wd * 4 + 8 * tt * tt * 4),
        name="hgrn_core",
    )(proj, proj, proj, proj, lb_logits, norm_gain.reshape(1, dh))
    return matmul_residual(o, w_out, x)


def _rwkv_shift_mix(x_ref, xp_ref, gn_ref, first):
    gn = gn_ref[...]
    h = _rms(x_ref[...], gn)
    last = _rms(xp_ref[...], gn)[V7X_SUBLANES - 1:V7X_SUBLANES, :]
    last = jnp.where(first, 0.0, last)
    row = lax.broadcasted_iota(jnp.int32, h.shape, 0)
    return h, jnp.where(row == 0, last, pltpu.roll(h, 1, 0)) - h


def _rwkv_rkv_body(x_ref, xp_ref, gn_ref, mu_ref, wrkv_ref, rkv_ref, mix_scr, *, tiles_per_seq, n_col_tiles):
    n = pl.program_id(1)
    first = (pl.program_id(0) % tiles_per_seq) == 0

    @pl.when(n == 0)
    def _():
        gn, mu = gn_ref[...], mu_ref[...]
        last = jnp.where(first, 0.0, _rms(xp_ref[...], gn)[V7X_SUBLANES - 1:V7X_SUBLANES, :])
        sub = min(RWKV_MIX_ROWS, x_ref.shape[0])
        for r0 in range(0, x_ref.shape[0], sub):
            h = _rms(x_ref[r0:r0 + sub, :], gn)
            row = lax.broadcasted_iota(jnp.int32, h.shape, 0)
            d = jnp.where(row == 0, last, pltpu.roll(h, 1, 0)) - h
            for m in range(3):
                mix_scr[m, r0:r0 + sub, :] = (h + d * mu[m:m + 1, :]).astype(BF16)
            last = h[sub - 1:sub, :]

    rkv_ref[...] = _dot(mix_scr[n // n_col_tiles], wrkv_ref[...])


def _rwkv_lora_body(x_ref, xp_ref, gn_ref, mu_ref, w0_ref, w1_ref, w2_ref, a0_ref, a1_ref, a2_ref, g1_ref, g2_ref,
                    lw_ref, a_ref, g_ref, *, tiles_per_seq):
    h, d = _rwkv_shift_mix(x_ref, xp_ref, gn_ref, (pl.program_id(0) % tiles_per_seq) == 0)
    mu = mu_ref[...]
    xw = (h + d * mu[3:4, :]).astype(BF16)
    xa = (h + d * mu[4:5, :]).astype(BF16)
    xg = (h + d * mu[5:6, :]).astype(BF16)
    z = -(w0_ref[...] + _dot(jnp.tanh(_dot(xw, w1_ref[...])).astype(BF16), w2_ref[...]))
    softplus = jnp.maximum(z, 0.0) + jnp.log1p(jnp.exp(-jnp.abs(z)))
    lw_ref[...] = -jnp.exp(-softplus - 0.5)
    a_ref[...] = jax.nn.sigmoid(a0_ref[...] + _dot(_dot(xa, a1_ref[...]).astype(BF16), a2_ref[...])).astype(a_ref.dtype)
    g_ref[...] = _dot(jax.nn.sigmoid(_dot(xg, g1_ref[...])).astype(BF16), g2_ref[...]).astype(g_ref.dtype)


def _rwkv_core_body(r_ref, k_ref, v_ref, lw_ref, a_ref, g_ref, kk_ref, ka_ref, rk_ref, lnw_ref, lnb_ref,
                    o_ref, h_scr):
    @pl.when(pl.program_id(2) == 0)
    def _():
        h_scr[...] = jnp.zeros_like(h_scr)

    tt = r_ref.shape[0]
    C, N = RWKV_CHUNK, RWKV_HEAD_DIM
    lw = lw_ref[...]
    g_cum, g_tot, _ = _chunk_sums(lw, C)
    r, k, v, a = r_ref[...], k_ref[...], v_ref[...], a_ref[...].astype(F32)
    pairs = [slice(p * 2 * N, (p + 1) * 2 * N) for p in range(r.shape[1] // (2 * N))]
    left = lax.broadcasted_iota(jnp.int32, (tt, 2 * N), 1) < N

    def head_sum(t):
        return jnp.concatenate(
            [jnp.where(left, jnp.sum(jnp.where(left, t[:, ps], 0.0), -1, keepdims=True),
                       jnp.sum(jnp.where(left, 0.0, t[:, ps]), -1, keepdims=True)) for ps in pairs], axis=1)

    kk = k * kk_ref[...]
    kk = kk * lax.rsqrt(jnp.maximum(head_sum(kk * kk), 1e-24))
    kmod = k * (1.0 + (a - 1.0) * ka_ref[...])
    bv = kk * a
    e_neg = jnp.exp(-g_cum)
    e_end = jnp.exp(g_tot - g_cum)
    a_t = (-kk) * jnp.exp(g_cum - lw)
    r_t = r * jnp.exp(g_cum)
    k_t, b_t = kmod * e_neg, bv * e_neg
    k_h, b_h = kmod * e_end, bv * e_end
    dec = jnp.exp(g_tot)
    def key_lanes(t):
        out = []
        for ps in pairs:
            out += [jnp.where(left, t[:, ps], 0.0), jnp.where(left, pltpu.roll(t[:, ps], N, 1), 0.0)]
        return out

    a_k, r_k, bt_k, kt_k, bh_k, kh_k, dec_k = map(key_lanes, (a_t, r_t, b_t, k_t, b_h, k_h, dec))
    v_v = []
    for ps in pairs:
        v_v += [jnp.where(left, 0.0, pltpu.roll(v[:, ps], N, 1)), jnp.where(left, 0.0, v[:, ps])]
    n_heads = 2 * len(pairs)
    i2 = lax.broadcasted_iota(jnp.int32, (C, 2 * C), 0)
    lane2 = lax.broadcasted_iota(jnp.int32, (C, 2 * C), 1)
    lo = lane2 < C
    t2 = jnp.bitwise_and(lane2, C - 1)
    strict, incl = t2 < i2, t2 <= i2
    eye_hi = jnp.where(lane2 == i2 + C, 1.0, 0.0)
    zeros16 = jnp.zeros((C, 2 * C), BF16)

    nc = tt // C
    units = [(hh, c) for hh in range(n_heads) for c in range(nc)]

    def cut(per_head, u):
        return per_head[u[0]][u[1] * C:(u[1] + 1) * C, :]

    v_c = [cut(v_v, u).astype(BF16) for u in units]
    prod = [_dot_nt(jnp.concatenate([cut(a_k, u), cut(r_k, u)], axis=0).astype(BF16),
                    jnp.concatenate([cut(bt_k, u), cut(kt_k, u)], axis=0).astype(BF16))
            for u in units]
    n_abk = [jnp.where(strict, p[:C], 0.0) for p in prod]
    t_rbk = [jnp.where(incl, p[C:], 0.0).astype(BF16) for p in prod]
    n_lo = [jnp.where(lo, n, 0.0) for n in n_abk]
    s1 = [_dot(n.astype(BF16), jnp.concatenate([nl.astype(BF16), vv], axis=0))
          for n, nl, vv in zip(n_abk, n_lo, v_c)]
    z = [jnp.where(lo, s, 0.0) + pltpu.roll(nl, C, 1) + eye_hi for s, nl in zip(s1, n_lo)]
    for _ in range(5):
        z16 = [zz.astype(BF16) for zz in z]
        z = [_dot(zz16[:, :C], zz16) + jnp.where(lo, 0.0, zz) for zz, zz16 in zip(z, z16)]
    w0 = [(cut(a_k, u) + jnp.where(lo, 0.0, s)).astype(BF16) for u, s in zip(units, s1)]
    au = [_dot(zz.astype(BF16), jnp.concatenate([zeros16, w], axis=0)).astype(BF16)
          for zz, w in zip(z, w0)]
    auv = [jnp.concatenate([x, vv], axis=0) for x, vv in zip(au, v_c)]
    ry = [_dot(t, x) for t, x in zip(t_rbk, auv)]
    th = [_dot_tn(x, jnp.concatenate([cut(bh_k, u), cut(kh_k, u)], axis=0).astype(BF16))
          for x, u in zip(auv, units)]
    r_p = [(cut(r_k, u) + jnp.where(lo, y, 0.0)).astype(BF16) for u, y in zip(units, ry)]

    ht = [h_scr[hh] for hh in range(n_heads)]
    ys = [[] for _ in range(n_heads)]
    for c in range(nc):
        for hh in range(n_heads):
            i = hh * nc + c
            ht16 = ht[hh].astype(BF16)
            ys[hh].append(_dot_nt(r_p[i], ht16) + ry[i][:, C:])
            ht[hh] = (ht[hh] * dec_k[hh][c * C:c * C + 1, :] + _dot(ht16[:, :C], th[i][:C].astype(BF16))
                      + th[i][C:])
    y_heads = []
    for hh in range(n_heads):
        h_scr[hh] = ht[hh]
        y = jnp.concatenate(ys[hh], axis=0)
        mean = jnp.mean(y, -1, keepdims=True)
        var = jnp.mean(jnp.square(y - mean), -1, keepdims=True)
        y_heads.append((y - mean) * lax.rsqrt(var + RWKV_GN_EPS))
    yn = jnp.concatenate(y_heads, axis=1) * lnw_ref[...] + lnb_ref[...]
    bonus = head_sum(r * kmod * rk_ref[...]) * v
    o_ref[...] = ((yn + bonus) * g_ref[...].astype(F32)).astype(o_ref.dtype)


def rwkv_mixer(x, B, S, gain, mu, w_rkv, w0, w1, w2, a0, a1, a2, g1, g2, k_k, k_a, r_k, ln_w, ln_b, w_out,
               *, tm=1024, tn=1024, tl=256, tt=256, heads_per_step=8):
    T, D = x.shape
    tm, tl, tt = min(tm, S), min(tl, S), min(tt, S)
    nct = D // tn
    row = lambda i, n: (0, 0)

    def prev_rows(t):
        return lambda i, *_: (jnp.maximum(i * (t // V7X_SUBLANES) - 1, 0), 0)

    rkv = pl.pallas_call(
        functools.partial(_rwkv_rkv_body, tiles_per_seq=S // tm, n_col_tiles=nct),
        grid=(T // tm, 3 * nct),
        in_specs=[
            pl.BlockSpec((tm, D), lambda i, n: (i, 0)),
            pl.BlockSpec((V7X_SUBLANES, D), prev_rows(tm)),
            pl.BlockSpec((1, D), row),
            pl.BlockSpec((6, D), row),
            _wspec(w_rkv, (None, D, tn), lambda i, n: (n // nct, 0, n % nct)),
        ],
        out_specs=pl.BlockSpec((tm, tn), lambda i, n: (i, n)),
        out_shape=jax.ShapeDtypeStruct((T, 3 * D), F32),
        scratch_shapes=[pltpu.VMEM((3, tm, D), BF16)],
        compiler_params=_cparams(("parallel", "arbitrary"),
                                 2 * tm * D * 4 + 3 * tm * D * 2 + 2 * D * tn * 2 + 2 * tm * tn * 4 + 3 * tm * D * 4),
        name="rwkv_rkv",
    )(x, x, gain.reshape(1, D), mu, w_rkv[0])

    lora = w1.shape[1]
    pad = (-lora) % V7X_LANES
    w1p, a1p = jnp.pad(w1, ((0, 0), (0, pad))), jnp.pad(a1, ((0, 0), (0, pad)))
    w2p, a2p = jnp.pad(w2, ((0, pad), (0, 0))), jnp.pad(a2, ((0, pad), (0, 0)))
    lp, gl = lora + pad, g1.shape[1]
    one = lambda i: (0, 0)
    tok_l = pl.BlockSpec((tl, D), lambda i: (i, 0))
    lw, a, g = pl.pallas_call(
        functools.partial(_rwkv_lora_body, tiles_per_seq=S // tl),
        grid=(T // tl,),
        in_specs=[
            tok_l,
            pl.BlockSpec((V7X_SUBLANES, D), prev_rows(tl)),
            pl.BlockSpec((1, D), one),
            pl.BlockSpec((6, D), one),
            pl.BlockSpec((1, D), one), pl.BlockSpec((D, lp), one), pl.BlockSpec((lp, D), one),
            pl.BlockSpec((1, D), one), pl.BlockSpec((D, lp), one), pl.BlockSpec((lp, D), one),
            pl.BlockSpec((D, gl), one), pl.BlockSpec((gl, D), one),
        ],
        out_specs=[tok_l, tok_l, tok_l],
        out_shape=[jax.ShapeDtypeStruct((T, D), F32), jax.ShapeDtypeStruct((T, D), BF16),
                   jax.ShapeDtypeStruct((T, D), BF16)],
        compiler_params=_cparams(("parallel",), 14 * tl * D * 4 + 4 * (2 * D * lp + D * gl) * 2),
        name="rwkv_lora",
    )(x, x, gain.reshape(1, D), mu, w0.reshape(1, D), w1p, w2p, a0.reshape(1, D), a1p, a2p, g1, g2)

    nt = S // tt
    pw = heads_per_step * RWKV_HEAD_DIM
    npair = D // pw

    def tok(col0):
        return pl.BlockSpec((tt, pw), lambda b, p, s: (b * nt + s, col0 + p))

    par = pl.BlockSpec((1, pw), lambda b, p, s: (0, p))
    o = pl.pallas_call(
        _rwkv_core_body,
        grid=(B, npair, nt),
        in_specs=[tok(0), tok(npair), tok(2 * npair), tok(0), tok(0), tok(0), par, par, par, par, par],
        out_specs=pl.BlockSpec((tt, pw), lambda b, p, s: (b * nt + s, p)),
        out_shape=jax.ShapeDtypeStruct((T, D), BF16),
        scratch_shapes=[pltpu.VMEM((heads_per_step, RWKV_HEAD_DIM, 2 * RWKV_HEAD_DIM), F32)],
        compiler_params=_cparams(("parallel", "parallel", "arbitrary"), 40 * tt * pw * 4 + 8 * tt * tt * 4),
        name="rwkv_core",
    )(rkv, rkv, rkv, lw, a, g, k_k.reshape(1, D), k_a.reshape(1, D), r_k.reshape(1, D),
      ln_w.reshape(1, D), ln_b.reshape(1, D))
    return matmul_residual(o, w_out, x)


def kernel(x, mem, positions, ffn_norm, ffn_w_gate, ffn_w_up, ffn_w_down, mix_norm, xattn_norm, mem_norm, xattn_wq, xattn_wkv, xattn_wo, xattn_q_gain, xattn_k_gain, conv_w_in, conv_w, conv_w_out, dil_w_qkv, dil_q_gain, dil_k_gain, dil_w_out, hgrn_w_in, hgrn_lb_logits, hgrn_norm, hgrn_w_out, rwkv_mu, rwkv_w_rkv, rwkv_w0, rwkv_w1, rwkv_w2, rwkv_a0, rwkv_a1, rwkv_a2, rwkv_g1, rwkv_g2, rwkv_k_k, rwkv_k_a, rwkv_r_k, rwkv_ln_w, rwkv_ln_b, rwkv_w_out):
    B, S, D = x.shape
    assert D == D_MODEL and S % (DIL_BLOCK * DIL_PATTERNS[-1][1]) == 0
    depth = ffn_norm.shape[0]
    xf = x.reshape(B * S, D)
    memf = mem.reshape(B * MEM_LEN, D)
    bf = lambda w: w.astype(BF16)
    wq_all, wkv_all, wo_all = bf(xattn_wq), bf(xattn_wkv), bf(xattn_wo)
    conv_in_all, conv_out_all = bf(conv_w_in), bf(conv_w_out)
    dil_qkv_all, dil_out_all = bf(dil_w_qkv), bf(dil_w_out)
    hgrn_in_all, hgrn_out_all = bf(hgrn_w_in), bf(hgrn_w_out)
    rkv_all, rwkv_out_all = bf(rwkv_w_rkv), bf(rwkv_w_out)
    ffn_f32 = lambda idx: ((ffn_w_gate, idx), (ffn_w_up, idx), (ffn_w_down, idx))
    ffn_w = [(bf(w[0][0, 0]), ()) for w in ffn_f32((0, 0))]

    def ffn(xf, i, half):
        last = i == depth - 1 and half == 1
        nxt = None if last else ffn_f32((i, 1) if half == 0 else (i + 1, 0))
        return ffn_half(xf, ffn_norm[i, half], *ffn_w, cast_next=nxt)

    for i in range(depth):
        kind, j = i % N_MIXERS, i // N_MIXERS
        xf, ffn_w = ffn(xf, i, 0)
        if kind == 0:
            xf = conv_mixer(xf, S, mix_norm[i], (conv_in_all, (j,)), conv_w[j], (conv_out_all, (j,)))
        elif kind == 1:
            xf = dilated_mixer(xf, B, S, positions, mix_norm[i], (dil_qkv_all, (j,)), dil_q_gain[j], dil_k_gain[j],
                               (dil_out_all, (j,)))
        elif kind == 2:
            xf = hgrn_mixer(xf, B, S, i, mix_norm[i], (hgrn_in_all, (j,)), hgrn_lb_logits, hgrn_norm[j],
                            (hgrn_out_all, (j,)))
        else:
            xf = rwkv_mixer(xf, B, S, mix_norm[i], rwkv_mu[j], (rkv_all, (j,)), rwkv_w0[j], bf(rwkv_w1[j]),
                            bf(rwkv_w2[j]), rwkv_a0[j], bf(rwkv_a1[j]), bf(rwkv_a2[j]), bf(rwkv_g1[j]),
                            bf(rwkv_g2[j]), rwkv_k_k[j], rwkv_k_a[j], rwkv_r_k[j], rwkv_ln_w[j], rwkv_ln_b[j],
                            (rwkv_out_all, (j,)))
        xf = cross_attention(xf, S, memf, xattn_norm[i], mem_norm[i], (wq_all, (i,)), (wkv_all, (i,)),
                             (wo_all, (i,)), xattn_q_gain[i], xattn_k_gain[i])
        xf, ffn_w = ffn(xf, i, 1)
    return xf.reshape(B, S, D)
```

```python
import functools

import jax
import jax.numpy as jnp
from jax import lax
from jax.experimental import pallas as pl
from jax.experimental.pallas import tpu as pltpu

F32 = jnp.float32
BF16 = jnp.bfloat16

D_MODEL = 2048
DEPTH = 4
N_MIXERS = 4
MEM_LEN = 256
NORM_EPS = 1e-6
NEG_INF = -1e30
ROPE_THETA = 500000.0
DIL_PATTERNS = ((128, 1), (512, 4), (2048, 16))
DIL_GROUPS = 3
DIL_HEADS = 8
DIL_HEAD_DIM = 128
DIL_BLOCK = 128
DIL_PREP_ROWS = 256
DIL_UNITS_IN_FLIGHT = 16
HGRN_CHUNK = 16
HGRN_HEADS = 16
RWKV_HEAD_DIM = 64
RWKV_CHUNK = 64
RWKV_MIX_ROWS = 256
RWKV_GN_EPS = 64e-5
XATTN_HEADS = 4
XATTN_HEAD_DIM = 512

V7X_LANES = 128
V7X_SUBLANES = 8
V7X_VMEM_BYTES = 64 * 2**20
V7X_VMEM_CAP = V7X_VMEM_BYTES - 4 * 2**20


def _cparams(sem, vmem_bytes):
    limit = min(int(vmem_bytes * 1.25) + (4 << 20), V7X_VMEM_CAP)
    return pltpu.CompilerParams(dimension_semantics=sem, vmem_limit_bytes=limit)


def _rms(x, gain):
    return x * lax.rsqrt(jnp.mean(x * x, axis=-1, keepdims=True) + NORM_EPS) * gain


def _dot(a, b):
    return jnp.dot(a, b, preferred_element_type=F32)


def _dot_nt(a, b):
    return lax.dot_general(a, b, (((1,), (1,)), ((), ())), preferred_element_type=F32)


def _dot_tn(a, b):
    return lax.dot_general(a, b, (((0,), (0,)), ((), ())), preferred_element_type=F32)


def _wshape(w):
    arr, lead = w
    return arr.shape[len(lead):]


def _wspec(w, block, tail):
    lead = tuple(w[1])
    return pl.BlockSpec((None,) * len(lead) + tuple(block), lambda *g: lead + tuple(tail(*g)))


def _chunk_sums(x, chunk):
    t, w = x.shape
    rows = lax.broadcasted_iota(jnp.int32, (t, t), 0)
    cols = lax.broadcasted_iota(jnp.int32, (t, t), 1)
    same = _chunk_of(rows, chunk) == _chunk_of(cols, chunk)
    tri = same & (cols <= rows)
    sel = jnp.concatenate([tri.astype(BF16), same.astype(BF16)], axis=0)
    hi = x.astype(BF16)
    r1 = x - hi.astype(F32)
    mid = r1.astype(BF16)
    lo = (r1 - mid.astype(F32)).astype(BF16)
    s = _dot(sel, jnp.concatenate([hi, mid, lo], axis=1))
    s = s[:, :w] + s[:, w:2 * w] + s[:, 2 * w:]
    return s[:t], s[t:], tri


def _chunk_of(idx, chunk):
    return jnp.right_shift(idx, chunk.bit_length() - 1)


def _silu(x):
    return x * jax.nn.sigmoid(x)


def _norm_matmul_body(x_ref, g_ref, w_ref, o_ref, h_scr):
    @pl.when(pl.program_id(1) == 0)
    def _():
        h_scr[...] = _rms(x_ref[...], g_ref[...]).astype(BF16)

    o_ref[...] = _dot(h_scr[...], w_ref[...]).astype(o_ref.dtype)


def norm_matmul(x, gain, w, *, tm=1024, tn=1024, out_dtype=F32):
    M, K = x.shape
    N = _wshape(w)[1]
    tm, tn = min(tm, M), min(tn, N)
    ob = jnp.dtype(out_dtype).itemsize
    vmem = 2 * tm * K * 4 + tm * K * 2 + 2 * K * tn * 2 + 2 * tm * tn * ob
    return pl.pallas_call(
        _norm_matmul_body,
        grid=(M // tm, N // tn),
        in_specs=[
            pl.BlockSpec((tm, K), lambda i, j: (i, 0)),
            pl.BlockSpec((1, K), lambda i, j: (0, 0)),
            _wspec(w, (K, tn), lambda i, j: (0, j)),
        ],
        out_specs=pl.BlockSpec((tm, tn), lambda i, j: (i, j)),
        out_shape=jax.ShapeDtypeStruct((M, N), out_dtype),
        scratch_shapes=[pltpu.VMEM((tm, K), BF16)],
        compiler_params=_cparams(("parallel", "arbitrary"), vmem),
        name="norm_matmul",
    )(x, gain.reshape(1, K), w[0])


def _ffn_body(*refs, n_cast):
    x_ref, g_ref, wg_ref, wu_ref, wd_ref = refs[:5]
    src = refs[5:5 + n_cast]
    o_ref = refs[5 + n_cast]
    dst = refs[6 + n_cast:6 + 2 * n_cast]
    h_scr = refs[-1]

    @pl.when(pl.program_id(1) == 0)
    def _():
        x = x_ref[...]
        h_scr[...] = _rms(x, g_ref[...]).astype(BF16)
        o_ref[...] = x

    h = h_scr[...]
    act = _silu(_dot(h, wg_ref[...])) * _dot(h, wu_ref[...])
    o_ref[...] += 0.5 * _dot(act.astype(BF16), wd_ref[...])
    for s_ref, d_ref in zip(src, dst):
        d_ref[...] = s_ref[...].astype(BF16)


def ffn_half(x, gain, wg, wu, wd, cast_next=None, *, tm=1024, tf=512):
    M, D = x.shape
    F = _wshape(wg)[1]
    tm = min(tm, M)
    gm, gf = M // tm, F // tf
    vmem = 4 * tm * D * 4 + tm * D * 2 + 2 * 3 * D * tf * 2 + 3 * tm * tf * 4
    in_specs = [
        pl.BlockSpec((tm, D), lambda i, f: (i, 0)),
        pl.BlockSpec((1, D), lambda i, f: (0, 0)),
        _wspec(wg, (D, tf), lambda i, f: (0, f)),
        _wspec(wu, (D, tf), lambda i, f: (0, f)),
        _wspec(wd, (tf, D), lambda i, f: (f, 0)),
    ]
    out_specs = [pl.BlockSpec((tm, D), lambda i, f: (i, 0))]
    out_shape = [jax.ShapeDtypeStruct((M, D), F32)]
    cast_in = []
    if cast_next is not None:
        assert D % gm == 0 and F % gf == 0
        dm = D // gm
        for w, shape, block, tail in (
                (cast_next[0], (D, F), (dm, tf), lambda i, f: (i, f)),
                (cast_next[1], (D, F), (dm, tf), lambda i, f: (i, f)),
                (cast_next[2], (F, D), (tf, dm), lambda i, f: (f, i))):
            in_specs.append(_wspec(w, block, tail))
            out_specs.append(pl.BlockSpec(block, tail))
            out_shape.append(jax.ShapeDtypeStruct(shape, BF16))
            cast_in.append(w[0])
            vmem += 2 * block[0] * block[1] * 6
    outs = pl.pallas_call(
        functools.partial(_ffn_body, n_cast=len(cast_in)),
        grid=(gm, gf),
        in_specs=in_specs,
        out_specs=out_specs,
        out_shape=out_shape,
        scratch_shapes=[pltpu.VMEM((tm, D), BF16)],
        compiler_params=_cparams(("parallel", "arbitrary"), vmem),
        name="ffn_half",
    )(x, gain.reshape(1, D), wg[0], wu[0], wd[0], *cast_in)
    return outs[0], [(w, ()) for w in outs[1:]]


def _matmul_res_body(a_ref, w_ref, r_ref, o_ref):
    o_ref[...] = r_ref[...] + _dot(a_ref[...], w_ref[...])


def matmul_residual(a, w, res, *, tm=1024, tn=1024):
    M, K = a.shape
    N = _wshape(w)[1]
    tm = min(tm, M)
    vmem = 2 * tm * K * 2 + 2 * K * tn * 2 + 4 * tm * tn * 4
    return pl.pallas_call(
        _matmul_res_body,
        grid=(M // tm, N // tn),
        in_specs=[
            pl.BlockSpec((tm, K), lambda i, j: (i, 0)),
            _wspec(w, (K, tn), lambda i, j: (0, j)),
            pl.BlockSpec((tm, tn), lambda i, j: (i, j)),
        ],
        out_specs=pl.BlockSpec((tm, tn), lambda i, j: (i, j)),
        out_shape=jax.ShapeDtypeStruct((M, N), F32),
        compiler_params=_cparams(("parallel", "arbitrary"), vmem),
        name="matmul_residual",
    )(a, w[0], res)


def _prologue_matmul_res_body(prologue, n_in, *refs):
    in_refs = refs[:n_in]
    w_ref, r_ref, o_ref, lhs_scr = refs[n_in:]
    row_tile = pl.program_id(0)

    @pl.when(pl.program_id(1) == 0)
    def _():
        prologue(row_tile, *in_refs, lhs_scr)

    o_ref[...] = r_ref[...] + _dot(lhs_scr[...], w_ref[...])


def prologue_matmul_residual(prologue, inputs, in_specs, w, res, *, tm, tn, in_vmem, name):
    M, N = res.shape
    K = _wshape(w)[0]
    vmem = in_vmem + tm * K * 2 + 2 * K * tn * 2 + 4 * tm * tn * 4
    return pl.pallas_call(
        functools.partial(_prologue_matmul_res_body, prologue, len(inputs)),
        grid=(M // tm, N // tn),
        in_specs=list(in_specs) + [
            _wspec(w, (K, tn), lambda i, j: (0, j)),
            pl.BlockSpec((tm, tn), lambda i, j: (i, j)),
        ],
        out_specs=pl.BlockSpec((tm, tn), lambda i, j: (i, j)),
        out_shape=jax.ShapeDtypeStruct((M, N), F32),
        scratch_shapes=[pltpu.VMEM((tm, K), BF16)],
        compiler_params=_cparams(("parallel", "arbitrary"), vmem),
        name=name,
    )(*inputs, w[0], res)


def _conv_prologue(tiles_per_seq, row_tile, b_ref, c_ref, u_ref, cp_ref, up_ref, cw_ref, lhs_scr):
    cu = c_ref[...].astype(F32) * u_ref[...].astype(F32)
    prev = cp_ref[...].astype(F32) * up_ref[...].astype(F32)
    first = (row_tile % tiles_per_seq) == 0
    prev = jnp.where(first, 0.0, prev)
    n_prev = prev.shape[0]
    p1, p2 = prev[n_prev - 1:n_prev, :], prev[n_prev - 2:n_prev - 1, :]
    row = lax.broadcasted_iota(jnp.int32, cu.shape, 0)
    s1 = jnp.where(row == 0, p1, pltpu.roll(cu, 1, 0))
    s2 = jnp.where(row == 0, p2, jnp.where(row == 1, p1, pltpu.roll(cu, 2, 0)))
    w = cw_ref[...]
    y = w[0:1, :] * s2 + w[1:2, :] * s1 + w[2:3, :] * cu
    lhs_scr[...] = (b_ref[...].astype(F32) * y).astype(BF16)


def conv_mixer(x, S, gain, w_in, conv_w, w_out, *, tm=512):
    T, D = x.shape
    tm = min(tm, S)
    bcu = norm_matmul(x, gain, w_in, out_dtype=BF16)
    halo = 2 * V7X_SUBLANES
    rh = tm // halo

    def prev_map(col):
        return lambda i, j: (jnp.maximum(i * rh - 1, 0), col)

    in_specs = [
        pl.BlockSpec((tm, D), lambda i, j: (i, 0)),
        pl.BlockSpec((tm, D), lambda i, j: (i, 1)),
        pl.BlockSpec((tm, D), lambda i, j: (i, 2)),
        pl.BlockSpec((halo, D), prev_map(1)),
        pl.BlockSpec((halo, D), prev_map(2)),
        pl.BlockSpec((3, D), lambda i, j: (0, 0)),
    ]
    return prologue_matmul_residual(
        functools.partial(_conv_prologue, S // tm), (bcu, bcu, bcu, bcu, bcu, conv_w), in_specs, w_out, x,
        tm=tm, tn=1024, in_vmem=2 * 3 * tm * D * 2 + 4 * halo * D * 2 + 3 * tm * D * 4, name="conv_mixer_out")


def _xattn_body(x_ref, kv_ref, xg_ref, qg_ref, kg_ref, wq_ref, wo_ref, o_ref, attn_scr):
    scale = XATTN_HEAD_DIM ** -0.5
    D = XATTN_HEADS * XATTN_HEAD_DIM
    x = x_ref[...]
    q = _dot(_rms(x, xg_ref[...]).astype(BF16), wq_ref[...])
    for h in range(XATTN_HEADS):
        sl = slice(h * XATTN_HEAD_DIM, (h + 1) * XATTN_HEAD_DIM)
        qn = _rms(q[:, sl], qg_ref[...]).astype(BF16)
        kn = _rms(kv_ref[:, sl], kg_ref[...]).astype(BF16)
        v = kv_ref[:, D + h * XATTN_HEAD_DIM:D + (h + 1) * XATTN_HEAD_DIM].astype(BF16)
        s = _dot_nt(qn, kn) * scale
        p = jnp.exp(s - jnp.max(s, axis=-1, keepdims=True))
        l = jnp.sum(p, axis=-1, keepdims=True)
        attn_scr[:, sl] = (_dot(p.astype(BF16), v) / l).astype(BF16)
    o_ref[...] = x + _dot(attn_scr[...], wo_ref[...])


def cross_attention(x, S, mem, xgain, mgain, wq, wkv, wo, q_gain, k_gain, *, tm=512):
    T, D = x.shape
    tm = min(tm, S)
    kv = norm_matmul(mem, mgain, wkv)
    tps = S // tm
    once = pl.Buffered(1)
    vmem = 4 * tm * D * 4 + 2 * MEM_LEN * 2 * D * 4 + 2 * D * D * 2 + tm * D * 2 + 3 * tm * D * 4
    return pl.pallas_call(
        _xattn_body,
        grid=(T // tm,),
        in_specs=[
            pl.BlockSpec((tm, D), lambda i: (i, 0)),
            pl.BlockSpec((MEM_LEN, 2 * D), lambda i: (i // tps, 0)),
            pl.BlockSpec((1, D), lambda i: (0, 0)),
            pl.BlockSpec((1, XATTN_HEAD_DIM), lambda i: (0, 0)),
            pl.BlockSpec((1, XATTN_HEAD_DIM), lambda i: (0, 0)),
            pl.BlockSpec((None,) * len(wq[1]) + (D, D), lambda i: tuple(wq[1]) + (0, 0), pipeline_mode=once),
            pl.BlockSpec((None,) * len(wo[1]) + (D, D), lambda i: tuple(wo[1]) + (0, 0), pipeline_mode=once),
        ],
        out_specs=pl.BlockSpec((tm, D), lambda i: (i, 0)),
        out_shape=jax.ShapeDtypeStruct((T, D), F32),
        scratch_shapes=[pltpu.VMEM((tm, D), BF16)],
        compiler_params=_cparams(("parallel",), vmem),
        name="xattn",
    )(x, kv, xgain.reshape(1, D), q_gain.reshape(1, -1), k_gain.reshape(1, -1), wq[0], wo[0])


def _dil_prep_body(x_ref, pos_ref, invf_ref, qg_ref, kg_ref, o_ref):
    ang = pos_ref[...] * invf_ref[...]
    lane = lax.broadcasted_iota(jnp.int32, ang.shape, 1)
    half = DIL_HEAD_DIM // 8
    cos, sin = jnp.cos(ang), jnp.sin(ang)
    sin_lo = jnp.where(lane < half, -sin, 0.0)
    sin_hi = jnp.where((lane >= half) & (lane < 2 * half), sin, 0.0)
    r = lax.broadcasted_iota(jnp.int32, (2 * DIL_HEAD_DIM, 2 * DIL_HEAD_DIM), 0)
    c = lax.broadcasted_iota(jnp.int32, (2 * DIL_HEAD_DIM, 2 * DIL_HEAD_DIM), 1)
    mean_mat = jnp.where((r < DIL_HEAD_DIM) == (c < DIL_HEAD_DIM), 1.0 / DIL_HEAD_DIM, 0.0).astype(BF16)
    for part, g_ref in ((0, qg_ref), (1, kg_ref)):
        for g in range(DIL_GROUPS):
            gain = g_ref[g:g + 1, :]
            for pair in range(DIL_HEADS // 2):
                col = ((part * DIL_GROUPS + g) * DIL_HEADS + 2 * pair) * DIL_HEAD_DIM
                x2 = x_ref[:, col:col + 2 * DIL_HEAD_DIM]
                inv = lax.rsqrt(_dot((x2 * x2).astype(BF16), mean_mat) + NORM_EPS)
                for hh in range(2):
                    sl = slice(hh * DIL_HEAD_DIM, (hh + 1) * DIL_HEAD_DIM)
                    xn = x2[:, sl] * inv[:, sl] * gain
                    o_ref[:, col + sl.start:col + sl.stop] = (
                        xn * cos + pltpu.roll(xn, DIL_HEAD_DIM - half, 1) * sin_lo
                        + pltpu.roll(xn, half, 1) * sin_hi)


def _dil_attn_body(*refs):
    ins, o_ref, scr = refs[:15], refs[15], refs[16:]
    c = pl.program_id(1)
    scale = DIL_HEAD_DIM ** -0.5
    ii = lax.broadcasted_iota(jnp.int32, (DIL_BLOCK, DIL_BLOCK), 0)
    jj = lax.broadcasted_iota(jnp.int32, (DIL_BLOCK, DIL_BLOCK), 1)
    cur_mask = jj <= ii
    prev_mask = jj >= ii
    ch = o_ref.shape[0]
    for g, (_, dil) in enumerate(DIL_PATTERNS):
        q_ref, k_ref, v_ref, kh_ref, vh_ref = ins[5 * g:5 * g + 5]
        kf, vf, og, lg = scr[4 * g:4 * g + 4]
        hist = DIL_BLOCK * dil
        per_res = DIL_BLOCK + ch // dil
        for r in range(dil):
            for src_h, src_c, dst in ((kh_ref, k_ref, kf), (vh_ref, v_ref, vf)):
                if dil > 1:
                    dst[r * per_res:r * per_res + DIL_BLOCK, :] = src_h[pl.ds(r, DIL_BLOCK, stride=dil), :]
                    dst[r * per_res + DIL_BLOCK:(r + 1) * per_res, :] = src_c[pl.ds(r, ch // dil, stride=dil), :]
                else:
                    dst[0:DIL_BLOCK, :] = src_h[...]
                    dst[DIL_BLOCK:per_res, :] = src_c[...]

        def rows(start):
            return pl.ds(start, DIL_BLOCK, stride=dil) if dil > 1 else pl.ds(start, DIL_BLOCK)

        def kv_rows(blk, r, current):
            return pl.ds(r * per_res + (blk + int(current)) * DIL_BLOCK, DIL_BLOCK)

        units = [(blk, r, blk * hist + r) for blk in range(ch // hist) for r in range(dil)]
        for b0 in range(0, len(units), DIL_UNITS_IN_FLIGHT):
            batch = units[b0:b0 + DIL_UNITS_IN_FLIGHT]
            qv = [q_ref[rows(q0), :].astype(BF16) for _, _, q0 in batch]
            sc = [_dot_nt(q, kf[kv_rows(blk, r, True), :].astype(BF16)) for q, (blk, r, _) in zip(qv, batch)]
            sp = [_dot_nt(q, kf[kv_rows(blk, r, False), :].astype(BF16)) for q, (blk, r, _) in zip(qv, batch)]
            sc = [jnp.where(cur_mask, s * scale, NEG_INF) for s in sc]
            sp = [jnp.where(prev_mask if blk > 0 else prev_mask & (c > 0), s * scale, NEG_INF)
                  for s, (blk, _, _) in zip(sp, batch)]
            m = [jnp.maximum(jnp.max(a, -1, keepdims=True), jnp.max(b, -1, keepdims=True)) for a, b in zip(sc, sp)]
            pc = [jnp.exp(a - mm) for a, mm in zip(sc, m)]
            pp = [jnp.exp(b - mm) for b, mm in zip(sp, m)]
            l = [jnp.sum(a, -1, keepdims=True) + jnp.sum(b, -1, keepdims=True) for a, b in zip(pc, pp)]
            o = [_dot(a.astype(BF16), vf[kv_rows(blk, r, True), :].astype(BF16))
                 + _dot(b.astype(BF16), vf[kv_rows(blk, r, False), :].astype(BF16))
                 for a, b, (blk, r, _) in zip(pc, pp, batch)]
            for oo, ll, mm, (_, _, q0) in zip(o, l, m, batch):
                og[rows(q0), :] = oo / ll
                lg[rows(q0), :] = jnp.broadcast_to(mm + jnp.log(ll), (DIL_BLOCK, DIL_HEAD_DIM))
    l0, l1, l2 = scr[3][...], scr[7][...], scr[11][...]
    mx = jnp.maximum(jnp.maximum(l0, l1), l2)
    e0, e1, e2 = jnp.exp(l0 - mx), jnp.exp(l1 - mx), jnp.exp(l2 - mx)
    o_ref[...] = ((e0 * scr[2][...] + e1 * scr[6][...] + e2 * scr[10][...]) / (e0 + e1 + e2)).astype(o_ref.dtype)


def dilated_mixer(x, B, S, positions, gain, w_qkv, q_gain, k_gain, w_out):
    T, D = x.shape
    nh = DIL_GROUPS * DIL_HEADS
    rot = DIL_HEAD_DIM // 4
    inv_freq = ROPE_THETA ** (-jnp.arange(0, rot, 2, dtype=F32) / rot)
    invf = jnp.concatenate([inv_freq, inv_freq, jnp.zeros((DIL_HEAD_DIM - rot,), F32)]).reshape(1, DIL_HEAD_DIM)
    pos = positions.astype(F32).reshape(T, 1)
    qkv = norm_matmul(x, gain, w_qkv)
    tp = DIL_PREP_ROWS
    qk_cols = 2 * nh * DIL_HEAD_DIM
    qk = pl.pallas_call(
        _dil_prep_body,
        grid=(T // tp,),
        in_specs=[
            pl.BlockSpec((tp, qk_cols), lambda i: (i, 0)),
            pl.BlockSpec((tp, 1), lambda i: (i, 0)),
            pl.BlockSpec((1, DIL_HEAD_DIM), lambda i: (0, 0)),
            pl.BlockSpec((DIL_GROUPS, DIL_HEAD_DIM), lambda i: (0, 0)),
            pl.BlockSpec((DIL_GROUPS, DIL_HEAD_DIM), lambda i: (0, 0)),
        ],
        out_specs=pl.BlockSpec((tp, qk_cols), lambda i: (i, 0)),
        out_shape=jax.ShapeDtypeStruct((T, qk_cols), F32),
        compiler_params=_cparams(("parallel",), 4 * tp * qk_cols * 4 + 2 * tp * V7X_LANES * 4),
        name="dil_qk_prep",
    )(qkv, pos, invf, q_gain, k_gain)

    ch = DIL_BLOCK * DIL_PATTERNS[-1][1]
    n_chunks = S // ch
    inputs, in_specs, scratch = [], [], []
    vmem = 2 * ch * DIL_HEAD_DIM * 2
    for g, (_, dil) in enumerate(DIL_PATTERNS):
        hist = DIL_BLOCK * dil
        per = ch // hist

        def cur_map(col):
            return lambda b, c, h: (b * n_chunks + c, col + h)

        def hist_map(col, per=per):
            return lambda b, c, h: (jnp.maximum((b * n_chunks + c) * per - 1, 0), col + h)

        inputs += [qk, qk, qkv, qk, qkv]
        in_specs += [
            pl.BlockSpec((ch, DIL_HEAD_DIM), cur_map(g * DIL_HEADS)),
            pl.BlockSpec((ch, DIL_HEAD_DIM), cur_map(nh + g * DIL_HEADS)),
            pl.BlockSpec((ch, DIL_HEAD_DIM), cur_map(2 * nh + g * DIL_HEADS)),
            pl.BlockSpec((hist, DIL_HEAD_DIM), hist_map(nh + g * DIL_HEADS)),
            pl.BlockSpec((hist, DIL_HEAD_DIM), hist_map(2 * nh + g * DIL_HEADS)),
        ]
        scratch += [pltpu.VMEM((hist + ch, DIL_HEAD_DIM), F32), pltpu.VMEM((hist + ch, DIL_HEAD_DIM), F32),
                    pltpu.VMEM((ch, DIL_HEAD_DIM), F32), pltpu.VMEM((ch, DIL_HEAD_DIM), F32)]
        vmem += (2 * (3 * ch + 2 * hist) + 2 * (hist + ch) + 2 * ch) * DIL_HEAD_DIM * 4
    o = pl.pallas_call(
        _dil_attn_body,
        grid=(B, n_chunks, DIL_HEADS),
        in_specs=in_specs,
        out_specs=pl.BlockSpec((ch, DIL_HEAD_DIM), lambda b, c, h: (b * n_chunks + c, h)),
        out_shape=jax.ShapeDtypeStruct((T, DIL_HEADS * DIL_HEAD_DIM), BF16),
        scratch_shapes=scratch,
        compiler_params=_cparams(("parallel", "arbitrary", "arbitrary"), vmem),
        name="dil_attention",
    )(*inputs)
    return matmul_residual(o, w_out, x)


def _hgrn_body(q_ref, f_ref, i_ref, gt_ref, lbl_ref, gain_ref, o_ref, st_scr, *, layer):
    @pl.when(pl.program_id(2) == 0)
    def _():
        st_scr[...] = jnp.zeros_like(st_scr)

    tt = q_ref.shape[0]
    dh = gain_ref.shape[1]
    heads = [slice(h * dh, (h + 1) * dh) for h in range(q_ref.shape[1] // dh)]
    lbl = lbl_ref[...]
    e = jnp.exp(lbl - jnp.max(lbl, axis=0, keepdims=True))
    p = e / jnp.sum(e, axis=0, keepdims=True)
    lb = jnp.sum(p[1:layer + 1, :], axis=0, keepdims=True)
    forget = lb + (1.0 - lb) * jax.nn.sigmoid(f_ref[...].astype(F32))
    k = 1.0 - forget
    gl = jnp.log(forget)
    a_cum, a_tot, tri = _chunk_sums(gl, HGRN_CHUNK)
    q_dec = (q_ref[...].astype(F32) * jnp.exp(a_cum)).astype(BF16)
    k_in = (k * jnp.exp(-a_cum)).astype(BF16)
    k_end = (k * jnp.exp(a_tot - a_cum)).astype(BF16)
    v = i_ref[...]
    dec = jnp.exp(a_tot)
    att = [jnp.where(tri, _dot_nt(q_dec[:, hs], k_in[:, hs]), 0.0).astype(BF16) for hs in heads]
    o = [_dot(a, v[:, hs]) for a, hs in zip(att, heads)]
    chunks = [slice(c * HGRN_CHUNK, (c + 1) * HGRN_CHUNK) for c in range(tt // HGRN_CHUNK)]
    upd = [[_dot_tn(v[sl, hs], k_end[sl, hs]) for sl in chunks] for hs in heads]
    st = [st_scr[h] for h in range(len(heads))]
    inter = [[] for _ in heads]
    for ci, sl in enumerate(chunks):
        for h, hs in enumerate(heads):
            inter[h].append(_dot_nt(q_dec[sl, hs], st[h].astype(BF16)))
            st[h] = st[h] * dec[sl.start:sl.start + 1, hs] + upd[h][ci]
    gt = gt_ref[...].astype(F32)
    for h, hs in enumerate(heads):
        st_scr[h] = st[h]
        oh = o[h] + jnp.concatenate(inter[h], axis=0)
        o_ref[:, hs] = (_rms(oh, gain_ref[...]) * _silu(gt[:, hs])).astype(o_ref.dtype)


def hgrn_mixer(x, B, S, layer, gain, w_in, lb_logits, norm_gain, w_out, *, tt=256, heads_per_step=8):
    T, D = x.shape
    dh = D // HGRN_HEADS
    proj = norm_matmul(x, gain, w_in, out_dtype=BF16)
    tt = min(tt, S)
    nt = S // tt
    ng = HGRN_HEADS // heads_per_step
    wd = heads_per_step * dh

    def part(pidx):
        return pl.BlockSpec((tt, wd), lambda b, h, s: (b * nt + s, pidx * ng + h))

    o = pl.pallas_call(
        functools.partial(_hgrn_body, layer=layer),
        grid=(B, ng, nt),
        in_specs=[part(0), part(1), part(2), part(3),
                  pl.BlockSpec((DEPTH, wd), lambda b, h, s: (0, h)),
                  pl.BlockSpec((1, dh), lambda b, h, s: (0, 0))],
        out_specs=pl.BlockSpec((tt, wd), lambda b, h, s: (b * nt + s, h)),
        out_shape=jax.ShapeDtypeStruct((T, D), BF16),
        scratch_shapes=[pltpu.VMEM((heads_per_step, dh, dh), F32)],
        compiler_params=_cparams(("parallel", "parallel", "arbitrary"), 24 * tt * wd * 4 + 8 * tt * tt * 4),
        name="hgrn_core",
    )(proj, proj, proj, proj, lb_logits, norm_gain.reshape(1, dh))
    return matmul_residual(o, w_out, x)


def _rwkv_shift_mix(x_ref, xp_ref, gn_ref, first):
    gn = gn_ref[...]
    h = _rms(x_ref[...], gn)
    last = _rms(xp_ref[...], gn)[V7X_SUBLANES - 1:V7X_SUBLANES, :]
    last = jnp.where(first, 0.0, last)
    row = lax.broadcasted_iota(jnp.int32, h.shape, 0)
    return h, jnp.where(row == 0, last, pltpu.roll(h, 1, 0)) - h


def _rwkv_rkv_body(x_ref, xp_ref, gn_ref, mu_ref, wrkv_ref, rkv_ref, mix_scr, *, tiles_per_seq, n_col_tiles):
    n = pl.program_id(1)
    first = (pl.program_id(0) % tiles_per_seq) == 0

    @pl.when(n == 0)
    def _():
        gn, mu = gn_ref[...], mu_ref[...]
        last = jnp.where(first, 0.0, _rms(xp_ref[...], gn)[V7X_SUBLANES - 1:V7X_SUBLANES, :])
        sub = min(RWKV_MIX_ROWS, x_ref.shape[0])
        for r0 in range(0, x_ref.shape[0], sub):
            h = _rms(x_ref[r0:r0 + sub, :], gn)
            row = lax.broadcasted_iota(jnp.int32, h.shape, 0)
            d = jnp.where(row == 0, last, pltpu.roll(h, 1, 0)) - h
            for m in range(3):
                mix_scr[m, r0:r0 + sub, :] = (h + d * mu[m:m + 1, :]).astype(BF16)
            last = h[sub - 1:sub, :]

    rkv_ref[...] = _dot(mix_scr[n // n_col_tiles], wrkv_ref[...])


def _rwkv_lora_body(x_ref, xp_ref, gn_ref, mu_ref, w0_ref, w1_ref, w2_ref, a0_ref, a1_ref, a2_ref, g1_ref, g2_ref,
                    lw_ref, a_ref, g_ref, *, tiles_per_seq):
    h, d = _rwkv_shift_mix(x_ref, xp_ref, gn_ref, (pl.program_id(0) % tiles_per_seq) == 0)
    mu = mu_ref[...]
    xw = (h + d * mu[3:4, :]).astype(BF16)
    xa = (h + d * mu[4:5, :]).astype(BF16)
    xg = (h + d * mu[5:6, :]).astype(BF16)
    z = -(w0_ref[...] + _dot(jnp.tanh(_dot(xw, w1_ref[...])).astype(BF16), w2_ref[...]))
    softplus = jnp.maximum(z, 0.0) + jnp.log1p(jnp.exp(-jnp.abs(z)))
    lw_ref[...] = -jnp.exp(-softplus - 0.5)
    a_ref[...] = jax.nn.sigmoid(a0_ref[...] + _dot(_dot(xa, a1_ref[...]).astype(BF16), a2_ref[...])).astype(a_ref.dtype)
    g_ref[...] = _dot(jax.nn.sigmoid(_dot(xg, g1_ref[...])).astype(BF16), g2_ref[...]).astype(g_ref.dtype)


def _rwkv_core_body(r_ref, k_ref, v_ref, lw_ref, a_ref, g_ref, kk_ref, ka_ref, rk_ref, lnw_ref, lnb_ref,
                    o_ref, h_scr):
    @pl.when(pl.program_id(2) == 0)
    def _():
        h_scr[...] = jnp.zeros_like(h_scr)

    tt = r_ref.shape[0]
    C, N = RWKV_CHUNK, RWKV_HEAD_DIM
    lw = lw_ref[...]
    g_cum, g_tot, _ = _chunk_sums(lw, C)
    r, k, v, a = r_ref[...], k_ref[...], v_ref[...], a_ref[...].astype(F32)
    pairs = [slice(p * 2 * N, (p + 1) * 2 * N) for p in range(r.shape[1] // (2 * N))]
    left = lax.broadcasted_iota(jnp.int32, (tt, 2 * N), 1) < N

    def head_sum(t):
        return jnp.concatenate(
            [jnp.where(left, jnp.sum(jnp.where(left, t[:, ps], 0.0), -1, keepdims=True),
                       jnp.sum(jnp.where(left, 0.0, t[:, ps]), -1, keepdims=True)) for ps in pairs], axis=1)

    kk = k * kk_ref[...]
    kk = kk * lax.rsqrt(jnp.maximum(head_sum(kk * kk), 1e-24))
    kmod = k * (1.0 + (a - 1.0) * ka_ref[...])
    bv = kk * a
    e_neg = jnp.exp(-g_cum)
    e_end = jnp.exp(g_tot - g_cum)
    a_t = (-kk) * jnp.exp(g_cum - lw)
    r_t = r * jnp.exp(g_cum)
    k_t, b_t = kmod * e_neg, bv * e_neg
    k_h, b_h = kmod * e_end, bv * e_end
    dec = jnp.exp(g_tot)
    def key_lanes(t):
        out = []
        for ps in pairs:
            out += [jnp.where(left, t[:, ps], 0.0), jnp.where(left, pltpu.roll(t[:, ps], N, 1), 0.0)]
        return out

    a_k, r_k, bt_k, kt_k, bh_k, kh_k, dec_k = map(key_lanes, (a_t, r_t, b_t, k_t, b_h, k_h, dec))
    v_v = []
    for ps in pairs:
        v_v += [jnp.where(left, 0.0, pltpu.roll(v[:, ps], N, 1)), jnp.where(left, 0.0, v[:, ps])]
    n_heads = 2 * len(pairs)
    i2 = lax.broadcasted_iota(jnp.int32, (C, 2 * C), 0)
    lane2 = lax.broadcasted_iota(jnp.int32, (C, 2 * C), 1)
    lo = lane2 < C
    t2 = jnp.bitwise_and(lane2, C - 1)
    strict, incl = t2 < i2, t2 <= i2
    eye_hi = jnp.where(lane2 == i2 + C, 1.0, 0.0)
    zeros16 = jnp.zeros((C, 2 * C), BF16)

    nc = tt // C
    units = [(hh, c) for hh in range(n_heads) for c in range(nc)]

    def cut(per_head, u):
        return per_head[u[0]][u[1] * C:(u[1] + 1) * C, :]

    v_c = [cut(v_v, u).astype(BF16) for u in units]
    prod = [_dot_nt(jnp.concatenate([cut(a_k, u), cut(r_k, u)], axis=0).astype(BF16),
                    jnp.concatenate([cut(bt_k, u), cut(kt_k, u)], axis=0).astype(BF16))
            for u in units]
    n_abk = [jnp.where(strict, p[:C], 0.0) for p in prod]
    t_rbk = [jnp.where(incl, p[C:], 0.0).astype(BF16) for p in prod]
    n_lo = [jnp.where(lo, n, 0.0) for n in n_abk]
    s1 = [_dot(n.astype(BF16), jnp.concatenate([nl.astype(BF16), vv], axis=0))
          for n, nl, vv in zip(n_abk, n_lo, v_c)]
    z = [jnp.where(lo, s, 0.0) + pltpu.roll(nl, C, 1) + eye_hi for s, nl in zip(s1, n_lo)]
    for _ in range(5):
        z16 = [zz.astype(BF16) for zz in z]
        z = [_dot(zz16[:, :C], zz16) + jnp.where(lo, 0.0, zz) for zz, zz16 in zip(z, z16)]
    w0 = [(cut(a_k, u) + jnp.where(lo, 0.0, s)).astype(BF16) for u, s in zip(units, s1)]
    au = [_dot(zz.astype(BF16), jnp.concatenate([zeros16, w], axis=0)).astype(BF16)
          for zz, w in zip(z, w0)]
    auv = [jnp.concatenate([x, vv], axis=0) for x, vv in zip(au, v_c)]
    ry = [_dot(t, x) for t, x in zip(t_rbk, auv)]
    th = [_dot_tn(x, jnp.concatenate([cut(bh_k, u), cut(kh_k, u)], axis=0).astype(BF16))
          for x, u in zip(auv, units)]
    r_p = [(cut(r_k, u) + jnp.where(lo, y, 0.0)).astype(BF16) for u, y in zip(units, ry)]

    ht = [h_scr[hh] for hh in range(n_heads)]
    ys = [[] for _ in range(n_heads)]
    for c in range(nc):
        for hh in range(n_heads):
            i = hh * nc + c
            ht16 = ht[hh].astype(BF16)
            ys[hh].append(_dot_nt(r_p[i], ht16) + ry[i][:, C:])
            ht[hh] = (ht[hh] * dec_k[hh][c * C:c * C + 1, :] + _dot(ht16[:, :C], th[i][:C].astype(BF16))
                      + th[i][C:])
    y_heads = []
    for hh in range(n_heads):
        h_scr[hh] = ht[hh]
        y = jnp.concatenate(ys[hh], axis=0)
        mean = jnp.mean(y, -1, keepdims=True)
        var = jnp.mean(jnp.square(y - mean), -1, keepdims=True)
        y_heads.append((y - mean) * lax.rsqrt(var + RWKV_GN_EPS))
    yn = jnp.concatenate(y_heads, axis=1) * lnw_ref[...] + lnb_ref[...]
    bonus = head_sum(r * kmod * rk_ref[...]) * v
    o_ref[...] = ((yn + bonus) * g_ref[...].astype(F32)).astype(o_ref.dtype)


def rwkv_mixer(x, B, S, gain, mu, w_rkv, w0, w1, w2, a0, a1, a2, g1, g2, k_k, k_a, r_k, ln_w, ln_b, w_out,
               *, tm=1024, tn=1024, tl=256, tt=256, heads_per_step=8):
    T, D = x.shape
    tm, tl, tt = min(tm, S), min(tl, S), min(tt, S)
    nct = D // tn
    row = lambda i, n: (0, 0)

    def prev_rows(t):
        return lambda i, *_: (jnp.maximum(i * (t // V7X_SUBLANES) - 1, 0), 0)

    rkv = pl.pallas_call(
        functools.partial(_rwkv_rkv_body, tiles_per_seq=S // tm, n_col_tiles=nct),
        grid=(T // tm, 3 * nct),
        in_specs=[
            pl.BlockSpec((tm, D), lambda i, n: (i, 0)),
            pl.BlockSpec((V7X_SUBLANES, D), prev_rows(tm)),
            pl.BlockSpec((1, D), row),
            pl.BlockSpec((6, D), row),
            _wspec(w_rkv, (None, D, tn), lambda i, n: (n // nct, 0, n % nct)),
        ],
        out_specs=pl.BlockSpec((tm, tn), lambda i, n: (i, n)),
        out_shape=jax.ShapeDtypeStruct((T, 3 * D), F32),
        scratch_shapes=[pltpu.VMEM((3, tm, D), BF16)],
        compiler_params=_cparams(("parallel", "arbitrary"),
                                 2 * tm * D * 4 + 3 * tm * D * 2 + 2 * D * tn * 2 + 2 * tm * tn * 4 + 3 * tm * D * 4),
        name="rwkv_rkv",
    )(x, x, gain.reshape(1, D), mu, w_rkv[0])

    lora = w1.shape[1]
    pad = (-lora) % V7X_LANES
    w1p, a1p = jnp.pad(w1, ((0, 0), (0, pad))), jnp.pad(a1, ((0, 0), (0, pad)))
    w2p, a2p = jnp.pad(w2, ((0, pad), (0, 0))), jnp.pad(a2, ((0, pad), (0, 0)))
    lp, gl = lora + pad, g1.shape[1]
    one = lambda i: (0, 0)
    tok_l = pl.BlockSpec((tl, D), lambda i: (i, 0))
    lw, a, g = pl.pallas_call(
        functools.partial(_rwkv_lora_body, tiles_per_seq=S // tl),
        grid=(T // tl,),
        in_specs=[
            tok_l,
            pl.BlockSpec((V7X_SUBLANES, D), prev_rows(tl)),
            pl.BlockSpec((1, D), one),
            pl.BlockSpec((6, D), one),
            pl.BlockSpec((1, D), one), pl.BlockSpec((D, lp), one), pl.BlockSpec((lp, D), one),
            pl.BlockSpec((1, D), one), pl.BlockSpec((D, lp), one), pl.BlockSpec((lp, D), one),
            pl.BlockSpec((D, gl), one), pl.BlockSpec((gl, D), one),
        ],
        out_specs=[tok_l, tok_l, tok_l],
        out_shape=[jax.ShapeDtypeStruct((T, D), F32), jax.ShapeDtypeStruct((T, D), BF16),
                   jax.ShapeDtypeStruct((T, D), BF16)],
        compiler_params=_cparams(("parallel",), 14 * tl * D * 4 + 4 * (2 * D * lp + D * gl) * 2),
        name="rwkv_lora",
    )(x, x, gain.reshape(1, D), mu, w0.reshape(1, D), w1p, w2p, a0.reshape(1, D), a1p, a2p, g1, g2)

    nt = S // tt
    pw = heads_per_step * RWKV_HEAD_DIM
    npair = D // pw

    def tok(col0):
        return pl.BlockSpec((tt, pw), lambda b, p, s: (b * nt + s, col0 + p))

    par = pl.BlockSpec((1, pw), lambda b, p, s: (0, p))
    o = pl.pallas_call(
        _rwkv_core_body,
        grid=(B, npair, nt),
        in_specs=[tok(0), tok(npair), tok(2 * npair), tok(0), tok(0), tok(0), par, par, par, par, par],
        out_specs=pl.BlockSpec((tt, pw), lambda b, p, s: (b * nt + s, p)),
        out_shape=jax.ShapeDtypeStruct((T, D), BF16),
        scratch_shapes=[pltpu.VMEM((heads_per_step, RWKV_HEAD_DIM, 2 * RWKV_HEAD_DIM), F32)],
        compiler_params=_cparams(("parallel", "parallel", "arbitrary"), 40 * tt * pw * 4 + 8 * tt * tt * 4),
        name="rwkv_core",
    )(rkv, rkv, rkv, lw, a, g, k_k.reshape(1, D), k_a.reshape(1, D), r_k.reshape(1, D),
      ln_w.reshape(1, D), ln_b.reshape(1, D))
    return matmul_residual(o, w_out, x)


def kernel(x, mem, positions, ffn_norm, ffn_w_gate, ffn_w_up, ffn_w_down, mix_norm, xattn_norm, mem_norm, xattn_wq, xattn_wkv, xattn_wo, xattn_q_gain, xattn_k_gain, conv_w_in, conv_w, conv_w_out, dil_w_qkv, dil_q_gain, dil_k_gain, dil_w_out, hgrn_w_in, hgrn_lb_logits, hgrn_norm, hgrn_w_out, rwkv_mu, rwkv_w_rkv, rwkv_w0, rwkv_w1, rwkv_w2, rwkv_a0, rwkv_a1, rwkv_a2, rwkv_g1, rwkv_g2, rwkv_k_k, rwkv_k_a, rwkv_r_k, rwkv_ln_w, rwkv_ln_b, rwkv_w_out):
    B, S, D = x.shape
    assert D == D_MODEL and S % (DIL_BLOCK * DIL_PATTERNS[-1][1]) == 0
    depth = ffn_norm.shape[0]
    xf = x.reshape(B * S, D)
    memf = mem.reshape(B * MEM_LEN, D)
    bf = lambda w: w.astype(BF16)
    wq_all, wkv_all, wo_all = bf(xattn_wq), bf(xattn_wkv), bf(xattn_wo)
    conv_in_all, conv_out_all = bf(conv_w_in), bf(conv_w_out)
    dil_qkv_all, dil_out_all = bf(dil_w_qkv), bf(dil_w_out)
    hgrn_in_all, hgrn_out_all = bf(hgrn_w_in), bf(hgrn_w_out)
    rkv_all, rwkv_out_all = bf(rwkv_w_rkv), bf(rwkv_w_out)
    ffn_f32 = lambda idx: ((ffn_w_gate, idx), (ffn_w_up, idx), (ffn_w_down, idx))
    ffn_w = [(bf(w[0][0, 0]), ()) for w in ffn_f32((0, 0))]

    def ffn(xf, i, half):
        last = i == depth - 1 and half == 1
        nxt = None if last else ffn_f32((i, 1) if half == 0 else (i + 1, 0))
        return ffn_half(xf, ffn_norm[i, half], *ffn_w, cast_next=nxt)

    for i in range(depth):
        kind, j = i % N_MIXERS, i // N_MIXERS
        xf, ffn_w = ffn(xf, i, 0)
        if kind == 0:
            xf = conv_mixer(xf, S, mix_norm[i], (conv_in_all, (j,)), conv_w[j], (conv_out_all, (j,)))
        elif kind == 1:
            xf = dilated_mixer(xf, B, S, positions, mix_norm[i], (dil_qkv_all, (j,)), dil_q_gain[j], dil_k_gain[j],
                               (dil_out_all, (j,)))
        elif kind == 2:
            xf = hgrn_mixer(xf, B, S, i, mix_norm[i], (hgrn_in_all, (j,)), hgrn_lb_logits, hgrn_norm[j],
                            (hgrn_out_all, (j,)))
        else:
            xf = rwkv_mixer(xf, B, S, mix_norm[i], rwkv_mu[j], (rkv_all, (j,)), rwkv_w0[j], bf(rwkv_w1[j]),
                            bf(rwkv_w2[j]), rwkv_a0[j], bf(rwkv_a1[j]), bf(rwkv_a2[j]), bf(rwkv_g1[j]),
                            bf(rwkv_g2[j]), rwkv_k_k[j], rwkv_k_a[j], rwkv_r_k[j], rwkv_ln_w[j], rwkv_ln_b[j],
                            (rwkv_out_all, (j,)))
        xf = cross_attention(xf, S, memf, xattn_norm[i], mem_norm[i], (wq_all, (i,)), (wkv_all, (i,)),
                             (wo_all, (i,)), xattn_q_gain[i], xattn_k_gain[i])
        xf, ffn_w = ffn(xf, i, 1)
    return xf.reshape(B, S, D)
```
